```python
import math
import jax, jax.numpy as jnp
from jax import lax
import numpy as np

D_MODEL = 1024
BATCH = 16
SEQ = 256
DEPTH = 1
DEC_BATCH = 8
DEC_SEQ = 4096
PAST_LEN = 256

GRID_W = 64
DIFF_HEADS = 8
HEAD_DIM = 64
ATTN_WIDTH = DIFF_HEADS * 2 * HEAD_DIM
ROPE_HALF = HEAD_DIM // 4
ROPE_BASE = 10000.0
Q_BLOCK = 128
GMLP_WIDTH = 1024
GMLP_GROUPS = 4
CHUNK = 128
N_EXPERTS = 32
TOP_K = 4
D_EXPERT = 1024
SWIGLU_LIMIT = 7.0
SWIGLU_ALPHA = 1.702
MOE_BLOCK = 256
LN_EPS = 1e-5
IN_COLS = 3 * ATTN_WIDTH + 2 * GMLP_WIDTH + 2 * D_MODEL
IN_SPLITS = (ATTN_WIDTH, 2 * ATTN_WIDTH, 3 * ATTN_WIDTH,
             3 * ATTN_WIDTH + GMLP_WIDTH, 3 * ATTN_WIDTH + 2 * GMLP_WIDTH,
             3 * ATTN_WIDTH + 2 * GMLP_WIDTH + D_MODEL)

kernel_name = 'diffattn_gmlp_moe_deepnorm_prefix_dit'


def layer_norm(x, g, b):
    xf = x.astype(jnp.float32)
    mu = jnp.mean(xf, axis=-1, keepdims=True)
    var = jnp.mean(jnp.square(xf - mu), axis=-1, keepdims=True)
    return ((xf - mu) * lax.rsqrt(var + LN_EPS) * g + b).astype(x.dtype)


def rms_norm(x, g):
    xf = x.astype(jnp.float32)
    return (xf * lax.rsqrt(jnp.mean(jnp.square(xf), axis=-1, keepdims=True) + LN_EPS) * g).astype(x.dtype)


def axial_rope_tables(n_tokens, dtype):
    rows = n_tokens // GRID_W
    r = jnp.repeat(jnp.arange(rows, dtype=jnp.float32), GRID_W)
    col = jnp.tile(jnp.arange(GRID_W, dtype=jnp.float32), rows)
    inv = jnp.power(ROPE_BASE, -jnp.arange(ROPE_HALF, dtype=jnp.float32) / ROPE_HALF)
    ang_r = r[:, None] * inv
    ang_c = col[:, None] * inv
    ang = jnp.concatenate([ang_r, ang_r, ang_c, ang_c], axis=-1)
    return jnp.cos(ang).astype(dtype), jnp.sin(ang).astype(dtype)


def apply_axial_rope(x, cos, sin):
    xs = x.reshape(x.shape[:-1] + (2, 2, ROPE_HALF))
    rot = jnp.stack([-xs[..., 1, :], xs[..., 0, :]], axis=-2).reshape(x.shape)
    return x * cos[:, None, :] + rot * sin[:, None, :]


def differential_attention(q, k, v, lam):
    B, H, L = q.shape[:3]
    nb = L // Q_BLOCK
    qb = jnp.moveaxis(q.reshape(B, H, nb, Q_BLOCK, 2, HEAD_DIM), 2, 0)
    scale = HEAD_DIM ** -0.5

    def attend(q_blk):
        s = jnp.einsum('bhqid,bhkid->bihqk', q_blk, k).astype(jnp.float32) * scale
        p = jax.nn.softmax(s, axis=-1)
        w = (p[:, 0] - lam * p[:, 1]).astype(v.dtype)
        return jnp.einsum('bhqk,bhke->bhqe', w, v)

    out = lax.map(attend, qb)
    return jnp.moveaxis(out, 0, 2).reshape(B, H, L, 2 * HEAD_DIM)


def token_mixers(h, l, p, ctx_k, ctx_v):
    B, L, _ = h.shape
    z = h @ p['w_in'][l]
    q, k, v, ug, vg, ga, gb = jnp.split(z, IN_SPLITS, axis=-1)
    q = q.reshape(B, L, DIFF_HEADS, 2, HEAD_DIM).transpose(0, 2, 1, 3, 4)
    k = k.reshape(B, L, DIFF_HEADS, 2, HEAD_DIM).transpose(0, 2, 1, 3, 4)
    v = v.reshape(B, L, DIFF_HEADS, 2 * HEAD_DIM).transpose(0, 2, 1, 3)
    if ctx_k is None:
        k_all, v_all = k, v
    else:
        cos, sin = axial_rope_tables(L, h.dtype)
        q = apply_axial_rope(q, cos, sin)
        k = apply_axial_rope(k, cos, sin)
        k_all = jnp.concatenate([k, ctx_k], axis=2)
        v_all = jnp.concatenate([v, ctx_v], axis=2)
    lam_init = 0.8 - 0.6 * math.exp(-0.3 * l)
    lam = (jnp.exp(jnp.sum((p['lambda_q1'][l] * p['lambda_k1'][l]).astype(jnp.float32)))
           - jnp.exp(jnp.sum((p['lambda_q2'][l] * p['lambda_k2'][l]).astype(jnp.float32)))
           + lam_init)
    a = differential_attention(q, k_all, v_all, lam)
    a = rms_norm(a, p['subln_w'][l]) * (1.0 - lam_init)
    a = a.transpose(0, 2, 1, 3).reshape(B, L, ATTN_WIDTH)
    ug = jax.nn.gelu(ug, approximate=False)
    vg = layer_norm(jax.nn.gelu(vg, approximate=False), p['gmlp_ln_g'][l], p['gmlp_ln_b'][l])
    vc = vg.reshape(B, L // CHUNK, CHUNK, GMLP_GROUPS, GMLP_WIDTH // GMLP_GROUPS)
    sp = jnp.einsum('gpq,bcqgf->bcpgf', p['w_spatial'][l], vc) + p['b_spatial'][l].T[:, :, None]
    gm = ug * sp.reshape(B, L, GMLP_WIDTH)
    merged = (jax.nn.sigmoid(ga + p['b_gate'][l, 0]) * (a @ p['w_pa'][l])
              + jax.nn.sigmoid(gb + p['b_gate'][l, 1]) * (gm @ p['w_pb'][l]))
    return merged @ p['w_o'][l], k, v


def moe_ffn(h, l, p):
    B, L, D = h.shape
    xt = h.reshape(-1, D)
    T = xt.shape[0]
    logits = (xt @ p['w_router'][l] + p['b_router'][l]).astype(jnp.float32)
    top_val, top_idx = lax.top_k(logits, TOP_K)
    gate = jax.nn.softmax(top_val, axis=-1).astype(h.dtype)
    A = T * TOP_K
    flat_e = top_idx.reshape(A)
    order = jnp.argsort(flat_e)
    sorted_e = flat_e[order]
    tok = order // TOP_K
    counts = jnp.bincount(flat_e, length=N_EXPERTS)
    padded = (counts + MOE_BLOCK - 1) // MOE_BLOCK * MOE_BLOCK
    start = jnp.cumsum(counts) - counts
    pend = jnp.cumsum(padded)
    pstart = pend - padded
    dest = pstart[sorted_e] + (jnp.arange(A) - start[sorted_e])
    n_blocks = -(-A // MOE_BLOCK) + N_EXPERTS
    n_rows = n_blocks * MOE_BLOCK
    row_tok = jnp.zeros((n_rows,), jnp.int32).at[dest].set(tok.astype(jnp.int32))
    blk_exp = jnp.minimum(jnp.searchsorted(pend, jnp.arange(n_blocks) * MOE_BLOCK, side='right'),
                          N_EXPERTS - 1)
    xb = xt[row_tok].reshape(n_blocks, MOE_BLOCK, D)
    w_up, b_up, w_down, b_down = p['w_up'][l], p['b_up'][l], p['w_down'][l], p['b_down'][l]

    def expert_block(args):
        xblk, e = args
        hu = xblk @ w_up[e] + b_up[e]
        glu, lin = jnp.split(hu, 2, axis=-1)
        glu = jnp.minimum(glu, SWIGLU_LIMIT)
        lin = jnp.clip(lin, -SWIGLU_LIMIT, SWIGLU_LIMIT)
        act = glu * jax.nn.sigmoid(SWIGLU_ALPHA * glu) * (lin + 1.0)
        return act @ w_down[e] + b_down[e]

    yb = lax.map(expert_block, (xb, blk_exp)).reshape(n_rows, D)
    y_assign = yb[dest] * gate.reshape(A)[order][:, None]
    out = jax.ops.segment_sum(y_assign, tok, num_segments=T)
    return out.reshape(B, L, D)


def trunk_layer(x, cond, l, p, ctx_k, ctx_v):
    alpha = (2.0 * DEPTH) ** 0.25
    mod = (jax.nn.silu(cond) @ p['w_ada'][l] + p['b_ada'][l])[:, None, :]
    sh1, sc1, g1, sh2, sc2, g2 = jnp.split(mod, 6, axis=-1)
    mix, k, v = token_mixers(x * (1.0 + sc1) + sh1, l, p, ctx_k, ctx_v)
    x = layer_norm(alpha * x + g1 * mix, p['ln1_g'][l], p['ln1_b'][l])
    ffn = moe_ffn(x * (1.0 + sc2) + sh2, l, p)
    x = layer_norm(alpha * x + g2 * ffn, p['ln2_g'][l], p['ln2_b'][l])
    return x, k, v


def setup_inputs(seed: int = 0) -> dict:
    key = jax.random.key(seed)
    ks = jax.random.split(key, 40)
    beta = (8.0 * DEPTH) ** -0.25
    f32 = jnp.float32

    def nrm(i, shape, scale):
        return jax.random.normal(ks[i], shape, f32) * scale

    col_scale = jnp.concatenate([jnp.ones((2 * ATTN_WIDTH,), f32),
                                 jnp.full((ATTN_WIDTH,), beta, f32),
                                 jnp.ones((2 * GMLP_WIDTH + 2 * D_MODEL,), f32)])
    return {
        'x_prompt': nrm(0, (BATCH, SEQ, D_MODEL), 1.0),
        'x_sample': nrm(1, (DEC_BATCH, DEC_SEQ, D_MODEL), 1.0),
        'c': nrm(2, (DEC_BATCH, D_MODEL), 1.0),
        'cache_k': nrm(3, (DEC_BATCH, DEPTH, DIFF_HEADS, PAST_LEN, 2, HEAD_DIM), 1.0),
        'cache_v': nrm(4, (DEC_BATCH, DEPTH, DIFF_HEADS, PAST_LEN, 2 * HEAD_DIM), beta),
        'c_ctx': nrm(5, (D_MODEL,), 0.5),
        'w_ada': nrm(6, (DEPTH, D_MODEL, 6 * D_MODEL), 0.5 * D_MODEL ** -0.5),
        'b_ada': nrm(7, (DEPTH, 6 * D_MODEL), 0.02),
        'w_in': nrm(8, (DEPTH, D_MODEL, IN_COLS), D_MODEL ** -0.5) * col_scale,
        'lambda_q1': nrm(9, (DEPTH, HEAD_DIM), 0.1),
        'lambda_k1': nrm(10, (DEPTH, HEAD_DIM), 0.1),
        'lambda_q2': nrm(11, (DEPTH, HEAD_DIM), 0.1),
        'lambda_k2': nrm(12, (DEPTH, HEAD_DIM), 0.1),
        'subln_w': 1.0 + nrm(13, (DEPTH, 2 * HEAD_DIM), 0.02),
        'gmlp_ln_g': 1.0 + nrm(14, (DEPTH, GMLP_WIDTH), 0.02),
        'gmlp_ln_b': nrm(15, (DEPTH, GMLP_WIDTH), 0.02),
        'w_spatial': nrm(16, (DEPTH, GMLP_GROUPS, CHUNK, CHUNK), CHUNK ** -0.5),
        'b_spatial': 1.0 + nrm(17, (DEPTH, GMLP_GROUPS, CHUNK), 0.02),
        'b_gate': nrm(18, (DEPTH, 2, D_MODEL), 0.02),
        'w_pa': nrm(19, (DEPTH, ATTN_WIDTH, D_MODEL), ATTN_WIDTH ** -0.5),
        'w_pb': nrm(20, (DEPTH, GMLP_WIDTH, D_MODEL), GMLP_WIDTH ** -0.5),
        'w_o': nrm(21, (DEPTH, D_MODEL, D_MODEL), beta * D_MODEL ** -0.5),
        'ln1_g': 1.0 + nrm(22, (DEPTH, D_MODEL), 0.02),
        'ln1_b': nrm(23, (DEPTH, D_MODEL), 0.02),
        'w_router': nrm(24, (DEPTH, D_MODEL, N_EXPERTS), D_MODEL ** -0.5),
        'b_router': nrm(25, (DEPTH, N_EXPERTS), 0.01),
        'w_up': nrm(26, (DEPTH, N_EXPERTS, D_MODEL, 2 * D_EXPERT), D_MODEL ** -0.5),
        'b_up': nrm(27, (DEPTH, N_EXPERTS, 2 * D_EXPERT), 0.02),
        'w_down': nrm(28, (DEPTH, N_EXPERTS, D_EXPERT, D_MODEL), beta * D_EXPERT ** -0.5),
        'b_down': nrm(29, (DEPTH, N_EXPERTS, D_MODEL), 0.02),
        'ln2_g': 1.0 + nrm(30, (DEPTH, D_MODEL), 0.02),
        'ln2_b': nrm(31, (DEPTH, D_MODEL), 0.02),
    }


def reference(x_prompt, x_sample, c, cache_k, cache_v, c_ctx, w_ada, b_ada, w_in,
              lambda_q1, lambda_k1, lambda_q2, lambda_k2, subln_w, gmlp_ln_g, gmlp_ln_b,
              w_spatial, b_spatial, b_gate, w_pa, w_pb, w_o, ln1_g, ln1_b,
              w_router, b_router, w_up, b_up, w_down, b_down, ln2_g, ln2_b):
    p = {'w_ada': w_ada, 'b_ada': b_ada, 'w_in': w_in,
         'lambda_q1': lambda_q1, 'lambda_k1': lambda_k1,
         'lambda_q2': lambda_q2, 'lambda_k2': lambda_k2,
         'subln_w': subln_w, 'gmlp_ln_g': gmlp_ln_g, 'gmlp_ln_b': gmlp_ln_b,
         'w_spatial': w_spatial, 'b_spatial': b_spatial, 'b_gate': b_gate,
         'w_pa': w_pa, 'w_pb': w_pb, 'w_o': w_o, 'ln1_g': ln1_g, 'ln1_b': ln1_b,
         'w_router': w_router, 'b_router': b_router, 'w_up': w_up, 'b_up': b_up,
         'w_down': w_down, 'b_down': b_down, 'ln2_g': ln2_g, 'ln2_b': ln2_b}
    xp = x_prompt
    ks_list = []
    vs_list = []
    for l in range(DEPTH):
        xp, k_l, v_l = trunk_layer(xp, c_ctx[None, :], l, p, None, None)
        ks_list.append(k_l)
        vs_list.append(v_l)
    new_cache_k = jnp.stack(ks_list, axis=1)
    new_cache_v = jnp.stack(vs_list, axis=1)
    xs = x_sample
    for l in range(DEPTH):
        xs, _, _ = trunk_layer(xs, c, l, p, cache_k[:, l], cache_v[:, l])
    return (xp, xs, new_cache_k, new_cache_v)
```

```python
import functools
import math

import jax
import jax.numpy as jnp
from jax import lax
from jax.experimental import pallas as pl
from jax.experimental.pallas import tpu as pltpu

F32 = jnp.float32
BF16 = jnp.bfloat16

D_MODEL = 1024
N_CTX_B = 16
CTX_LEN = 256
N_LAT_B = 8
LAT_LEN = 4096
PAST_LEN = 256
GRID_W = 64
N_HEADS = 8
HEAD_DIM = 64
HEAD_W = 2 * HEAD_DIM
ROPE_HALF = HEAD_DIM // 4
ROPE_BASE = 10000.0
GMLP_GROUPS = 4
CHUNK = 128
N_EXPERTS = 32
TOP_K = 4
D_EXPERT = 1024
SWIGLU_LIMIT = 7.0
SWIGLU_ALPHA = 1.702
MOE_BLOCK = 256
LN_EPS = 1e-5
DEPTH = 1
N_SEG = 7

TM = 256
NT_CTX = N_CTX_B * CTX_LEN // TM
NT_LAT = N_LAT_B * LAT_LEN // TM
NT = NT_CTX + NT_LAT
T_ALL = NT * TM
LAT_TILES = LAT_LEN // TM
LANES = 128
NEG_BIG = -1e30
VMEM_LIMIT = 56 * 1024 * 1024

ALPHA = (2.0 * DEPTH) ** 0.25
LAM_INIT = 0.8 - 0.6 * math.exp(-0.3 * 0)
Q_SCALE = HEAD_DIM ** -0.5 * math.log2(math.e)


def _ctx_idx(i):
    return jnp.minimum(i, NT_CTX - 1)


def _lat_idx(i):
    return jnp.maximum(i - NT_CTX, 0)


def _mod_row(i):
    return jnp.where(i < NT_CTX, 0, 1 + (i - NT_CTX) // LAT_TILES)


def _layer_norm(x, g, b):
    mu = jnp.mean(x, axis=-1, keepdims=True)
    xc = x - mu
    var = jnp.mean(xc * xc, axis=-1, keepdims=True)
    return xc * lax.rsqrt(var + LN_EPS) * g + b


def _gelu(x):
    return 0.5 * x * (1.0 + lax.erf(x * (1.0 / math.sqrt(2.0))))


def _params(sem):
    return pltpu.CompilerParams(dimension_semantics=sem, vmem_limit_bytes=VMEM_LIMIT)


def _mod_kernel(c_ref, w_ref, b_ref, o_ref):
    c = c_ref[...]
    s = c * jax.nn.sigmoid(c)
    o_ref[...] = jnp.dot(s, w_ref[...], preferred_element_type=F32,
                         precision=lax.Precision.HIGHEST) + b_ref[...]


def _modulation(cond, w_ada, b_ada):
    n = cond.shape[0]
    return pl.pallas_call(
        _mod_kernel,
        grid=(6,),
        in_specs=[pl.BlockSpec((n, D_MODEL), lambda j: (0, 0)),
                  pl.BlockSpec((D_MODEL, D_MODEL), lambda j: (0, j)),
                  pl.BlockSpec((1, D_MODEL), lambda j: (0, j))],
        out_specs=pl.BlockSpec((n, D_MODEL), lambda j: (0, j)),
        out_shape=jax.ShapeDtypeStruct((n, 6 * D_MODEL), F32),
        compiler_params=_params(("arbitrary",)),
        name="adaln_mod",
    )(cond, w_ada, b_ada)


def _proj_kernel(xp_ref, xs_ref, mod_ref, w_ref, cos_ref, sin_ref, bg_ref, lg_ref, lb_ref,
                 q_ref, k_ref, v_ref, nk_ref, nv_ref, ug_ref, vg_ref, sa_ref, sb_ref):
    i = pl.program_id(0)
    is_ctx = i < NT_CTX
    x = jnp.where(is_ctx, xp_ref[...], xs_ref[...])
    mod = mod_ref[0]
    sh1 = mod[:, 0:D_MODEL]
    sc1 = mod[:, D_MODEL:2 * D_MODEL]
    h = (x * (1.0 + sc1) + sh1).astype(BF16)

    def seg(s):
        return jnp.dot(h, w_ref[:, s * D_MODEL:(s + 1) * D_MODEL], preferred_element_type=F32)

    cos = cos_ref[...]
    sin = sin_ref[...]
    lane = lax.broadcasted_iota(jnp.int32, (TM, LANES), 1)
    first = (lane % (2 * ROPE_HALF)) < ROPE_HALF

    def rope(zh):
        up = pltpu.roll(zh, LANES - ROPE_HALF, 1)
        dn = pltpu.roll(zh, ROPE_HALF, 1)
        return zh * cos + jnp.where(first, up, dn) * sin

    zq = seg(0)
    for hd in range(N_HEADS):
        q_ref[0, hd] = (rope(zq[:, hd * HEAD_W:(hd + 1) * HEAD_W]) * Q_SCALE).astype(BF16)
    zk = seg(1)
    for hd in range(N_HEADS):
        k_ref[0, hd] = rope(zk[:, hd * HEAD_W:(hd + 1) * HEAD_W]).astype(BF16)
    zv = seg(2)
    for hd in range(N_HEADS):
        v_ref[0, hd] = zv[:, hd * HEAD_W:(hd + 1) * HEAD_W].astype(BF16)

    @pl.when(is_ctx)
    def _():
        for hd in range(N_HEADS):
            nk_ref[0, hd] = zk[:, hd * HEAD_W:(hd + 1) * HEAD_W]
            nv_ref[0, hd] = zv[:, hd * HEAD_W:(hd + 1) * HEAD_W]

    ug_ref[...] = _gelu(seg(3)).astype(BF16)
    vg_ref[...] = _layer_norm(_gelu(seg(4)), lg_ref[...], lb_ref[...]).astype(BF16)
    sa_ref[...] = jax.nn.sigmoid(seg(5) + bg_ref[0:1, :]).astype(BF16)
    sb_ref[...] = jax.nn.sigmoid(seg(6) + bg_ref[1:2, :]).astype(BF16)


def _input_projection(xp, xs, mod3, w_in, cos_t, sin_t, b_gate, ln_g, ln_b):
    tile = lambda i: (i, 0)
    head_blk = (1, N_HEADS, TM, HEAD_W)
    tok_spec = pl.BlockSpec((TM, D_MODEL), tile)
    return pl.pallas_call(
        _proj_kernel,
        grid=(NT,),
        in_specs=[
            pl.BlockSpec((TM, D_MODEL), lambda i: (_ctx_idx(i), 0)),
            pl.BlockSpec((TM, D_MODEL), lambda i: (_lat_idx(i), 0)),
            pl.BlockSpec((1, 1, 6 * D_MODEL), lambda i: (_mod_row(i), 0, 0)),
            pl.BlockSpec((D_MODEL, N_SEG * D_MODEL), lambda i: (0, 0)),
            pl.BlockSpec((TM, LANES), lambda i: (jnp.where(i < NT_CTX, 0, 1 + (i - NT_CTX) % LAT_TILES), 0)),
            pl.BlockSpec((TM, LANES), lambda i: (jnp.where(i < NT_CTX, 0, 1 + (i - NT_CTX) % LAT_TILES), 0)),
            pl.BlockSpec((2, D_MODEL), lambda i: (0, 0)),
            pl.BlockSpec((1, D_MODEL), lambda i: (0, 0)),
            pl.BlockSpec((1, D_MODEL), lambda i: (0, 0)),
        ],
        out_specs=[
            pl.BlockSpec(head_blk, lambda i: (i, 0, 0, 0)),
            pl.BlockSpec(head_blk, lambda i: (i, 0, 0, 0)),
            pl.BlockSpec(head_blk, lambda i: (i, 0, 0, 0)),
            pl.BlockSpec(head_blk, lambda i: (_ctx_idx(i), 0, 0, 0)),
            pl.BlockSpec(head_blk, lambda i: (_ctx_idx(i), 0, 0, 0)),
            tok_spec, tok_spec, tok_spec, tok_spec,
        ],
        out_shape=[
            jax.ShapeDtypeStruct((NT, N_HEADS, TM, HEAD_W), BF16),
            jax.ShapeDtypeStruct((NT, N_HEADS, TM, HEAD_W), BF16),
            jax.ShapeDtypeStruct((NT, N_HEADS, TM, HEAD_W), BF16),
            jax.ShapeDtypeStruct((NT_CTX, N_HEADS, TM, HEAD_W), F32),
            jax.ShapeDtypeStruct((NT_CTX, N_HEADS, TM, HEAD_W), F32),
            jax.ShapeDtypeStruct((T_ALL, D_MODEL), BF16),
            jax.ShapeDtypeStruct((T_ALL, D_MODEL), BF16),
            jax.ShapeDtypeStruct((T_ALL, D_MODEL), BF16),
            jax.ShapeDtypeStruct((T_ALL, D_MODEL), BF16),
        ],
        compiler_params=_params(("arbitrary",)),
        name="input_projection",
    )(xp, xs, mod3, w_in, cos_t, sin_t, b_gate, ln_g, ln_b)


def _attn_kernel(n_main, has_ctx, lam_ref, q_ref, k_ref, v_ref, *rest):
    if has_ctx:
        ck_ref, cv_ref, sub_ref, o_ref, kall, vall, s_ref, w_ref = rest
    else:
        sub_ref, o_ref, kall, vall, s_ref, w_ref = rest
    n_keys = kall.shape[0]
    main = n_main * TM

    @pl.when(pl.program_id(2) == 0)
    def _():
        kall[0:main, :] = k_ref[:, 0].reshape(main, HEAD_W)
        vall[0:main, :] = v_ref[:, 0].reshape(main, HEAD_W)
        if has_ctx:
            kall[main:n_keys, :] = ck_ref[0, 0].astype(BF16)
            vall[main:n_keys, :] = cv_ref[0, 0].astype(BF16)

    q = q_ref[0, 0]
    lane = lax.broadcasted_iota(jnp.int32, (TM, HEAD_W), 1)
    zero = jnp.zeros_like(q)
    qq = jnp.concatenate([jnp.where(lane < HEAD_DIM, q, zero),
                          jnp.where(lane >= HEAD_DIM, q, zero)], axis=0)
    s_ref[...] = lax.dot_general(qq, kall[...], (((1,), (1,)), ((), ())),
                                 preferred_element_type=F32)
    lam = lam_ref[0]
    rows = 16

    def body(r, carry):
        r0 = pl.multiple_of(r * rows, rows)
        s1 = s_ref[pl.ds(r0, rows), :]
        s2 = s_ref[pl.ds(TM + r0, rows), :]
        e1 = jnp.exp2(s1 - jnp.max(s1, axis=-1, keepdims=True))
        e2 = jnp.exp2(s2 - jnp.max(s2, axis=-1, keepdims=True))
        c1 = 1.0 / jnp.sum(e1, axis=-1, keepdims=True)
        c2 = lam / jnp.sum(e2, axis=-1, keepdims=True)
        w_ref[pl.ds(r0, rows), :] = (e1 * c1 - e2 * c2).astype(BF16)
        return carry

    lax.fori_loop(0, TM // rows, body, 0)
    a = jnp.dot(w_ref[...], vall[...], preferred_element_type=F32)
    ms = jnp.mean(a * a, axis=-1, keepdims=True)
    o_ref[...] = (a * lax.rsqrt(ms + LN_EPS) * sub_ref[...]).astype(BF16)


def _attention(lam, q_all, k_all, v_all, ctx_k, ctx_v, subw, latent):
    if latent:
        n_b, n_q, n_main, base = N_LAT_B, LAT_TILES, LAT_TILES, NT_CTX
        n_keys = LAT_LEN + PAST_LEN
    else:
        n_b, n_q, n_main, base = N_CTX_B, 1, 1, 0
        n_keys = CTX_LEN
    q_spec = pl.BlockSpec((1, 1, TM, HEAD_W), lambda b, h, t: (base + b * n_q + t, h, 0, 0))
    kv_spec = pl.BlockSpec((n_main, 1, TM, HEAD_W), lambda b, h, t: (base // n_main + b, h, 0, 0))
    in_specs = [pl.BlockSpec(memory_space=pltpu.SMEM), q_spec, kv_spec, kv_spec]
    args = [lam, q_all, k_all, v_all]
    if latent:
        c_spec = pl.BlockSpec((1, 1, PAST_LEN, HEAD_W), lambda b, h, t: (b, h, 0, 0))
        in_specs += [c_spec, c_spec]
        args += [ctx_k, ctx_v]
    in_specs.append(pl.BlockSpec((1, HEAD_W), lambda b, h, t: (0, 0)))
    args.append(subw)
    return pl.pallas_call(
        functools.partial(_attn_kernel, n_main, latent),
        grid=(n_b, N_HEADS, n_q),
        in_specs=in_specs,
        out_specs=pl.BlockSpec((TM, HEAD_W), lambda b, h, t: (b * n_q + t, h)),
        out_shape=jax.ShapeDtypeStruct((n_b * n_q * TM, D_MODEL), BF16),
        scratch_shapes=[pltpu.VMEM((n_keys, HEAD_W), BF16),
                        pltpu.VMEM((n_keys, HEAD_W), BF16),
                        pltpu.VMEM((2 * TM, n_keys), F32),
                        pltpu.VMEM((TM, n_keys), BF16)],
        compiler_params=_params(("arbitrary", "arbitrary", "arbitrary")),
        name="diff_attention_lat" if latent else "diff_attention_ctx",
    )(*args)


def _mix_kernel(ac_ref, al_ref, ug_ref, vg_ref, sa_ref, sb_ref, xp_ref, xs_ref, mod_ref,
                ws_ref, bs_ref, wpa_ref, wpb_ref, wo_ref, g_ref, b_ref, wrh_ref, wrl_ref, br_ref,
                x1_ref, h2_ref, ti_ref, tg_ref):
    i = pl.program_id(0)
    is_ctx = i < NT_CTX
    a = jnp.where(is_ctx, ac_ref[...], al_ref[...])
    x = jnp.where(is_ctx, xp_ref[...], xs_ref[...])
    mod = mod_ref[0]
    g1 = mod[:, 2 * D_MODEL:3 * D_MODEL]
    sh2 = mod[:, 3 * D_MODEL:4 * D_MODEL]
    sc2 = mod[:, 4 * D_MODEL:5 * D_MODEL]

    gw = D_MODEL // GMLP_GROUPS
    chunks = []
    for c in range(TM // CHUNK):
        groups = []
        for g in range(GMLP_GROUPS):
            vc = vg_ref[c * CHUNK:(c + 1) * CHUNK, g * gw:(g + 1) * gw]
            groups.append(jnp.dot(ws_ref[g], vc, preferred_element_type=F32))
        chunks.append(jnp.concatenate(groups, axis=1) + bs_ref[...])
    sp = jnp.concatenate(chunks, axis=0)
    gm = (ug_ref[...].astype(F32) * sp).astype(BF16)

    pa = jnp.dot(a, wpa_ref[...], preferred_element_type=F32)
    pb = jnp.dot(gm, wpb_ref[...], preferred_element_type=F32)
    merged = (sa_ref[...].astype(F32) * pa + sb_ref[...].astype(F32) * pb).astype(BF16)
    mix = jnp.dot(merged, wo_ref[...], preferred_element_type=F32)
    x1 = _layer_norm(ALPHA * x + g1 * mix, g_ref[...], b_ref[...])
    x1_ref[...] = x1
    h2 = x1 * (1.0 + sc2) + sh2
    hi = h2.astype(BF16)
    h2_ref[...] = hi
    lo = (h2 - hi.astype(F32)).astype(BF16)
    logits = (jnp.dot(hi, wrh_ref[...], preferred_element_type=F32)
              + jnp.dot(lo, wrh_ref[...], preferred_element_type=F32)
              + jnp.dot(hi, wrl_ref[...], preferred_element_type=F32)) + br_ref[...]
    lane = lax.broadcasted_iota(jnp.int32, (TM, LANES), 1).astype(F32)
    vals, idxs = [], []
    for _ in range(TOP_K):
        mx = jnp.max(logits, axis=-1, keepdims=True)
        ix = jnp.min(jnp.where(logits == mx, lane, float(LANES)), axis=-1, keepdims=True)
        vals.append(mx)
        idxs.append(ix)
        logits = jnp.where(lane == ix, NEG_BIG * 2.0, logits)
    es = [jnp.exp(v - vals[0]) for v in vals]
    inv = 1.0 / (es[0] + es[1] + es[2] + es[3])
    ti = jnp.zeros((TM, LANES), F32)
    tg = jnp.zeros((TM, LANES), F32)
    for k in range(TOP_K):
        ti = jnp.where(lane == float(k), idxs[k], ti)
        tg = jnp.where(lane == float(k), es[k] * inv, tg)
    ti_ref[...] = ti.astype(jnp.int32)
    tg_ref[...] = tg


def _mix(a_ctx, a_lat, ug, vg, sa, sb, xp, xs, mod3, ws, bs, wpa, wpb, wo, g, b, wrh, wrl, br):
    tile = lambda i: (i, 0)
    full2 = lambda i: (0, 0)
    tok = pl.BlockSpec((TM, D_MODEL), tile)
    ctx = pl.BlockSpec((TM, D_MODEL), lambda i: (_ctx_idx(i), 0))
    lat = pl.BlockSpec((TM, D_MODEL), lambda i: (_lat_idx(i), 0))
    wsq = pl.BlockSpec((D_MODEL, D_MODEL), full2)
    vec = pl.BlockSpec((1, D_MODEL), full2)
    return pl.pallas_call(
        _mix_kernel,
        grid=(NT,),
        in_specs=[ctx, lat, tok, tok, tok, tok, ctx, lat,
                  pl.BlockSpec((1, 1, 6 * D_MODEL), lambda i: (_mod_row(i), 0, 0)),
                  pl.BlockSpec((GMLP_GROUPS, CHUNK, CHUNK), lambda i: (0, 0, 0)),
                  pl.BlockSpec((CHUNK, D_MODEL), full2),
                  wsq, wsq, wsq, vec, vec,
                  pl.BlockSpec((D_MODEL, LANES), full2),
                  pl.BlockSpec((D_MODEL, LANES), full2),
                  pl.BlockSpec((1, LANES), full2)],
        out_specs=[tok, tok, pl.BlockSpec((TM, LANES), tile), pl.BlockSpec((TM, LANES), tile)],
        out_shape=[jax.ShapeDtypeStruct((T_ALL, D_MODEL), F32),
                   jax.ShapeDtypeStruct((T_ALL, D_MODEL), BF16),
                   jax.ShapeDtypeStruct((T_ALL, LANES), jnp.int32),
                   jax.ShapeDtypeStruct((T_ALL, LANES), F32)],
        compiler_params=_params(("arbitrary",)),
        name="mix_ln1_router",
    )(a_ctx, a_lat, ug, vg, sa, sb, xp, xs, mod3, ws, bs, wpa, wpb, wo, g, b, wrh, wrl, br)


def _expert_kernel(be_ref, nu_ref, x_ref, wu_ref, bu_ref, wd_ref, bd_ref, y_ref):
    @pl.when(pl.program_id(0) < nu_ref[0])
    def _():
        hu = jnp.dot(x_ref[...], wu_ref[0], preferred_element_type=F32) + bu_ref[0]
        glu = jnp.minimum(hu[:, :D_EXPERT], SWIGLU_LIMIT)
        lin = jnp.clip(hu[:, D_EXPERT:], -SWIGLU_LIMIT, SWIGLU_LIMIT)
        act = glu * jax.nn.sigmoid(SWIGLU_ALPHA * glu) * (lin + 1.0)
        y_ref[...] = jnp.dot(act.astype(BF16), wd_ref[0], preferred_element_type=F32) + bd_ref[0]


def _experts(blk_exp, n_used, xb, w_up, b_up, w_down, b_down):
    n_blocks = xb.shape[0] // MOE_BLOCK
    grid_spec = pltpu.PrefetchScalarGridSpec(
        num_scalar_prefetch=2,
        grid=(n_blocks,),
        in_specs=[pl.BlockSpec((MOE_BLOCK, D_MODEL), lambda i, be, nu: (i, 0)),
                  pl.BlockSpec((1, D_MODEL, 2 * D_EXPERT), lambda i, be, nu: (be[i], 0, 0)),
                  pl.BlockSpec((1, 1, 2 * D_EXPERT), lambda i, be, nu: (be[i], 0, 0)),
                  pl.BlockSpec((1, D_EXPERT, D_MODEL), lambda i, be, nu: (be[i], 0, 0)),
                  pl.BlockSpec((1, 1, D_MODEL), lambda i, be, nu: (be[i], 0, 0))],
        out_specs=pl.BlockSpec((MOE_BLOCK, D_MODEL), lambda i, be, nu: (i, 0)),
    )
    return pl.pallas_call(
        _expert_kernel,
        grid_spec=grid_spec,
        out_shape=jax.ShapeDtypeStruct((n_blocks * MOE_BLOCK, D_MODEL), F32),
        compiler_params=_params(("arbitrary",)),
        name="expert_ffn",
    )(blk_exp, n_used, xb, w_up, b_up, w_down, b_down)


def _final_kernel(x1_ref, f_ref, mod_ref, g_ref, b_ref, yp_ref, ys_ref):
    i = pl.program_id(0)
    g2 = mod_ref[0][:, 5 * D_MODEL:6 * D_MODEL]
    y = _layer_norm(ALPHA * x1_ref[...] + g2 * f_ref[...], g_ref[...], b_ref[...])

    @pl.when(i < NT_CTX)
    def _():
        yp_ref[...] = y

    @pl.when(i >= NT_CTX)
    def _():
        ys_ref[...] = y


def _final(x1, ffn, mod3, g, b):
    tok = pl.BlockSpec((TM, D_MODEL), lambda i: (i, 0))
    vec = pl.BlockSpec((1, D_MODEL), lambda i: (0, 0))
    return pl.pallas_call(
        _final_kernel,
        grid=(NT,),
        in_specs=[tok, tok, pl.BlockSpec((1, 1, 6 * D_MODEL), lambda i: (_mod_row(i), 0, 0)), vec, vec],
        out_specs=[pl.BlockSpec((TM, D_MODEL), lambda i: (_ctx_idx(i), 0)),
                   pl.BlockSpec((TM, D_MODEL), lambda i: (_lat_idx(i), 0))],
        out_shape=[jax.ShapeDtypeStruct((NT_CTX * TM, D_MODEL), F32),
                   jax.ShapeDtypeStruct((NT_LAT * TM, D_MODEL), F32)],
        compiler_params=_params(("arbitrary",)),
        name="residual_ln2",
    )(x1, ffn, mod3, g, b)


def _rope_tables():
    t = jnp.arange(LAT_LEN)
    r = (t // GRID_W).astype(F32)
    col = (t % GRID_W).astype(F32)
    inv = jnp.power(ROPE_BASE, -jnp.arange(ROPE_HALF, dtype=F32) / ROPE_HALF)
    ang_r = r[:, None] * inv
    ang_c = col[:, None] * inv
    ang = jnp.concatenate([ang_r, ang_r, ang_c, ang_c], axis=-1)
    cos = jnp.tile(jnp.cos(ang), (1, 2))
    sin = jnp.tile(jnp.sin(ang), (1, 2))
    sign = jnp.where((jnp.arange(LANES) % (2 * ROPE_HALF)) < ROPE_HALF, -1.0, 1.0).astype(F32)
    cos = jnp.concatenate([jnp.ones((TM, LANES), F32), cos], axis=0)
    sin = jnp.concatenate([jnp.zeros((TM, LANES), F32), sin * sign], axis=0)
    return cos, sin


def _routing(top_idx):
    n_tok = top_idx.shape[0]
    n_assign = n_tok * TOP_K
    flat_e = top_idx.reshape(n_assign)
    order = jnp.argsort(flat_e)
    sorted_e = flat_e[order]
    tok = order // TOP_K
    counts = jnp.bincount(flat_e, length=N_EXPERTS)
    padded = (counts + MOE_BLOCK - 1) // MOE_BLOCK * MOE_BLOCK
    start = jnp.cumsum(counts) - counts
    pend = jnp.cumsum(padded)
    pstart = pend - padded
    dest = pstart[sorted_e] + (jnp.arange(n_assign) - start[sorted_e])
    n_blocks = -(-n_assign // MOE_BLOCK) + N_EXPERTS
    row_tok = jnp.zeros((n_blocks * MOE_BLOCK,), jnp.int32).at[dest].set(tok.astype(jnp.int32))
    blk_exp = jnp.minimum(jnp.searchsorted(pend, jnp.arange(n_blocks) * MOE_BLOCK, side='right'),
                          N_EXPERTS - 1).astype(jnp.int32)
    dest_tk = jnp.zeros((n_assign,), jnp.int32).at[order].set(dest.astype(jnp.int32))
    n_used = (pend[-1] // MOE_BLOCK).astype(jnp.int32).reshape(1)
    return row_tok, blk_exp, dest_tk.reshape(n_tok, TOP_K), n_used


def kernel(x_prompt, x_sample, c, cache_k, cache_v, c_ctx, w_ada, b_ada, w_in, lambda_q1, lambda_k1, lambda_q2, lambda_k2, subln_w, gmlp_ln_g, gmlp_ln_b, w_spatial, b_spatial, b_gate, w_pa, w_pb, w_o, ln1_g, ln1_b, w_router, b_router, w_up, b_up, w_down, b_down, ln2_g, ln2_b):
    l = 0
    xp = x_prompt.reshape(N_CTX_B * CTX_LEN, D_MODEL)
    xs = x_sample.reshape(N_LAT_B * LAT_LEN, D_MODEL)

    cond = jnp.concatenate([c_ctx[None, :], c, jnp.zeros((16 - 1 - N_LAT_B, D_MODEL), F32)], axis=0)
    mod3 = _modulation(cond, w_ada[l], b_ada[l][None, :]).reshape(16, 1, 6 * D_MODEL)

    cos_t, sin_t = _rope_tables()
    (q_all, k_all, v_all, new_k, new_v, ug, vg, sa, sb) = _input_projection(
        xp, xs, mod3, w_in[l].astype(BF16), cos_t, sin_t, b_gate[l],
        gmlp_ln_g[l][None, :], gmlp_ln_b[l][None, :])

    lam = (jnp.exp(jnp.sum(lambda_q1[l] * lambda_k1[l])) - jnp.exp(jnp.sum(lambda_q2[l] * lambda_k2[l]))
           + LAM_INIT).reshape(1).astype(F32)
    subw = (subln_w[l] * (1.0 - LAM_INIT))[None, :]
    ctx_k = cache_k[:, l].reshape(N_LAT_B, N_HEADS, PAST_LEN, HEAD_W)
    ctx_v = cache_v[:, l]
    a_ctx = _attention(lam, q_all, k_all, v_all, None, None, subw, latent=False)
    a_lat = _attention(lam, q_all, k_all, v_all, ctx_k, ctx_v, subw, latent=True)

    bs_full = jnp.repeat(b_spatial[l].T, D_MODEL // GMLP_GROUPS, axis=1)
    wr = jnp.pad(w_router[l], ((0, 0), (0, LANES - N_EXPERTS)))
    wr_hi = wr.astype(BF16)
    wr_lo = (wr - wr_hi.astype(F32)).astype(BF16)
    br = jnp.concatenate([b_router[l], jnp.full((LANES - N_EXPERTS,), NEG_BIG, F32)])[None, :]
    x1, h2, top_i, top_g = _mix(
        a_ctx, a_lat, ug, vg, sa, sb, xp, xs, mod3, w_spatial[l].astype(BF16), bs_full,
        w_pa[l].astype(BF16), w_pb[l].astype(BF16), w_o[l].astype(BF16),
        ln1_g[l][None, :], ln1_b[l][None, :], wr_hi, wr_lo, br)

    top_idx = top_i[:, :TOP_K]
    gate = top_g[:, :TOP_K]
    row_tok, blk_exp, dest_tk, n_used = _routing(top_idx)
    xb = h2[row_tok]
    yb = _experts(blk_exp, n_used, xb, w_up[l].astype(BF16), b_up[l][:, None, :],
                  w_down[l].astype(BF16), b_down[l][:, None, :])
    ffn = jnp.sum(yb[dest_tk] * gate[:, :, None], axis=1)

    y_prompt, y_sample = _final(x1, ffn, mod3, ln2_g[l][None, :], ln2_b[l][None, :])
    return (y_prompt.reshape(N_CTX_B, CTX_LEN, D_MODEL),
            y_sample.reshape(N_LAT_B, LAT_LEN, D_MODEL),
            new_k.reshape(N_CTX_B, DEPTH, N_HEADS, CTX_LEN, 2, HEAD_DIM),
            new_v.reshape(N_CTX_B, DEPTH, N_HEADS, CTX_LEN, HEAD_W))
```

```python
import functools
import math

import jax
import jax.numpy as jnp
from jax import lax
from jax.experimental import pallas as pl
from jax.experimental.pallas import tpu as pltpu

F32 = jnp.float32
BF16 = jnp.bfloat16

D_MODEL = 1024
N_CTX_B = 16
CTX_LEN = 256
N_LAT_B = 8
LAT_LEN = 4096
PAST_LEN = 256
GRID_W = 64
N_HEADS = 8
HEAD_DIM = 64
HEAD_W = 2 * HEAD_DIM
ROPE_HALF = HEAD_DIM // 4
ROPE_BASE = 10000.0
GMLP_GROUPS = 4
CHUNK = 128
N_EXPERTS = 32
TOP_K = 4
D_EXPERT = 1024
SWIGLU_LIMIT = 7.0
SWIGLU_ALPHA = 1.702
MOE_BLOCK = 256
LN_EPS = 1e-5
DEPTH = 1
N_SEG = 7

TM = 256
NT_CTX = N_CTX_B * CTX_LEN // TM
NT_LAT = N_LAT_B * LAT_LEN // TM
NT = NT_CTX + NT_LAT
T_ALL = NT * TM
LAT_TILES = LAT_LEN // TM
LANES = 128
NEG_BIG = -1e30
VMEM_LIMIT = 56 * 1024 * 1024

ALPHA = (2.0 * DEPTH) ** 0.25
LAM_INIT = 0.8 - 0.6 * math.exp(-0.3 * 0)
Q_SCALE = HEAD_DIM ** -0.5 * math.log2(math.e)


def _ctx_idx(i):
    return jnp.minimum(i, NT_CTX - 1)


def _lat_idx(i):
    return jnp.maximum(i - NT_CTX, 0)


def _mod_row(i):
    return jnp.where(i < NT_CTX, 0, 1 + (i - NT_CTX) // LAT_TILES)


def _layer_norm(x, g, b):
    mu = jnp.mean(x, axis=-1, keepdims=True)
    xc = x - mu
    var = jnp.mean(xc * xc, axis=-1, keepdims=True)
    return xc * lax.rsqrt(var + LN_EPS) * g + b


def _gelu(x):
    return 0.5 * x * (1.0 + lax.erf(x * (1.0 / math.sqrt(2.0))))


def _params(sem):
    return pltpu.CompilerParams(dimension_semantics=sem, vmem_limit_bytes=VMEM_LIMIT)


def _mod_kernel(c_ref, w_ref, b_ref, o_ref):
    c = c_ref[...]
    s = c * jax.nn.sigmoid(c)
    o_ref[...] = jnp.dot(s, w_ref[...], preferred_element_type=F32,
                         precision=lax.Precision.HIGHEST) + b_ref[...]


def _modulation(cond, w_ada, b_ada):
    n = cond.shape[0]
    return pl.pallas_call(
        _mod_kernel,
        grid=(6,),
        in_specs=[pl.BlockSpec((n, D_MODEL), lambda j: (0, 0)),
                  pl.BlockSpec((D_MODEL, D_MODEL), lambda j: (0, j)),
                  pl.BlockSpec((1, D_MODEL), lambda j: (0, j))],
        out_specs=pl.BlockSpec((n, D_MODEL), lambda j: (0, j)),
        out_shape=jax.ShapeDtypeStruct((n, 6 * D_MODEL), F32),
        compiler_params=_params(("arbitrary",)),
        name="adaln_mod",
    )(cond, w_ada, b_ada)


def _proj_kernel(xp_ref, xs_ref, mod_ref, w_ref, cos_ref, sin_ref, bg_ref, lg_ref, lb_ref,
                 q_ref, k_ref, v_ref, nk_ref, nv_ref, ug_ref, vg_ref, sa_ref, sb_ref):
    i = pl.program_id(0)
    is_ctx = i < NT_CTX
    x = jnp.where(is_ctx, xp_ref[...], xs_ref[...])
    mod = mod_ref[0]
    sh1 = mod[:, 0:D_MODEL]
    sc1 = mod[:, D_MODEL:2 * D_MODEL]
    h = (x * (1.0 + sc1) + sh1).astype(BF16)

    def seg(s):
        return jnp.dot(h, w_ref[:, s * D_MODEL:(s + 1) * D_MODEL], preferred_element_type=F32)

    cos = cos_ref[...]
    sin = sin_ref[...]
    lane = lax.broadcasted_iota(jnp.int32, (TM, LANES), 1)
    first = (lane % (2 * ROPE_HALF)) < ROPE_HALF

    def rope(zh):
        up = pltpu.roll(zh, LANES - ROPE_HALF, 1)
        dn = pltpu.roll(zh, ROPE_HALF, 1)
        return zh * cos + jnp.where(first, up, dn) * sin

    zq = seg(0)
    for hd in range(N_HEADS):
        q_ref[0, hd] = (rope(zq[:, hd * HEAD_W:(hd + 1) * HEAD_W]) * Q_SCALE).astype(BF16)
    zk = seg(1)
    for hd in range(N_HEADS):
        k_ref[0, hd] = rope(zk[:, hd * HEAD_W:(hd + 1) * HEAD_W]).astype(BF16)
    zv = seg(2)
    for hd in range(N_HEADS):
        v_ref[0, hd] = zv[:, hd * HEAD_W:(hd + 1) * HEAD_W].astype(BF16)

    @pl.when(is_ctx)
    def _():
        for hd in range(N_HEADS):
            nk_ref[0, hd] = zk[:, hd * HEAD_W:(hd + 1) * HEAD_W]
            nv_ref[0, hd] = zv[:, hd * HEAD_W:(hd + 1) * HEAD_W]

    ug_ref[...] = _gelu(seg(3)).astype(BF16)
    vg_ref[...] = _layer_norm(_gelu(seg(4)), lg_ref[...], lb_ref[...]).astype(BF16)
    sa_ref[...] = jax.nn.sigmoid(seg(5) + bg_ref[0:1, :]).astype(BF16)
    sb_ref[...] = jax.nn.sigmoid(seg(6) + bg_ref[1:2, :]).astype(BF16)


def _input_projection(xp, xs, mod3, w_in, cos_t, sin_t, b_gate, ln_g, ln_b):
    tile = lambda i: (i, 0)
    head_blk = (1, N_HEADS, TM, HEAD_W)
    tok_spec = pl.BlockSpec((TM, D_MODEL), tile)
    return pl.pallas_call(
        _proj_kernel,
        grid=(NT,),
        in_specs=[
            pl.BlockSpec((TM, D_MODEL), lambda i: (_ctx_idx(i), 0)),
            pl.BlockSpec((TM, D_MODEL), lambda i: (_lat_idx(i), 0)),
            pl.BlockSpec((1, 1, 6 * D_MODEL), lambda i: (_mod_row(i), 0, 0)),
            pl.BlockSpec((D_MODEL, N_SEG * D_MODEL), lambda i: (0, 0)),
            pl.BlockSpec((TM, LANES), lambda i: (jnp.where(i < NT_CTX, 0, 1 + (i - NT_CTX) % LAT_TILES), 0)),
            pl.BlockSpec((TM, LANES), lambda i: (jnp.where(i < NT_CTX, 0, 1 + (i - NT_CTX) % LAT_TILES), 0)),
            pl.BlockSpec((2, D_MODEL), lambda i: (0, 0)),
            pl.BlockSpec((1, D_MODEL), lambda i: (0, 0)),
            pl.BlockSpec((1, D_MODEL), lambda i: (0, 0)),
        ],
        out_specs=[
            pl.BlockSpec(head_blk, lambda i: (i, 0, 0, 0)),
            pl.BlockSpec(head_blk, lambda i: (i, 0, 0, 0)),
            pl.BlockSpec(head_blk, lambda i: (i, 0, 0, 0)),
            pl.BlockSpec(head_blk, lambda i: (_ctx_idx(i), 0, 0, 0)),
            pl.BlockSpec(head_blk, lambda i: (_ctx_idx(i), 0, 0, 0)),
            tok_spec, tok_spec, tok_spec, tok_spec,
        ],
        out_shape=[
            jax.ShapeDtypeStruct((NT, N_HEADS, TM, HEAD_W), BF16),
            jax.ShapeDtypeStruct((NT, N_HEADS, TM, HEAD_W), BF16),
            jax.ShapeDtypeStruct((NT, N_HEADS, TM, HEAD_W), BF16),
            jax.ShapeDtypeStruct((NT_CTX, N_HEADS, TM, HEAD_W), F32),
            jax.ShapeDtypeStruct((NT_CTX, N_HEADS, TM, HEAD_W), F32),
            jax.ShapeDtypeStruct((T_ALL, D_MODEL), BF16),
            jax.ShapeDtypeStruct((T_ALL, D_MODEL), BF16),
            jax.ShapeDtypeStruct((T_ALL, D_MODEL), BF16),
            jax.ShapeDtypeStruct((T_ALL, D_MODEL), BF16),
        ],
        compiler_params=_params(("arbitrary",)),
        name="input_projection",
    )(xp, xs, mod3, w_in, cos_t, sin_t, b_gate, ln_g, ln_b)


def _attn_kernel(n_main, has_ctx, lam_ref, q_ref, k_ref, v_ref, *rest):
    if has_ctx:
        ck_ref, cv_ref, sub_ref, o_ref, kall, vt_all = rest
    else:
        sub_ref, o_ref, kall, vt_all = rest
    n_chunks = kall.shape[0]

    @pl.when(pl.program_id(2) == 0)
    def _():
        for c in range(n_main):
            kall[c] = k_ref[c, 0]
            vt_all[c] = v_ref[c, 0].astype(F32).T.astype(BF16)
        if has_ctx:
            kall[n_main] = ck_ref[0, 0].astype(BF16)
            vt_all[n_main] = cv_ref[0, 0].T.astype(BF16)

    q = q_ref[0, 0]
    lane = lax.broadcasted_iota(jnp.int32, (TM, HEAD_W), 1)
    zero = jnp.zeros_like(q)
    qq = jnp.concatenate([jnp.where(lane < HEAD_DIM, q, zero),
                          jnp.where(lane >= HEAD_DIM, q, zero)], axis=0)

    def scores(c):
        return lax.dot_general(kall[c], qq, (((1,), (1,)), ((), ())), preferred_element_type=F32)

    s_next = scores(0)
    m = l = acc = None
    for c in range(n_chunks):
        s = s_next
        if c + 1 < n_chunks:
            s_next = scores(c + 1)
        m_c = jnp.max(s, axis=0, keepdims=True)
        m_new = m_c if c == 0 else jnp.maximum(m, m_c)
        e = jnp.exp2(s - m_new)
        l_c = jnp.sum(e, axis=0, keepdims=True)
        pv = jnp.dot(vt_all[c], e.astype(BF16), preferred_element_type=F32)
        if c == 0:
            l, acc = l_c, pv
        else:
            alpha = jnp.exp2(m - m_new)
            l = alpha * l + l_c
            acc = alpha * acc + pv
        m = m_new
    c1 = 1.0 / l[:, :TM]
    c2 = lam_ref[0] / l[:, TM:]
    a = (acc[:, :TM] * c1 - acc[:, TM:] * c2).T
    ms = jnp.mean(a * a, axis=-1, keepdims=True)
    o_ref[...] = (a * lax.rsqrt(ms + LN_EPS) * sub_ref[...]).astype(BF16)


def _attention(lam, q_all, k_all, v_all, ctx_k, ctx_v, subw, latent):
    if latent:
        n_b, n_q, n_main, base = N_LAT_B, LAT_TILES, LAT_TILES, NT_CTX
        n_keys = LAT_LEN + PAST_LEN
    else:
        n_b, n_q, n_main, base = N_CTX_B, 1, 1, 0
        n_keys = CTX_LEN
    q_spec = pl.BlockSpec((1, 1, TM, HEAD_W), lambda b, h, t: (base + b * n_q + t, h, 0, 0))
    kv_spec = pl.BlockSpec((n_main, 1, TM, HEAD_W), lambda b, h, t: (base // n_main + b, h, 0, 0))
    in_specs = [pl.BlockSpec(memory_space=pltpu.SMEM), q_spec, kv_spec, kv_spec]
    args = [lam, q_all, k_all, v_all]
    if latent:
        c_spec = pl.BlockSpec((1, 1, PAST_LEN, HEAD_W), lambda b, h, t: (b, h, 0, 0))
        in_specs += [c_spec, c_spec]
        args += [ctx_k, ctx_v]
    in_specs.append(pl.BlockSpec((1, HEAD_W), lambda b, h, t: (0, 0)))
    args.append(subw)
    return pl.pallas_call(
        functools.partial(_attn_kernel, n_main, latent),
        grid=(n_b, N_HEADS, n_q),
        in_specs=in_specs,
        out_specs=pl.BlockSpec((TM, HEAD_W), lambda b, h, t: (b * n_q + t, h)),
        out_shape=jax.ShapeDtypeStruct((n_b * n_q * TM, D_MODEL), BF16),
        scratch_shapes=[pltpu.VMEM((n_keys // TM, TM, HEAD_W), BF16),
                        pltpu.VMEM((n_keys // TM, HEAD_W, TM), BF16)],
        compiler_params=_params(("arbitrary", "arbitrary", "arbitrary")),
        name="diff_attention_lat" if latent else "diff_attention_ctx",
    )(*args)


def _mix_kernel(ac_ref, al_ref, ug_ref, vg_ref, sa_ref, sb_ref, xp_ref, xs_ref, mod_ref,
                ws_ref, bs_ref, wpa_ref, wpb_ref, wo_ref, g_ref, b_ref, wrh_ref, wrl_ref, br_ref,
                x1_ref, h2_ref, ti_ref, tg_ref):
    i = pl.program_id(0)
    is_ctx = i < NT_CTX
    a = jnp.where(is_ctx, ac_ref[...], al_ref[...])
    x = jnp.where(is_ctx, xp_ref[...], xs_ref[...])
    mod = mod_ref[0]
    g1 = mod[:, 2 * D_MODEL:3 * D_MODEL]
    sh2 = mod[:, 3 * D_MODEL:4 * D_MODEL]
    sc2 = mod[:, 4 * D_MODEL:5 * D_MODEL]

    gw = D_MODEL // GMLP_GROUPS
    chunks = []
    for c in range(TM // CHUNK):
        groups = []
        for g in range(GMLP_GROUPS):
            vc = vg_ref[c * CHUNK:(c + 1) * CHUNK, g * gw:(g + 1) * gw]
            groups.append(jnp.dot(ws_ref[g], vc, preferred_element_type=F32))
        chunks.append(jnp.concatenate(groups, axis=1) + bs_ref[...])
    sp = jnp.concatenate(chunks, axis=0)
    gm = (ug_ref[...].astype(F32) * sp).astype(BF16)

    pa = jnp.dot(a, wpa_ref[...], preferred_element_type=F32)
    pb = jnp.dot(gm, wpb_ref[...], preferred_element_type=F32)
    merged = (sa_ref[...].astype(F32) * pa + sb_ref[...].astype(F32) * pb).astype(BF16)
    mix = jnp.dot(merged, wo_ref[...], preferred_element_type=F32)
    x1 = _layer_norm(ALPHA * x + g1 * mix, g_ref[...], b_ref[...])
    x1_ref[...] = x1
    h2 = x1 * (1.0 + sc2) + sh2
    hi = h2.astype(BF16)
    h2_ref[...] = hi
    lo = (h2 - hi.astype(F32)).astype(BF16)
    logits = (jnp.dot(hi, wrh_ref[...], preferred_element_type=F32)
              + jnp.dot(lo, wrh_ref[...], preferred_element_type=F32)
              + jnp.dot(hi, wrl_ref[...], preferred_element_type=F32)) + br_ref[...]
    lane = lax.broadcasted_iota(jnp.int32, (TM, LANES), 1).astype(F32)
    vals, idxs = [], []
    for _ in range(TOP_K):
        mx = jnp.max(logits, axis=-1, keepdims=True)
        ix = jnp.min(jnp.where(logits == mx, lane, float(LANES)), axis=-1, keepdims=True)
        vals.append(mx)
        idxs.append(ix)
        logits = jnp.where(lane == ix, NEG_BIG * 2.0, logits)
    es = [jnp.exp(v - vals[0]) for v in vals]
    inv = 1.0 / (es[0] + es[1] + es[2] + es[3])
    ti = jnp.zeros((TM, LANES), F32)
    tg = jnp.zeros((TM, LANES), F32)
    for k in range(TOP_K):
        ti = jnp.where(lane == float(k), idxs[k], ti)
        tg = jnp.where(lane == float(k), es[k] * inv, tg)
    ti_ref[...] = ti.astype(jnp.int32)
    tg_ref[...] = tg


def _mix(a_ctx, a_lat, ug, vg, sa, sb, xp, xs, mod3, ws, bs, wpa, wpb, wo, g, b, wrh, wrl, br):
    tile = lambda i: (i, 0)
    full2 = lambda i: (0, 0)
    tok = pl.BlockSpec((TM, D_MODEL), tile)
    ctx = pl.BlockSpec((TM, D_MODEL), lambda i: (_ctx_idx(i), 0))
    lat = pl.BlockSpec((TM, D_MODEL), lambda i: (_lat_idx(i), 0))
    wsq = pl.BlockSpec((D_MODEL, D_MODEL), full2)
    vec = pl.BlockSpec((1, D_MODEL), full2)
    return pl.pallas_call(
        _mix_kernel,
        grid=(NT,),
        in_specs=[ctx, lat, tok, tok, tok, tok, ctx, lat,
                  pl.BlockSpec((1, 1, 6 * D_MODEL), lambda i: (_mod_row(i), 0, 0)),
                  pl.BlockSpec((GMLP_GROUPS, CHUNK, CHUNK), lambda i: (0, 0, 0)),
                  pl.BlockSpec((CHUNK, D_MODEL), full2),
                  wsq, wsq, wsq, vec, vec,
                  pl.BlockSpec((D_MODEL, LANES), full2),
                  pl.BlockSpec((D_MODEL, LANES), full2),
                  pl.BlockSpec((1, LANES), full2)],
        out_specs=[tok, tok, pl.BlockSpec((TM, LANES), tile), pl.BlockSpec((TM, LANES), tile)],
        out_shape=[jax.ShapeDtypeStruct((T_ALL, D_MODEL), F32),
                   jax.ShapeDtypeStruct((T_ALL, D_MODEL), BF16),
                   jax.ShapeDtypeStruct((T_ALL, LANES), jnp.int32),
                   jax.ShapeDtypeStruct((T_ALL, LANES), F32)],
        compiler_params=_params(("arbitrary",)),
        name="mix_ln1_router",
    )(a_ctx, a_lat, ug, vg, sa, sb, xp, xs, mod3, ws, bs, wpa, wpb, wo, g, b, wrh, wrl, br)


def _expert_kernel(be_ref, nu_ref, x_ref, wu_ref, bu_ref, wd_ref, bd_ref, y_ref):
    @pl.when(pl.program_id(0) < nu_ref[0])
    def _():
        hu = jnp.dot(x_ref[...], wu_ref[0], preferred_element_type=F32) + bu_ref[0]
        glu = jnp.minimum(hu[:, :D_EXPERT], SWIGLU_LIMIT)
        lin = jnp.clip(hu[:, D_EXPERT:], -SWIGLU_LIMIT, SWIGLU_LIMIT)
        act = glu * jax.nn.sigmoid(SWIGLU_ALPHA * glu) * (lin + 1.0)
        y_ref[...] = jnp.dot(act.astype(BF16), wd_ref[0], preferred_element_type=F32) + bd_ref[0]


def _experts(blk_exp, n_used, xb, w_up, b_up, w_down, b_down):
    n_blocks = xb.shape[0] // MOE_BLOCK
    grid_spec = pltpu.PrefetchScalarGridSpec(
        num_scalar_prefetch=2,
        grid=(n_blocks,),
        in_specs=[pl.BlockSpec((MOE_BLOCK, D_MODEL), lambda i, be, nu: (i, 0)),
                  pl.BlockSpec((1, D_MODEL, 2 * D_EXPERT), lambda i, be, nu: (be[i], 0, 0)),
                  pl.BlockSpec((1, 1, 2 * D_EXPERT), lambda i, be, nu: (be[i], 0, 0)),
                  pl.BlockSpec((1, D_EXPERT, D_MODEL), lambda i, be, nu: (be[i], 0, 0)),
                  pl.BlockSpec((1, 1, D_MODEL), lambda i, be, nu: (be[i], 0, 0))],
        out_specs=pl.BlockSpec((MOE_BLOCK, D_MODEL), lambda i, be, nu: (i, 0)),
    )
    return pl.pallas_call(
        _expert_kernel,
        grid_spec=grid_spec,
        out_shape=jax.ShapeDtypeStruct((n_blocks * MOE_BLOCK, D_MODEL), F32),
        compiler_params=_params(("arbitrary",)),
        name="expert_ffn",
    )(blk_exp, n_used, xb, w_up, b_up, w_down, b_down)


def _final_kernel(x1_ref, f_ref, mod_ref, g_ref, b_ref, yp_ref, ys_ref):
    i = pl.program_id(0)
    g2 = mod_ref[0][:, 5 * D_MODEL:6 * D_MODEL]
    y = _layer_norm(ALPHA * x1_ref[...] + g2 * f_ref[...], g_ref[...], b_ref[...])

    @pl.when(i < NT_CTX)
    def _():
        yp_ref[...] = y

    @pl.when(i >= NT_CTX)
    def _():
        ys_ref[...] = y


def _final(x1, ffn, mod3, g, b):
    tok = pl.BlockSpec((TM, D_MODEL), lambda i: (i, 0))
    vec = pl.BlockSpec((1, D_MODEL), lambda i: (0, 0))
    return pl.pallas_call(
        _final_kernel,
        grid=(NT,),
        in_specs=[tok, tok, pl.BlockSpec((1, 1, 6 * D_MODEL), lambda i: (_mod_row(i), 0, 0)), vec, vec],
        out_specs=[pl.BlockSpec((TM, D_MODEL), lambda i: (_ctx_idx(i), 0)),
                   pl.BlockSpec((TM, D_MODEL), lambda i: (_lat_idx(i), 0))],
        out_shape=[jax.ShapeDtypeStruct((NT_CTX * TM, D_MODEL), F32),
                   jax.ShapeDtypeStruct((NT_LAT * TM, D_MODEL), F32)],
        compiler_params=_params(("arbitrary",)),
        name="residual_ln2",
    )(x1, ffn, mod3, g, b)


def _rope_tables():
    t = jnp.arange(LAT_LEN)
    r = (t // GRID_W).astype(F32)
    col = (t % GRID_W).astype(F32)
    inv = jnp.power(ROPE_BASE, -jnp.arange(ROPE_HALF, dtype=F32) / ROPE_HALF)
    ang_r = r[:, None] * inv
    ang_c = col[:, None] * inv
    ang = jnp.concatenate([ang_r, ang_r, ang_c, ang_c], axis=-1)
    cos = jnp.tile(jnp.cos(ang), (1, 2))
    sin = jnp.tile(jnp.sin(ang), (1, 2))
    sign = jnp.where((jnp.arange(LANES) % (2 * ROPE_HALF)) < ROPE_HALF, -1.0, 1.0).astype(F32)
    cos = jnp.concatenate([jnp.ones((TM, LANES), F32), cos], axis=0)
    sin = jnp.concatenate([jnp.zeros((TM, LANES), F32), sin * sign], axis=0)
    return cos, sin


def _routing(top_idx):
    n_tok = top_idx.shape[0]
    n_assign = n_tok * TOP_K
    flat_e = top_idx.reshape(n_assign)
    order = jnp.argsort(flat_e)
    sorted_e = flat_e[order]
    tok = order // TOP_K
    counts = jnp.bincount(flat_e, length=N_EXPERTS)
    padded = (counts + MOE_BLOCK - 1) // MOE_BLOCK * MOE_BLOCK
    start = jnp.cumsum(counts) - counts
    pend = jnp.cumsum(padded)
    pstart = pend - padded
    dest = pstart[sorted_e] + (jnp.arange(n_assign) - start[sorted_e])
    n_blocks = -(-n_assign // MOE_BLOCK) + N_EXPERTS
    row_tok = jnp.zeros((n_blocks * MOE_BLOCK,), jnp.int32).at[dest].set(tok.astype(jnp.int32))
    blk_exp = jnp.minimum(jnp.searchsorted(pend, jnp.arange(n_blocks) * MOE_BLOCK, side='right'),
                          N_EXPERTS - 1).astype(jnp.int32)
    dest_tk = jnp.zeros((n_assign,), jnp.int32).at[order].set(dest.astype(jnp.int32))
    n_used = (pend[-1] // MOE_BLOCK).astype(jnp.int32).reshape(1)
    return row_tok, blk_exp, dest_tk.reshape(n_tok, TOP_K), n_used


def kernel(x_prompt, x_sample, c, cache_k, cache_v, c_ctx, w_ada, b_ada, w_in, lambda_q1, lambda_k1, lambda_q2, lambda_k2, subln_w, gmlp_ln_g, gmlp_ln_b, w_spatial, b_spatial, b_gate, w_pa, w_pb, w_o, ln1_g, ln1_b, w_router, b_router, w_up, b_up, w_down, b_down, ln2_g, ln2_b):
    l = 0
    xp = x_prompt.reshape(N_CTX_B * CTX_LEN, D_MODEL)
    xs = x_sample.reshape(N_LAT_B * LAT_LEN, D_MODEL)

    cond = jnp.concatenate([c_ctx[None, :], c, jnp.zeros((16 - 1 - N_LAT_B, D_MODEL), F32)], axis=0)
    mod3 = _modulation(cond, w_ada[l], b_ada[l][None, :]).reshape(16, 1, 6 * D_MODEL)

    cos_t, sin_t = _rope_tables()
    (q_all, k_all, v_all, new_k, new_v, ug, vg, sa, sb) = _input_projection(
        xp, xs, mod3, w_in[l].astype(BF16), cos_t, sin_t, b_gate[l],
        gmlp_ln_g[l][None, :], gmlp_ln_b[l][None, :])

    lam = (jnp.exp(jnp.sum(lambda_q1[l] * lambda_k1[l])) - jnp.exp(jnp.sum(lambda_q2[l] * lambda_k2[l]))
           + LAM_INIT).reshape(1).astype(F32)
    subw = (subln_w[l] * (1.0 - LAM_INIT))[None, :]
    ctx_k = cache_k[:, l].reshape(N_LAT_B, N_HEADS, PAST_LEN, HEAD_W)
    ctx_v = cache_v[:, l]
    a_ctx = _attention(lam, q_all, k_all, v_all, None, None, subw, latent=False)
    a_lat = _attention(lam, q_all, k_all, v_all, ctx_k, ctx_v, subw, latent=True)

    bs_full = jnp.repeat(b_spatial[l].T, D_MODEL // GMLP_GROUPS, axis=1)
    wr = jnp.pad(w_router[l], ((0, 0), (0, LANES - N_EXPERTS)))
    wr_hi = wr.astype(BF16)
    wr_lo = (wr - wr_hi.astype(F32)).astype(BF16)
    br = jnp.concatenate([b_router[l], jnp.full((LANES - N_EXPERTS,), NEG_BIG, F32)])[None, :]
    x1, h2, top_i, top_g = _mix(
        a_ctx, a_lat, ug, vg, sa, sb, xp, xs, mod3, w_spatial[l].astype(BF16), bs_full,
        w_pa[l].astype(BF16), w_pb[l].astype(BF16), w_o[l].astype(BF16),
        ln1_g[l][None, :], ln1_b[l][None, :], wr_hi, wr_lo, br)

    top_idx = top_i[:, :TOP_K]
    gate = top_g[:, :TOP_K]
    row_tok, blk_exp, dest_tk, n_used = _routing(top_idx)
    xb = h2[row_tok]
    yb = _experts(blk_exp, n_used, xb, w_up[l].astype(BF16), b_up[l][:, None, :],
                  w_down[l].astype(BF16), b_down[l][:, None, :])
    ffn = jnp.sum(yb[dest_tk] * gate[:, :, None], axis=1)

    y_prompt, y_sample = _final(x1, ffn, mod3, ln2_g[l][None, :], ln2_b[l][None, :])
    return (y_prompt.reshape(N_CTX_B, CTX_LEN, D_MODEL),
            y_sample.reshape(N_LAT_B, LAT_LEN, D_MODEL),
            new_k.reshape(N_CTX_B, DEPTH, N_HEADS, CTX_LEN, 2, HEAD_DIM),
            new_v.reshape(N_CTX_B, DEPTH, N_HEADS, CTX_LEN, HEAD_W))
```

```python
import functools
import math

import jax
import jax.numpy as jnp
from jax import lax
from jax.experimental import pallas as pl
from jax.experimental.pallas import tpu as pltpu

F32 = jnp.float32
BF16 = jnp.bfloat16

D_MODEL = 1024
N_CTX_B = 16
CTX_LEN = 256
N_LAT_B = 8
LAT_LEN = 4096
PAST_LEN = 256
GRID_W = 64
N_HEADS = 8
HEAD_DIM = 64
HEAD_W = 2 * HEAD_DIM
ROPE_HALF = HEAD_DIM // 4
ROPE_BASE = 10000.0
GMLP_GROUPS = 4
CHUNK = 128
N_EXPERTS = 32
TOP_K = 4
D_EXPERT = 1024
SWIGLU_LIMIT = 7.0
SWIGLU_ALPHA = 1.702
MOE_BLOCK = 256
LN_EPS = 1e-5
DEPTH = 1
N_SEG = 7

TM = 256
NT_CTX = N_CTX_B * CTX_LEN // TM
NT_LAT = N_LAT_B * LAT_LEN // TM
NT = NT_CTX + NT_LAT
T_ALL = NT * TM
LAT_TILES = LAT_LEN // TM
LANES = 128
NEG_BIG = -1e30
VMEM_LIMIT = 56 * 1024 * 1024

ALPHA = (2.0 * DEPTH) ** 0.25
LAM_INIT = 0.8 - 0.6 * math.exp(-0.3 * 0)
Q_SCALE = HEAD_DIM ** -0.5 * math.log2(math.e)


def _ctx_idx(i):
    return jnp.minimum(i, NT_CTX - 1)


def _lat_idx(i):
    return jnp.maximum(i - NT_CTX, 0)


def _mod_row(i):
    return jnp.where(i < NT_CTX, 0, 1 + (i - NT_CTX) // LAT_TILES)


def _layer_norm(x, g, b):
    mu = jnp.mean(x, axis=-1, keepdims=True)
    xc = x - mu
    var = jnp.mean(xc * xc, axis=-1, keepdims=True)
    return xc * lax.rsqrt(var + LN_EPS) * g + b


def _gelu(x):
    return 0.5 * x * (1.0 + lax.erf(x * (1.0 / math.sqrt(2.0))))


def _split(x):
    hi = x.astype(BF16)
    return hi, (x - hi.astype(F32)).astype(BF16)


def _params(sem):
    return pltpu.CompilerParams(dimension_semantics=sem, vmem_limit_bytes=VMEM_LIMIT)


def _mod_kernel(c_ref, w_ref, b_ref, o_ref):
    c = c_ref[...]
    s = c * jax.nn.sigmoid(c)
    s_hi, s_lo = _split(s)
    w_hi, w_lo = _split(w_ref[...])
    o_ref[...] = (jnp.dot(s_hi, w_hi, preferred_element_type=F32)
                  + jnp.dot(s_lo, w_hi, preferred_element_type=F32)
                  + jnp.dot(s_hi, w_lo, preferred_element_type=F32)) + b_ref[...]


def _modulation(cond, w_ada, b_ada):
    n = cond.shape[0]
    return pl.pallas_call(
        _mod_kernel,
        grid=(6,),
        in_specs=[pl.BlockSpec((n, D_MODEL), lambda j: (0, 0)),
                  pl.BlockSpec((D_MODEL, D_MODEL), lambda j: (0, j)),
                  pl.BlockSpec((1, D_MODEL), lambda j: (0, j))],
        out_specs=pl.BlockSpec((n, D_MODEL), lambda j: (0, j)),
        out_shape=jax.ShapeDtypeStruct((n, 6 * D_MODEL), F32),
        compiler_params=_params(("arbitrary",)),
        name="adaln_mod",
    )(cond, w_ada, b_ada)


def _proj_kernel(xp_ref, xs_ref, mod_ref, w_ref, cos_ref, sin_ref, bg_ref, lg_ref, lb_ref,
                 q_ref, k_ref, v_ref, nk_ref, nv_ref, ug_ref, vg_ref, sa_ref, sb_ref):
    i = pl.program_id(0)
    is_ctx = i < NT_CTX
    x = jnp.where(is_ctx, xp_ref[...], xs_ref[...])
    mod = mod_ref[0]
    sh1 = mod[:, 0:D_MODEL]
    sc1 = mod[:, D_MODEL:2 * D_MODEL]
    h = (x * (1.0 + sc1) + sh1).astype(BF16)

    def seg(s):
        return jnp.dot(h, w_ref[:, s * D_MODEL:(s + 1) * D_MODEL], preferred_element_type=F32)

    cos = cos_ref[...]
    sin = sin_ref[...]
    lane = lax.broadcasted_iota(jnp.int32, (TM, LANES), 1)
    first = (lane % (2 * ROPE_HALF)) < ROPE_HALF

    def rope(zh):
        up = pltpu.roll(zh, LANES - ROPE_HALF, 1)
        dn = pltpu.roll(zh, ROPE_HALF, 1)
        return zh * cos + jnp.where(first, up, dn) * sin

    zq = seg(0)
    for hd in range(N_HEADS):
        q_ref[0, hd] = (rope(zq[:, hd * HEAD_W:(hd + 1) * HEAD_W]) * Q_SCALE).astype(BF16)
    zk = seg(1)
    for hd in range(N_HEADS):
        k_ref[0, hd] = rope(zk[:, hd * HEAD_W:(hd + 1) * HEAD_W]).astype(BF16)
    zv = seg(2)
    for hd in range(N_HEADS):
        v_ref[0, hd] = zv[:, hd * HEAD_W:(hd + 1) * HEAD_W].astype(BF16)

    @pl.when(is_ctx)
    def _():
        for hd in range(N_HEADS):
            nk_ref[0, hd] = zk[:, hd * HEAD_W:(hd + 1) * HEAD_W]
            nv_ref[0, hd] = zv[:, hd * HEAD_W:(hd + 1) * HEAD_W]

    ug_ref[...] = _gelu(seg(3)).astype(BF16)
    vg_ref[...] = _layer_norm(_gelu(seg(4)), lg_ref[...], lb_ref[...]).astype(BF16)
    sa_ref[...] = jax.nn.sigmoid(seg(5) + bg_ref[0:1, :]).astype(BF16)
    sb_ref[...] = jax.nn.sigmoid(seg(6) + bg_ref[1:2, :]).astype(BF16)


def _input_projection(xp, xs, mod3, w_in, cos_t, sin_t, b_gate, ln_g, ln_b):
    tile = lambda i: (i, 0)
    head_blk = (1, N_HEADS, TM, HEAD_W)
    tok_spec = pl.BlockSpec((TM, D_MODEL), tile)
    return pl.pallas_call(
        _proj_kernel,
        grid=(NT,),
        in_specs=[
            pl.BlockSpec((TM, D_MODEL), lambda i: (_ctx_idx(i), 0)),
            pl.BlockSpec((TM, D_MODEL), lambda i: (_lat_idx(i), 0)),
            pl.BlockSpec((1, 1, 6 * D_MODEL), lambda i: (_mod_row(i), 0, 0)),
            pl.BlockSpec((D_MODEL, N_SEG * D_MODEL), lambda i: (0, 0)),
            pl.BlockSpec((TM, LANES), lambda i: (jnp.where(i < NT_CTX, 0, 1 + (i - NT_CTX) % LAT_TILES), 0)),
            pl.BlockSpec((TM, LANES), lambda i: (jnp.where(i < NT_CTX, 0, 1 + (i - NT_CTX) % LAT_TILES), 0)),
            pl.BlockSpec((2, D_MODEL), lambda i: (0, 0)),
            pl.BlockSpec((1, D_MODEL), lambda i: (0, 0)),
            pl.BlockSpec((1, D_MODEL), lambda i: (0, 0)),
        ],
        out_specs=[
            pl.BlockSpec(head_blk, lambda i: (i, 0, 0, 0)),
            pl.BlockSpec(head_blk, lambda i: (i, 0, 0, 0)),
            pl.BlockSpec(head_blk, lambda i: (i, 0, 0, 0)),
            pl.BlockSpec(head_blk, lambda i: (_ctx_idx(i), 0, 0, 0)),
            pl.BlockSpec(head_blk, lambda i: (_ctx_idx(i), 0, 0, 0)),
            tok_spec, tok_spec, tok_spec, tok_spec,
        ],
        out_shape=[
            jax.ShapeDtypeStruct((NT, N_HEADS, TM, HEAD_W), BF16),
            jax.ShapeDtypeStruct((NT, N_HEADS, TM, HEAD_W), BF16),
            jax.ShapeDtypeStruct((NT, N_HEADS, TM, HEAD_W), BF16),
            jax.ShapeDtypeStruct((NT_CTX, N_HEADS, TM, HEAD_W), F32),
            jax.ShapeDtypeStruct((NT_CTX, N_HEADS, TM, HEAD_W), F32),
            jax.ShapeDtypeStruct((T_ALL, D_MODEL), BF16),
            jax.ShapeDtypeStruct((T_ALL, D_MODEL), BF16),
            jax.ShapeDtypeStruct((T_ALL, D_MODEL), BF16),
            jax.ShapeDtypeStruct((T_ALL, D_MODEL), BF16),
        ],
        compiler_params=_params(("arbitrary",)),
        name="input_projection",
    )(xp, xs, mod3, w_in, cos_t, sin_t, b_gate, ln_g, ln_b)


def _attn_kernel(n_main, has_ctx, lam_ref, q_ref, k_ref, v_ref, *rest):
    if has_ctx:
        ck_ref, cv_ref, sub_ref, o_ref, kall, vt_all = rest
    else:
        sub_ref, o_ref, kall, vt_all = rest
    n_chunks = kall.shape[0]

    @pl.when(pl.program_id(2) == 0)
    def _():
        for c in range(n_main):
            kall[c] = k_ref[c, 0]
            vt_all[c] = v_ref[c, 0].astype(F32).T.astype(BF16)
        if has_ctx:
            kall[n_main] = ck_ref[0, 0].astype(BF16)
            vt_all[n_main] = cv_ref[0, 0].T.astype(BF16)

    q = q_ref[0, 0]
    lane = lax.broadcasted_iota(jnp.int32, (TM, HEAD_W), 1)
    zero = jnp.zeros_like(q)
    qq = jnp.concatenate([jnp.where(lane < HEAD_DIM, q, zero),
                          jnp.where(lane >= HEAD_DIM, q, zero)], axis=0)

    def scores(c):
        return lax.dot_general(kall[c], qq, (((1,), (1,)), ((), ())), preferred_element_type=F32)

    s_next = scores(0)
    m = l = acc = None
    for c in range(n_chunks):
        s = s_next
        if c + 1 < n_chunks:
            s_next = scores(c + 1)
        m_c = jnp.max(s, axis=0, keepdims=True)
        m_new = m_c if c == 0 else jnp.maximum(m, m_c)
        e = jnp.exp2(s - m_new)
        l_c = jnp.sum(e, axis=0, keepdims=True)
        pv = jnp.dot(vt_all[c], e.astype(BF16), preferred_element_type=F32)
        if c == 0:
            l, acc = l_c, pv
        else:
            alpha = jnp.exp2(m - m_new)
            l = alpha * l + l_c
            acc = alpha * acc + pv
        m = m_new
    c1 = 1.0 / l[:, :TM]
    c2 = lam_ref[0] / l[:, TM:]
    a = (acc[:, :TM] * c1 - acc[:, TM:] * c2).T
    ms = jnp.mean(a * a, axis=-1, keepdims=True)
    o_ref[...] = (a * lax.rsqrt(ms + LN_EPS) * sub_ref[...]).astype(BF16)


def _attention(lam, q_all, k_all, v_all, ctx_k, ctx_v, subw, latent):
    if latent:
        n_b, n_q, n_main, base = N_LAT_B, LAT_TILES, LAT_TILES, NT_CTX
        n_keys = LAT_LEN + PAST_LEN
    else:
        n_b, n_q, n_main, base = N_CTX_B, 1, 1, 0
        n_keys = CTX_LEN
    q_spec = pl.BlockSpec((1, 1, TM, HEAD_W), lambda b, h, t: (base + b * n_q + t, h, 0, 0))
    kv_spec = pl.BlockSpec((n_main, 1, TM, HEAD_W), lambda b, h, t: (base // n_main + b, h, 0, 0))
    in_specs = [pl.BlockSpec(memory_space=pltpu.SMEM), q_spec, kv_spec, kv_spec]
    args = [lam, q_all, k_all, v_all]
    if latent:
        c_spec = pl.BlockSpec((1, 1, PAST_LEN, HEAD_W), lambda b, h, t: (b, h, 0, 0))
        in_specs += [c_spec, c_spec]
        args += [ctx_k, ctx_v]
    in_specs.append(pl.BlockSpec((1, HEAD_W), lambda b, h, t: (0, 0)))
    args.append(subw)
    return pl.pallas_call(
        functools.partial(_attn_kernel, n_main, latent),
        grid=(n_b, N_HEADS, n_q),
        in_specs=in_specs,
        out_specs=pl.BlockSpec((TM, HEAD_W), lambda b, h, t: (b * n_q + t, h)),
        out_shape=jax.ShapeDtypeStruct((n_b * n_q * TM, D_MODEL), BF16),
        scratch_shapes=[pltpu.VMEM((n_keys // TM, TM, HEAD_W), BF16),
                        pltpu.VMEM((n_keys // TM, HEAD_W, TM), BF16)],
        compiler_params=_params(("arbitrary", "arbitrary", "arbitrary")),
        name="diff_attention_lat" if latent else "diff_attention_ctx",
    )(*args)


def _mix_kernel(ac_ref, al_ref, ug_ref, vg_ref, sa_ref, sb_ref, xp_ref, xs_ref, mod_ref,
                ws_ref, bs_ref, wpa_ref, wpb_ref, wo_ref, g_ref, b_ref, wrh_ref, wrl_ref, br_ref,
                x1_ref, h2_ref, ti_ref, tg_ref, cnt_ref, run_ref):
    i = pl.program_id(0)
    is_ctx = i < NT_CTX
    a = jnp.where(is_ctx, ac_ref[...], al_ref[...])
    x = jnp.where(is_ctx, xp_ref[...], xs_ref[...])
    mod = mod_ref[0]
    g1 = mod[:, 2 * D_MODEL:3 * D_MODEL]
    sh2 = mod[:, 3 * D_MODEL:4 * D_MODEL]
    sc2 = mod[:, 4 * D_MODEL:5 * D_MODEL]

    gw = D_MODEL // GMLP_GROUPS
    chunks = []
    for c in range(TM // CHUNK):
        groups = []
        for g in range(GMLP_GROUPS):
            vc = vg_ref[c * CHUNK:(c + 1) * CHUNK, g * gw:(g + 1) * gw]
            groups.append(jnp.dot(ws_ref[g], vc, preferred_element_type=F32))
        chunks.append(jnp.concatenate(groups, axis=1) + bs_ref[...])
    sp = jnp.concatenate(chunks, axis=0)
    gm = (ug_ref[...].astype(F32) * sp).astype(BF16)

    pa = jnp.dot(a, wpa_ref[...], preferred_element_type=F32)
    pb = jnp.dot(gm, wpb_ref[...], preferred_element_type=F32)
    merged = (sa_ref[...].astype(F32) * pa + sb_ref[...].astype(F32) * pb).astype(BF16)
    mix = jnp.dot(merged, wo_ref[...], preferred_element_type=F32)
    x1 = _layer_norm(ALPHA * x + g1 * mix, g_ref[...], b_ref[...])
    x1_ref[...] = x1
    h2 = x1 * (1.0 + sc2) + sh2
    hi, lo = _split(h2)
    h2_ref[...] = hi
    logits = (jnp.dot(hi, wrh_ref[...], preferred_element_type=F32)
              + jnp.dot(lo, wrh_ref[...], preferred_element_type=F32)
              + jnp.dot(hi, wrl_ref[...], preferred_element_type=F32)) + br_ref[...]
    lane = lax.broadcasted_iota(jnp.int32, (TM, LANES), 1).astype(F32)
    vals, idxs = [], []
    for _ in range(TOP_K):
        mx = jnp.max(logits, axis=-1, keepdims=True)
        ix = jnp.min(jnp.where(logits == mx, lane, float(LANES)), axis=-1, keepdims=True)
        vals.append(mx)
        idxs.append(ix)
        logits = jnp.where(lane == ix, NEG_BIG * 2.0, logits)
    es = [jnp.exp(v - vals[0]) for v in vals]
    inv = 1.0 / (es[0] + es[1] + es[2] + es[3])

    @pl.when(i == 0)
    def _():
        run_ref[...] = jnp.zeros_like(run_ref)

    hot = [lane == ix for ix in idxs]
    memb = jnp.zeros((TM, LANES), F32)
    for hk in hot:
        memb = jnp.where(hk, 1.0, memb)
    row = lax.broadcasted_iota(jnp.int32, (TM, TM), 0)
    col = lax.broadcasted_iota(jnp.int32, (TM, TM), 1)
    before = jnp.where(row > col, 1.0, 0.0).astype(BF16)
    base = run_ref[...]
    rank_all = jnp.dot(before, memb.astype(BF16), preferred_element_type=F32) + base
    total = base + jnp.sum(memb, axis=0, keepdims=True)
    run_ref[...] = total
    cnt_ref[...] = total

    ti = jnp.zeros((TM, LANES), F32)
    tg = jnp.zeros((TM, LANES), F32)
    for k in range(TOP_K):
        rank_k = jnp.sum(jnp.where(hot[k], rank_all, 0.0), axis=-1, keepdims=True)
        ti = jnp.where(lane == float(k), idxs[k], ti)
        ti = jnp.where(lane == float(TOP_K + k), rank_k, ti)
        tg = jnp.where(lane == float(k), es[k] * inv, tg)
    ti_ref[...] = ti.astype(jnp.int32)
    tg_ref[...] = tg


def _mix(a_ctx, a_lat, ug, vg, sa, sb, xp, xs, mod3, ws, bs, wpa, wpb, wo, g, b, wrh, wrl, br):
    tile = lambda i: (i, 0)
    full2 = lambda i: (0, 0)
    tok = pl.BlockSpec((TM, D_MODEL), tile)
    ctx = pl.BlockSpec((TM, D_MODEL), lambda i: (_ctx_idx(i), 0))
    lat = pl.BlockSpec((TM, D_MODEL), lambda i: (_lat_idx(i), 0))
    wsq = pl.BlockSpec((D_MODEL, D_MODEL), full2)
    vec = pl.BlockSpec((1, D_MODEL), full2)
    return pl.pallas_call(
        _mix_kernel,
        grid=(NT,),
        in_specs=[ctx, lat, tok, tok, tok, tok, ctx, lat,
                  pl.BlockSpec((1, 1, 6 * D_MODEL), lambda i: (_mod_row(i), 0, 0)),
                  pl.BlockSpec((GMLP_GROUPS, CHUNK, CHUNK), lambda i: (0, 0, 0)),
                  pl.BlockSpec((CHUNK, D_MODEL), full2),
                  wsq, wsq, wsq, vec, vec,
                  pl.BlockSpec((D_MODEL, LANES), full2),
                  pl.BlockSpec((D_MODEL, LANES), full2),
                  pl.BlockSpec((1, LANES), full2)],
        out_specs=[tok, tok, pl.BlockSpec((TM, LANES), tile), pl.BlockSpec((TM, LANES), tile),
                   pl.BlockSpec((1, LANES), full2)],
        out_shape=[jax.ShapeDtypeStruct((T_ALL, D_MODEL), F32),
                   jax.ShapeDtypeStruct((T_ALL, D_MODEL), BF16),
                   jax.ShapeDtypeStruct((T_ALL, LANES), jnp.int32),
                   jax.ShapeDtypeStruct((T_ALL, LANES), F32),
                   jax.ShapeDtypeStruct((1, LANES), F32)],
        scratch_shapes=[pltpu.VMEM((1, LANES), F32)],
        compiler_params=_params(("arbitrary",)),
        name="mix_ln1_router",
    )(a_ctx, a_lat, ug, vg, sa, sb, xp, xs, mod3, ws, bs, wpa, wpb, wo, g, b, wrh, wrl, br)


def _expert_kernel(be_ref, nu_ref, x_ref, wu_ref, bu_ref, wd_ref, bd_ref, y_ref, wu_bf, wd_bf):
    i = pl.program_id(0)
    used = i < nu_ref[0]
    new_expert = jnp.logical_or(i == 0, be_ref[i] != be_ref[jnp.maximum(i - 1, 0)])

    @pl.when(jnp.logical_and(used, new_expert))
    def _():
        wu_bf[...] = wu_ref[0].astype(BF16)
        wd_bf[...] = wd_ref[0].astype(BF16)

    @pl.when(used)
    def _():
        hu = jnp.dot(x_ref[...], wu_bf[...], preferred_element_type=F32) + bu_ref[0]
        glu = jnp.minimum(hu[:, :D_EXPERT], SWIGLU_LIMIT)
        lin = jnp.clip(hu[:, D_EXPERT:], -SWIGLU_LIMIT, SWIGLU_LIMIT)
        act = glu * jax.nn.sigmoid(SWIGLU_ALPHA * glu) * (lin + 1.0)
        y = jnp.dot(act.astype(BF16), wd_bf[...], preferred_element_type=F32) + bd_ref[0]
        y_ref[...] = y.astype(BF16)

    @pl.when(jnp.logical_not(used))
    def _():
        y_ref[...] = jnp.zeros_like(y_ref)


def _experts(blk_exp, n_used, xb, w_up, b_up, w_down, b_down):
    n_blocks = xb.shape[0] // MOE_BLOCK
    grid_spec = pltpu.PrefetchScalarGridSpec(
        num_scalar_prefetch=2,
        grid=(n_blocks,),
        in_specs=[pl.BlockSpec((MOE_BLOCK, D_MODEL), lambda i, be, nu: (i, 0)),
                  pl.BlockSpec((1, D_MODEL, 2 * D_EXPERT), lambda i, be, nu: (be[i], 0, 0)),
                  pl.BlockSpec((1, 1, 2 * D_EXPERT), lambda i, be, nu: (be[i], 0, 0)),
                  pl.BlockSpec((1, D_EXPERT, D_MODEL), lambda i, be, nu: (be[i], 0, 0)),
                  pl.BlockSpec((1, 1, D_MODEL), lambda i, be, nu: (be[i], 0, 0))],
        out_specs=pl.BlockSpec((MOE_BLOCK, D_MODEL), lambda i, be, nu: (i, 0)),
        scratch_shapes=[pltpu.VMEM((D_MODEL, 2 * D_EXPERT), BF16),
                        pltpu.VMEM((D_EXPERT, D_MODEL), BF16)],
    )
    return pl.pallas_call(
        _expert_kernel,
        grid_spec=grid_spec,
        out_shape=jax.ShapeDtypeStruct((n_blocks * MOE_BLOCK, D_MODEL), BF16),
        compiler_params=_params(("arbitrary",)),
        name="expert_ffn",
    )(blk_exp, n_used, xb, w_up, b_up, w_down, b_down)


def _final_kernel(x1_ref, y0_ref, y1_ref, y2_ref, y3_ref, tg_ref, mod_ref, g_ref, b_ref,
                  yp_ref, ys_ref):
    i = pl.program_id(0)
    g2 = mod_ref[0][:, 5 * D_MODEL:6 * D_MODEL]
    gates = tg_ref[...]
    ffn = jnp.zeros((TM, D_MODEL), F32)
    for k, yk_ref in enumerate((y0_ref, y1_ref, y2_ref, y3_ref)):
        ffn = ffn + gates[:, k:k + 1] * yk_ref[...].astype(F32)
    y = _layer_norm(ALPHA * x1_ref[...] + g2 * ffn, g_ref[...], b_ref[...])

    @pl.when(i < NT_CTX)
    def _():
        yp_ref[...] = y

    @pl.when(i >= NT_CTX)
    def _():
        ys_ref[...] = y


def _final(x1, ys, top_g, mod3, g, b):
    tok = pl.BlockSpec((TM, D_MODEL), lambda i: (i, 0))
    vec = pl.BlockSpec((1, D_MODEL), lambda i: (0, 0))
    return pl.pallas_call(
        _final_kernel,
        grid=(NT,),
        in_specs=[tok, tok, tok, tok, tok, pl.BlockSpec((TM, LANES), lambda i: (i, 0)),
                  pl.BlockSpec((1, 1, 6 * D_MODEL), lambda i: (_mod_row(i), 0, 0)), vec, vec],
        out_specs=[pl.BlockSpec((TM, D_MODEL), lambda i: (_ctx_idx(i), 0)),
                   pl.BlockSpec((TM, D_MODEL), lambda i: (_lat_idx(i), 0))],
        out_shape=[jax.ShapeDtypeStruct((NT_CTX * TM, D_MODEL), F32),
                   jax.ShapeDtypeStruct((NT_LAT * TM, D_MODEL), F32)],
        compiler_params=_params(("arbitrary",)),
        name="residual_ln2",
    )(x1, *ys, top_g, mod3, g, b)


def _rope_tables():
    t = jnp.arange(LAT_LEN)
    r = (t // GRID_W).astype(F32)
    col = (t % GRID_W).astype(F32)
    inv = jnp.power(ROPE_BASE, -jnp.arange(ROPE_HALF, dtype=F32) / ROPE_HALF)
    ang_r = r[:, None] * inv
    ang_c = col[:, None] * inv
    ang = jnp.concatenate([ang_r, ang_r, ang_c, ang_c], axis=-1)
    cos = jnp.tile(jnp.cos(ang), (1, 2))
    sin = jnp.tile(jnp.sin(ang), (1, 2))
    sign = jnp.where((jnp.arange(LANES) % (2 * ROPE_HALF)) < ROPE_HALF, -1.0, 1.0).astype(F32)
    cos = jnp.concatenate([jnp.ones((TM, LANES), F32), cos], axis=0)
    sin = jnp.concatenate([jnp.zeros((TM, LANES), F32), sin * sign], axis=0)
    return cos, sin


def _routing(top_idx, rank, counts):
    n_tok = top_idx.shape[0]
    n_assign = n_tok * TOP_K
    padded = (counts + MOE_BLOCK - 1) // MOE_BLOCK * MOE_BLOCK
    pend = jnp.cumsum(padded)
    pstart = pend - padded
    dest_tk = pstart[top_idx] + rank
    n_blocks = -(-n_assign // MOE_BLOCK) + N_EXPERTS
    blk_start = jnp.arange(n_blocks, dtype=jnp.int32) * MOE_BLOCK
    blk_exp = jnp.minimum(jnp.sum((pend[None, :] <= blk_start[:, None]).astype(jnp.int32), axis=1),
                          N_EXPERTS - 1)
    n_used = (pend[-1] // MOE_BLOCK).astype(jnp.int32).reshape(1)
    max_pad = MOE_BLOCK - 1
    pad_ok = jnp.arange(max_pad, dtype=jnp.int32)[None, :] < (padded - counts)[:, None]
    pad_keys = jnp.where(pad_ok, 2 * jnp.arange(N_EXPERTS, dtype=jnp.int32)[:, None] + 1,
                         2 * N_EXPERTS + 1).reshape(-1)
    keys = jnp.concatenate([2 * top_idx.reshape(n_assign), pad_keys])
    toks = jnp.concatenate([jnp.arange(n_assign, dtype=jnp.int32) // TOP_K,
                            jnp.zeros((N_EXPERTS * max_pad,), jnp.int32)])
    _, row_tok = lax.sort((keys, toks), num_keys=1, is_stable=True)
    row_tok = jnp.concatenate([row_tok, jnp.zeros((n_blocks * MOE_BLOCK - row_tok.shape[0],), jnp.int32)])
    return row_tok, blk_exp, dest_tk, n_used


def kernel(x_prompt, x_sample, c, cache_k, cache_v, c_ctx, w_ada, b_ada, w_in, lambda_q1, lambda_k1, lambda_q2, lambda_k2, subln_w, gmlp_ln_g, gmlp_ln_b, w_spatial, b_spatial, b_gate, w_pa, w_pb, w_o, ln1_g, ln1_b, w_router, b_router, w_up, b_up, w_down, b_down, ln2_g, ln2_b):
    l = 0
    xp = x_prompt.reshape(N_CTX_B * CTX_LEN, D_MODEL)
    xs = x_sample.reshape(N_LAT_B * LAT_LEN, D_MODEL)

    cond = jnp.concatenate([c_ctx[None, :], c, jnp.zeros((16 - 1 - N_LAT_B, D_MODEL), F32)], axis=0)
    mod3 = _modulation(cond, w_ada[l], b_ada[l][None, :]).reshape(16, 1, 6 * D_MODEL)

    cos_t, sin_t = _rope_tables()
    (q_all, k_all, v_all, new_k, new_v, ug, vg, sa, sb) = _input_projection(
        xp, xs, mod3, w_in[l].astype(BF16), cos_t, sin_t, b_gate[l],
        gmlp_ln_g[l][None, :], gmlp_ln_b[l][None, :])

    lam = (jnp.exp(jnp.sum(lambda_q1[l] * lambda_k1[l])) - jnp.exp(jnp.sum(lambda_q2[l] * lambda_k2[l]))
           + LAM_INIT).reshape(1).astype(F32)
    subw = (subln_w[l] * (1.0 - LAM_INIT))[None, :]
    ctx_k = cache_k[:, l].reshape(N_LAT_B, N_HEADS, PAST_LEN, HEAD_W)
    ctx_v = cache_v[:, l]
    a_ctx = _attention(lam, q_all, k_all, v_all, None, None, subw, latent=False)
    a_lat = _attention(lam, q_all, k_all, v_all, ctx_k, ctx_v, subw, latent=True)

    bs_full = jnp.repeat(b_spatial[l].T, D_MODEL // GMLP_GROUPS, axis=1)
    wr = jnp.pad(w_router[l], ((0, 0), (0, LANES - N_EXPERTS)))
    wr_hi = wr.astype(BF16)
    wr_lo = (wr - wr_hi.astype(F32)).astype(BF16)
    br = jnp.concatenate([b_router[l], jnp.full((LANES - N_EXPERTS,), NEG_BIG, F32)])[None, :]
    x1, h2, top_i, top_g, cnt = _mix(
        a_ctx, a_lat, ug, vg, sa, sb, xp, xs, mod3, w_spatial[l].astype(BF16), bs_full,
        w_pa[l].astype(BF16), w_pb[l].astype(BF16), w_o[l].astype(BF16),
        ln1_g[l][None, :], ln1_b[l][None, :], wr_hi, wr_lo, br)

    row_tok, blk_exp, dest_tk, n_used = _routing(
        top_i[:, :TOP_K], top_i[:, TOP_K:2 * TOP_K], cnt[0, :N_EXPERTS].astype(jnp.int32))
    xb = h2[row_tok]
    yb = _experts(blk_exp, n_used, xb, w_up[l], b_up[l][:, None, :], w_down[l], b_down[l][:, None, :])
    ys = [yb[dest_tk[:, k]] for k in range(TOP_K)]

    y_prompt, y_sample = _final(x1, ys, top_g, mod3, ln2_g[l][None, :], ln2_b[l][None, :])
    return (y_prompt.reshape(N_CTX_B, CTX_LEN, D_MODEL),
            y_sample.reshape(N_LAT_B, LAT_LEN, D_MODEL),
            new_k.reshape(N_CTX_B, DEPTH, N_HEADS, CTX_LEN, 2, HEAD_DIM),
            new_v.reshape(N_CTX_B, DEPTH, N_HEADS, CTX_LEN, HEAD_W))
```

```python
import functools
import math

import jax
import jax.numpy as jnp
from jax import lax
from jax.experimental import pallas as pl
from jax.experimental.pallas import tpu as pltpu

F32 = jnp.float32
BF16 = jnp.bfloat16

D_MODEL = 1024
N_CTX_B = 16
CTX_LEN = 256
N_LAT_B = 8
LAT_LEN = 4096
PAST_LEN = 256
GRID_W = 64
N_HEADS = 8
HEAD_DIM = 64
HEAD_W = 2 * HEAD_DIM
ROPE_HALF = HEAD_DIM // 4
ROPE_BASE = 10000.0
GMLP_GROUPS = 4
CHUNK = 128
N_EXPERTS = 32
TOP_K = 4
D_EXPERT = 1024
SWIGLU_LIMIT = 7.0
SWIGLU_ALPHA = 1.702
MOE_BLOCK = 256
LN_EPS = 1e-5
DEPTH = 1
N_SEG = 7

TM = 256
NT_CTX = N_CTX_B * CTX_LEN // TM
NT_LAT = N_LAT_B * LAT_LEN // TM
NT = NT_CTX + NT_LAT
T_ALL = NT * TM
LAT_TILES = LAT_LEN // TM
LANES = 128
KEY_CHUNK = 512
SCORE_LEAD = 2
ONES_ROWS = 16
NEG_BIG = -1e30
VMEM_LIMIT = 56 * 1024 * 1024

ALPHA = (2.0 * DEPTH) ** 0.25
LAM_INIT = 0.8 - 0.6 * math.exp(-0.3 * 0)
Q_SCALE = HEAD_DIM ** -0.5 * math.log2(math.e)


def _ctx_idx(i):
    return jnp.minimum(i, NT_CTX - 1)


def _lat_idx(i):
    return jnp.maximum(i - NT_CTX, 0)


def _mod_row(i):
    return jnp.where(i < NT_CTX, 0, 1 + (i - NT_CTX) // LAT_TILES)


def _layer_norm(x, g, b):
    mu = jnp.mean(x, axis=-1, keepdims=True)
    xc = x - mu
    var = jnp.mean(xc * xc, axis=-1, keepdims=True)
    return xc * lax.rsqrt(var + LN_EPS) * g + b


def _gelu(x):
    return 0.5 * x * (1.0 + lax.erf(x * (1.0 / math.sqrt(2.0))))


def _split(x):
    hi = x.astype(BF16)
    return hi, (x - hi.astype(F32)).astype(BF16)


def _params(sem):
    return pltpu.CompilerParams(dimension_semantics=sem, vmem_limit_bytes=VMEM_LIMIT)


def _mod_kernel(c_ref, w_ref, b_ref, o_ref):
    c = c_ref[...]
    s = c * jax.nn.sigmoid(c)
    s_hi, s_lo = _split(s)
    w_hi, w_lo = _split(w_ref[...])
    o_ref[...] = (jnp.dot(s_hi, w_hi, preferred_element_type=F32)
                  + jnp.dot(s_lo, w_hi, preferred_element_type=F32)
                  + jnp.dot(s_hi, w_lo, preferred_element_type=F32)) + b_ref[...]


def _modulation(cond, w_ada, b_ada):
    n = cond.shape[0]
    return pl.pallas_call(
        _mod_kernel,
        grid=(6,),
        in_specs=[pl.BlockSpec((n, D_MODEL), lambda j: (0, 0)),
                  pl.BlockSpec((D_MODEL, D_MODEL), lambda j: (0, j)),
                  pl.BlockSpec((1, D_MODEL), lambda j: (0, j))],
        out_specs=pl.BlockSpec((n, D_MODEL), lambda j: (0, j)),
        out_shape=jax.ShapeDtypeStruct((n, 6 * D_MODEL), F32),
        compiler_params=_params(("arbitrary",)),
        name="adaln_mod",
    )(cond, w_ada, b_ada)


def _proj_kernel(xp_ref, xs_ref, mod_ref, w_ref, cos_ref, sin_ref, bg_ref, lg_ref, lb_ref,
                 q_ref, k_ref, v_ref, nk_ref, nv_ref, ug_ref, vg_ref, sa_ref, sb_ref):
    i = pl.program_id(0)
    is_ctx = i < NT_CTX
    x = jnp.where(is_ctx, xp_ref[...], xs_ref[...])
    mod = mod_ref[0]
    sh1 = mod[:, 0:D_MODEL]
    sc1 = mod[:, D_MODEL:2 * D_MODEL]
    h = (x * (1.0 + sc1) + sh1).astype(BF16)

    def seg(s):
        return jnp.dot(h, w_ref[:, s * D_MODEL:(s + 1) * D_MODEL], preferred_element_type=F32)

    cos = cos_ref[...]
    sin = sin_ref[...]
    lane = lax.broadcasted_iota(jnp.int32, (TM, LANES), 1)
    first = (lane % (2 * ROPE_HALF)) < ROPE_HALF

    def rope(zh):
        up = pltpu.roll(zh, LANES - ROPE_HALF, 1)
        dn = pltpu.roll(zh, ROPE_HALF, 1)
        return zh * cos + jnp.where(first, up, dn) * sin

    zq = seg(0)
    for hd in range(N_HEADS):
        q_ref[0, hd] = (rope(zq[:, hd * HEAD_W:(hd + 1) * HEAD_W]) * Q_SCALE).astype(BF16)
    zk = seg(1)
    for hd in range(N_HEADS):
        k_ref[0, hd] = rope(zk[:, hd * HEAD_W:(hd + 1) * HEAD_W]).astype(BF16)
    zv = seg(2)
    for hd in range(N_HEADS):
        v_ref[0, hd] = zv[:, hd * HEAD_W:(hd + 1) * HEAD_W].astype(BF16)

    @pl.when(is_ctx)
    def _():
        for hd in range(N_HEADS):
            nk_ref[0, hd] = zk[:, hd * HEAD_W:(hd + 1) * HEAD_W]
            nv_ref[0, hd] = zv[:, hd * HEAD_W:(hd + 1) * HEAD_W]

    ug_ref[...] = _gelu(seg(3)).astype(BF16)
    vg_ref[...] = _layer_norm(_gelu(seg(4)), lg_ref[...], lb_ref[...]).astype(BF16)
    sa_ref[...] = jax.nn.sigmoid(seg(5) + bg_ref[0:1, :]).astype(BF16)
    sb_ref[...] = jax.nn.sigmoid(seg(6) + bg_ref[1:2, :]).astype(BF16)


def _input_projection(xp, xs, mod3, w_in, cos_t, sin_t, b_gate, ln_g, ln_b):
    tile = lambda i: (i, 0)
    head_blk = (1, N_HEADS, TM, HEAD_W)
    tok_spec = pl.BlockSpec((TM, D_MODEL), tile)
    return pl.pallas_call(
        _proj_kernel,
        grid=(NT,),
        in_specs=[
            pl.BlockSpec((TM, D_MODEL), lambda i: (_ctx_idx(i), 0)),
            pl.BlockSpec((TM, D_MODEL), lambda i: (_lat_idx(i), 0)),
            pl.BlockSpec((1, 1, 6 * D_MODEL), lambda i: (_mod_row(i), 0, 0)),
            pl.BlockSpec((D_MODEL, N_SEG * D_MODEL), lambda i: (0, 0)),
            pl.BlockSpec((TM, LANES), lambda i: (jnp.where(i < NT_CTX, 0, 1 + (i - NT_CTX) % LAT_TILES), 0)),
            pl.BlockSpec((TM, LANES), lambda i: (jnp.where(i < NT_CTX, 0, 1 + (i - NT_CTX) % LAT_TILES), 0)),
            pl.BlockSpec((2, D_MODEL), lambda i: (0, 0)),
            pl.BlockSpec((1, D_MODEL), lambda i: (0, 0)),
            pl.BlockSpec((1, D_MODEL), lambda i: (0, 0)),
        ],
        out_specs=[
            pl.BlockSpec(head_blk, lambda i: (i, 0, 0, 0)),
            pl.BlockSpec(head_blk, lambda i: (i, 0, 0, 0)),
            pl.BlockSpec(head_blk, lambda i: (i, 0, 0, 0)),
            pl.BlockSpec(head_blk, lambda i: (_ctx_idx(i), 0, 0, 0)),
            pl.BlockSpec(head_blk, lambda i: (_ctx_idx(i), 0, 0, 0)),
            tok_spec, tok_spec, tok_spec, tok_spec,
        ],
        out_shape=[
            jax.ShapeDtypeStruct((NT, N_HEADS, TM, HEAD_W), BF16),
            jax.ShapeDtypeStruct((NT, N_HEADS, TM, HEAD_W), BF16),
            jax.ShapeDtypeStruct((NT, N_HEADS, TM, HEAD_W), BF16),
            jax.ShapeDtypeStruct((NT_CTX, N_HEADS, TM, HEAD_W), F32),
            jax.ShapeDtypeStruct((NT_CTX, N_HEADS, TM, HEAD_W), F32),
            jax.ShapeDtypeStruct((T_ALL, D_MODEL), BF16),
            jax.ShapeDtypeStruct((T_ALL, D_MODEL), BF16),
            jax.ShapeDtypeStruct((T_ALL, D_MODEL), BF16),
            jax.ShapeDtypeStruct((T_ALL, D_MODEL), BF16),
        ],
        compiler_params=_params(("arbitrary",)),
        name="input_projection",
    )(xp, xs, mod3, w_in, cos_t, sin_t, b_gate, ln_g, ln_b)


def _attn_kernel(n_main, has_ctx, lam_ref, q_ref, k_ref, v_ref, *rest):
    if has_ctx:
        ck_ref, cv_ref, sub_ref, o_ref, kall, vt_all = rest
    else:
        sub_ref, o_ref, kall, vt_all = rest
    n_keys = kall.shape[0]

    @pl.when(pl.program_id(2) == 0)
    def _():
        for c in range(n_main):
            kall[c * TM:(c + 1) * TM, :] = k_ref[c, 0]
            vt_all[0:HEAD_W, c * TM:(c + 1) * TM] = v_ref[c, 0].astype(F32).T.astype(BF16)
        if has_ctx:
            kall[n_main * TM:n_keys, :] = ck_ref[0, 0].astype(BF16)
            vt_all[0:HEAD_W, n_main * TM:n_keys] = cv_ref[0, 0].T.astype(BF16)
        r = lax.broadcasted_iota(jnp.int32, (ONES_ROWS, n_keys), 0)
        vt_all[HEAD_W:HEAD_W + ONES_ROWS, :] = jnp.where(r == 0, 1.0, 0.0).astype(BF16)

    q = q_ref[0, 0]
    lane = lax.broadcasted_iota(jnp.int32, (TM, HEAD_W), 1)
    zero = jnp.zeros_like(q)
    qq = jnp.concatenate([jnp.where(lane < HEAD_DIM, q, zero),
                          jnp.where(lane >= HEAD_DIM, q, zero)], axis=0)

    def scores(st, sz):
        return lax.dot_general(kall[st:st + sz, :], qq, (((1,), (1,)), ((), ())),
                               preferred_element_type=F32)

    chunks = [(st, min(KEY_CHUNK, n_keys - st)) for st in range(0, n_keys, KEY_CHUNK)]
    pending = [scores(*ch) for ch in chunks[:SCORE_LEAD]]
    m = acc = None
    for c, (st, sz) in enumerate(chunks):
        s = pending.pop(0)
        if c + SCORE_LEAD < len(chunks):
            pending.append(scores(*chunks[c + SCORE_LEAD]))
        m_c = jnp.max(s, axis=0, keepdims=True)
        m_new = m_c if c == 0 else jnp.maximum(m, m_c)
        e = jnp.exp2(s - m_new).astype(BF16)
        pv = jnp.dot(vt_all[:, st:st + sz], e, preferred_element_type=F32)
        acc = pv if c == 0 else jnp.exp2(m - m_new) * acc + pv
        m = m_new
    l = acc[HEAD_W:HEAD_W + 1, :]
    c1 = 1.0 / l[:, :TM]
    c2 = lam_ref[0] / l[:, TM:]
    a = (acc[:HEAD_W, :TM] * c1 - acc[:HEAD_W, TM:] * c2).T
    ms = jnp.mean(a * a, axis=-1, keepdims=True)
    o_ref[...] = (a * lax.rsqrt(ms + LN_EPS) * sub_ref[...]).astype(BF16)


def _attention(lam, q_all, k_all, v_all, ctx_k, ctx_v, subw, latent):
    if latent:
        n_b, n_q, n_main, base = N_LAT_B, LAT_TILES, LAT_TILES, NT_CTX
        n_keys = LAT_LEN + PAST_LEN
    else:
        n_b, n_q, n_main, base = N_CTX_B, 1, 1, 0
        n_keys = CTX_LEN
    q_spec = pl.BlockSpec((1, 1, TM, HEAD_W), lambda b, h, t: (base + b * n_q + t, h, 0, 0))
    kv_spec = pl.BlockSpec((n_main, 1, TM, HEAD_W), lambda b, h, t: (base // n_main + b, h, 0, 0))
    in_specs = [pl.BlockSpec(memory_space=pltpu.SMEM), q_spec, kv_spec, kv_spec]
    args = [lam, q_all, k_all, v_all]
    if latent:
        c_spec = pl.BlockSpec((1, 1, PAST_LEN, HEAD_W), lambda b, h, t: (b, h, 0, 0))
        in_specs += [c_spec, c_spec]
        args += [ctx_k, ctx_v]
    in_specs.append(pl.BlockSpec((1, HEAD_W), lambda b, h, t: (0, 0)))
    args.append(subw)
    return pl.pallas_call(
        functools.partial(_attn_kernel, n_main, latent),
        grid=(n_b, N_HEADS, n_q),
        in_specs=in_specs,
        out_specs=pl.BlockSpec((TM, HEAD_W), lambda b, h, t: (b * n_q + t, h)),
        out_shape=jax.ShapeDtypeStruct((n_b * n_q * TM, D_MODEL), BF16),
        scratch_shapes=[pltpu.VMEM((n_keys, HEAD_W), BF16),
                        pltpu.VMEM((HEAD_W + ONES_ROWS, n_keys), BF16)],
        compiler_params=_params(("arbitrary", "arbitrary", "arbitrary")),
        name="diff_attention_lat" if latent else "diff_attention_ctx",
    )(*args)


def _mix_kernel(ac_ref, al_ref, ug_ref, vg_ref, sa_ref, sb_ref, xp_ref, xs_ref, mod_ref,
                ws_ref, bs_ref, wpa_ref, wpb_ref, wo_ref, g_ref, b_ref, wrh_ref, wrl_ref, br_ref,
                x1_ref, h2_ref, ti_ref, tg_ref, cnt_ref, run_ref):
    i = pl.program_id(0)
    is_ctx = i < NT_CTX
    a = jnp.where(is_ctx, ac_ref[...], al_ref[...])
    x = jnp.where(is_ctx, xp_ref[...], xs_ref[...])
    mod = mod_ref[0]
    g1 = mod[:, 2 * D_MODEL:3 * D_MODEL]
    sh2 = mod[:, 3 * D_MODEL:4 * D_MODEL]
    sc2 = mod[:, 4 * D_MODEL:5 * D_MODEL]

    gw = D_MODEL // GMLP_GROUPS
    chunks = []
    for c in range(TM // CHUNK):
        groups = []
        for g in range(GMLP_GROUPS):
            vc = vg_ref[c * CHUNK:(c + 1) * CHUNK, g * gw:(g + 1) * gw]
            groups.append(jnp.dot(ws_ref[g], vc, preferred_element_type=F32))
        chunks.append(jnp.concatenate(groups, axis=1) + bs_ref[...])
    sp = jnp.concatenate(chunks, axis=0)
    gm = (ug_ref[...].astype(F32) * sp).astype(BF16)

    pa = jnp.dot(a, wpa_ref[...], preferred_element_type=F32)
    pb = jnp.dot(gm, wpb_ref[...], preferred_element_type=F32)
    merged = (sa_ref[...].astype(F32) * pa + sb_ref[...].astype(F32) * pb).astype(BF16)
    mix = jnp.dot(merged, wo_ref[...], preferred_element_type=F32)
    x1 = _layer_norm(ALPHA * x + g1 * mix, g_ref[...], b_ref[...])
    x1_ref[...] = x1
    h2 = x1 * (1.0 + sc2) + sh2
    hi, lo = _split(h2)
    h2_ref[...] = hi
    logits = (jnp.dot(hi, wrh_ref[...], preferred_element_type=F32)
              + jnp.dot(lo, wrh_ref[...], preferred_element_type=F32)
              + jnp.dot(hi, wrl_ref[...], preferred_element_type=F32)) + br_ref[...]
    lane = lax.broadcasted_iota(jnp.int32, (TM, LANES), 1).astype(F32)
    vals, idxs = [], []
    for _ in range(TOP_K):
        mx = jnp.max(logits, axis=-1, keepdims=True)
        ix = jnp.min(jnp.where(logits == mx, lane, float(LANES)), axis=-1, keepdims=True)
        vals.append(mx)
        idxs.append(ix)
        logits = jnp.where(lane == ix, NEG_BIG * 2.0, logits)
    es = [jnp.exp(v - vals[0]) for v in vals]
    inv = 1.0 / (es[0] + es[1] + es[2] + es[3])

    @pl.when(i == 0)
    def _():
        run_ref[...] = jnp.zeros_like(run_ref)

    hot = [lane == ix for ix in idxs]
    memb = jnp.zeros((TM, LANES), F32)
    for hk in hot:
        memb = jnp.where(hk, 1.0, memb)
    row = lax.broadcasted_iota(jnp.int32, (TM, TM), 0)
    col = lax.broadcasted_iota(jnp.int32, (TM, TM), 1)
    before = jnp.where(row > col, 1.0, 0.0).astype(BF16)
    base = run_ref[...]
    rank_all = jnp.dot(before, memb.astype(BF16), preferred_element_type=F32) + base
    total = base + jnp.sum(memb, axis=0, keepdims=True)
    run_ref[...] = total
    cnt_ref[...] = total

    ti = jnp.zeros((TM, LANES), F32)
    tg = jnp.zeros((TM, LANES), F32)
    for k in range(TOP_K):
        rank_k = jnp.sum(jnp.where(hot[k], rank_all, 0.0), axis=-1, keepdims=True)
        ti = jnp.where(lane == float(k), idxs[k], ti)
        ti = jnp.where(lane == float(TOP_K + k), rank_k, ti)
        tg = jnp.where(lane == float(k), es[k] * inv, tg)
    ti_ref[...] = ti.astype(jnp.int32)
    tg_ref[...] = tg


def _mix(a_ctx, a_lat, ug, vg, sa, sb, xp, xs, mod3, ws, bs, wpa, wpb, wo, g, b, wrh, wrl, br):
    tile = lambda i: (i, 0)
    full2 = lambda i: (0, 0)
    tok = pl.BlockSpec((TM, D_MODEL), tile)
    ctx = pl.BlockSpec((TM, D_MODEL), lambda i: (_ctx_idx(i), 0))
    lat = pl.BlockSpec((TM, D_MODEL), lambda i: (_lat_idx(i), 0))
    wsq = pl.BlockSpec((D_MODEL, D_MODEL), full2)
    vec = pl.BlockSpec((1, D_MODEL), full2)
    return pl.pallas_call(
        _mix_kernel,
        grid=(NT,),
        in_specs=[ctx, lat, tok, tok, tok, tok, ctx, lat,
                  pl.BlockSpec((1, 1, 6 * D_MODEL), lambda i: (_mod_row(i), 0, 0)),
                  pl.BlockSpec((GMLP_GROUPS, CHUNK, CHUNK), lambda i: (0, 0, 0)),
                  pl.BlockSpec((CHUNK, D_MODEL), full2),
                  wsq, wsq, wsq, vec, vec,
                  pl.BlockSpec((D_MODEL, LANES), full2),
                  pl.BlockSpec((D_MODEL, LANES), full2),
                  pl.BlockSpec((1, LANES), full2)],
        out_specs=[tok, tok, pl.BlockSpec((TM, LANES), tile), pl.BlockSpec((TM, LANES), tile),
                   pl.BlockSpec((1, LANES), full2)],
        out_shape=[jax.ShapeDtypeStruct((T_ALL, D_MODEL), F32),
                   jax.ShapeDtypeStruct((T_ALL, D_MODEL), BF16),
                   jax.ShapeDtypeStruct((T_ALL, LANES), jnp.int32),
                   jax.ShapeDtypeStruct((T_ALL, LANES), F32),
                   jax.ShapeDtypeStruct((1, LANES), F32)],
        scratch_shapes=[pltpu.VMEM((1, LANES), F32)],
        compiler_params=_params(("arbitrary",)),
        name="mix_ln1_router",
    )(a_ctx, a_lat, ug, vg, sa, sb, xp, xs, mod3, ws, bs, wpa, wpb, wo, g, b, wrh, wrl, br)


def _expert_kernel(be_ref, nu_ref, x_ref, wu_ref, bu_ref, wd_ref, bd_ref, y_ref, wu_bf, wd_bf):
    i = pl.program_id(0)
    used = i < nu_ref[0]
    new_expert = jnp.logical_or(i == 0, be_ref[i] != be_ref[jnp.maximum(i - 1, 0)])

    @pl.when(jnp.logical_and(used, new_expert))
    def _():
        wu_bf[...] = wu_ref[0].astype(BF16)
        wd_bf[...] = wd_ref[0].astype(BF16)

    @pl.when(used)
    def _():
        hu = jnp.dot(x_ref[...], wu_bf[...], preferred_element_type=F32) + bu_ref[0]
        glu = jnp.minimum(hu[:, :D_EXPERT], SWIGLU_LIMIT)
        lin = jnp.clip(hu[:, D_EXPERT:], -SWIGLU_LIMIT, SWIGLU_LIMIT)
        act = glu * jax.nn.sigmoid(SWIGLU_ALPHA * glu) * (lin + 1.0)
        y = jnp.dot(act.astype(BF16), wd_bf[...], preferred_element_type=F32) + bd_ref[0]
        y_ref[...] = y.astype(BF16)

    @pl.when(jnp.logical_not(used))
    def _():
        y_ref[...] = jnp.zeros_like(y_ref)


def _experts(blk_exp, n_used, xb, w_up, b_up, w_down, b_down):
    n_blocks = xb.shape[0] // MOE_BLOCK
    grid_spec = pltpu.PrefetchScalarGridSpec(
        num_scalar_prefetch=2,
        grid=(n_blocks,),
        in_specs=[pl.BlockSpec((MOE_BLOCK, D_MODEL), lambda i, be, nu: (i, 0)),
                  pl.BlockSpec((1, D_MODEL, 2 * D_EXPERT), lambda i, be, nu: (be[i], 0, 0)),
                  pl.BlockSpec((1, 1, 2 * D_EXPERT), lambda i, be, nu: (be[i], 0, 0)),
                  pl.BlockSpec((1, D_EXPERT, D_MODEL), lambda i, be, nu: (be[i], 0, 0)),
                  pl.BlockSpec((1, 1, D_MODEL), lambda i, be, nu: (be[i], 0, 0))],
        out_specs=pl.BlockSpec((MOE_BLOCK, D_MODEL), lambda i, be, nu: (i, 0)),
        scratch_shapes=[pltpu.VMEM((D_MODEL, 2 * D_EXPERT), BF16),
                        pltpu.VMEM((D_EXPERT, D_MODEL), BF16)],
    )
    return pl.pallas_call(
        _expert_kernel,
        grid_spec=grid_spec,
        out_shape=jax.ShapeDtypeStruct((n_blocks * MOE_BLOCK, D_MODEL), BF16),
        compiler_params=_params(("arbitrary",)),
        name="expert_ffn",
    )(blk_exp, n_used, xb, w_up, b_up, w_down, b_down)


def _final_kernel(x1_ref, y0_ref, y1_ref, y2_ref, y3_ref, tg_ref, mod_ref, g_ref, b_ref,
                  yp_ref, ys_ref):
    i = pl.program_id(0)
    g2 = mod_ref[0][:, 5 * D_MODEL:6 * D_MODEL]
    gates = tg_ref[...]
    ffn = jnp.zeros((TM, D_MODEL), F32)
    for k, yk_ref in enumerate((y0_ref, y1_ref, y2_ref, y3_ref)):
        ffn = ffn + gates[:, k:k + 1] * yk_ref[...].astype(F32)
    y = _layer_norm(ALPHA * x1_ref[...] + g2 * ffn, g_ref[...], b_ref[...])

    @pl.when(i < NT_CTX)
    def _():
        yp_ref[...] = y

    @pl.when(i >= NT_CTX)
    def _():
        ys_ref[...] = y


def _final(x1, ys, top_g, mod3, g, b):
    tok = pl.BlockSpec((TM, D_MODEL), lambda i: (i, 0))
    vec = pl.BlockSpec((1, D_MODEL), lambda i: (0, 0))
    return pl.pallas_call(
        _final_kernel,
        grid=(NT,),
        in_specs=[tok, tok, tok, tok, tok, pl.BlockSpec((TM, LANES), lambda i: (i, 0)),
                  pl.BlockSpec((1, 1, 6 * D_MODEL), lambda i: (_mod_row(i), 0, 0)), vec, vec],
        out_specs=[pl.BlockSpec((TM, D_MODEL), lambda i: (_ctx_idx(i), 0)),
                   pl.BlockSpec((TM, D_MODEL), lambda i: (_lat_idx(i), 0))],
        out_shape=[jax.ShapeDtypeStruct((NT_CTX * TM, D_MODEL), F32),
                   jax.ShapeDtypeStruct((NT_LAT * TM, D_MODEL), F32)],
        compiler_params=_params(("arbitrary",)),
        name="residual_ln2",
    )(x1, *ys, top_g, mod3, g, b)


def _rope_tables():
    t = jnp.arange(LAT_LEN)
    r = (t // GRID_W).astype(F32)
    col = (t % GRID_W).astype(F32)
    inv = jnp.power(ROPE_BASE, -jnp.arange(ROPE_HALF, dtype=F32) / ROPE_HALF)
    ang_r = r[:, None] * inv
    ang_c = col[:, None] * inv
    ang = jnp.concatenate([ang_r, ang_r, ang_c, ang_c], axis=-1)
    cos = jnp.tile(jnp.cos(ang), (1, 2))
    sin = jnp.tile(jnp.sin(ang), (1, 2))
    sign = jnp.where((jnp.arange(LANES) % (2 * ROPE_HALF)) < ROPE_HALF, -1.0, 1.0).astype(F32)
    cos = jnp.concatenate([jnp.ones((TM, LANES), F32), cos], axis=0)
    sin = jnp.concatenate([jnp.zeros((TM, LANES), F32), sin * sign], axis=0)
    return cos, sin


def _routing(top_idx, rank, counts):
    n_tok = top_idx.shape[0]
    n_assign = n_tok * TOP_K
    padded = (counts + MOE_BLOCK - 1) // MOE_BLOCK * MOE_BLOCK
    pend = jnp.cumsum(padded)
    pstart = pend - padded
    dest_tk = pstart[top_idx] + rank
    n_blocks = -(-n_assign // MOE_BLOCK) + N_EXPERTS
    blk_start = jnp.arange(n_blocks, dtype=jnp.int32) * MOE_BLOCK
    blk_exp = jnp.minimum(jnp.sum((pend[None, :] <= blk_start[:, None]).astype(jnp.int32), axis=1),
                          N_EXPERTS - 1)
    n_used = (pend[-1] // MOE_BLOCK).astype(jnp.int32).reshape(1)
    max_pad = MOE_BLOCK - 1
    pad_ok = jnp.arange(max_pad, dtype=jnp.int32)[None, :] < (padded - counts)[:, None]
    pad_keys = jnp.where(pad_ok, 2 * jnp.arange(N_EXPERTS, dtype=jnp.int32)[:, None] + 1,
                         2 * N_EXPERTS + 1).reshape(-1)
    keys = jnp.concatenate([2 * top_idx.reshape(n_assign), pad_keys])
    toks = jnp.concatenate([jnp.arange(n_assign, dtype=jnp.int32) // TOP_K,
                            jnp.zeros((N_EXPERTS * max_pad,), jnp.int32)])
    _, row_tok = lax.sort((keys, toks), num_keys=1, is_stable=True)
    row_tok = jnp.concatenate([row_tok, jnp.zeros((n_blocks * MOE_BLOCK - row_tok.shape[0],), jnp.int32)])
    return row_tok, blk_exp, dest_tk, n_used


def kernel(x_prompt, x_sample, c, cache_k, cache_v, c_ctx, w_ada, b_ada, w_in, lambda_q1, lambda_k1, lambda_q2, lambda_k2, subln_w, gmlp_ln_g, gmlp_ln_b, w_spatial, b_spatial, b_gate, w_pa, w_pb, w_o, ln1_g, ln1_b, w_router, b_router, w_up, b_up, w_down, b_down, ln2_g, ln2_b):
    l = 0
    xp = x_prompt.reshape(N_CTX_B * CTX_LEN, D_MODEL)
    xs = x_sample.reshape(N_LAT_B * LAT_LEN, D_MODEL)

    cond = jnp.concatenate([c_ctx[None, :], c, jnp.zeros((16 - 1 - N_LAT_B, D_MODEL), F32)], axis=0)
    mod3 = _modulation(cond, w_ada[l], b_ada[l][None, :]).reshape(16, 1, 6 * D_MODEL)

    cos_t, sin_t = _rope_tables()
    (q_all, k_all, v_all, new_k, new_v, ug, vg, sa, sb) = _input_projection(
        xp, xs, mod3, w_in[l].astype(BF16), cos_t, sin_t, b_gate[l],
        gmlp_ln_g[l][None, :], gmlp_ln_b[l][None, :])

    lam = (jnp.exp(jnp.sum(lambda_q1[l] * lambda_k1[l])) - jnp.exp(jnp.sum(lambda_q2[l] * lambda_k2[l]))
           + LAM_INIT).reshape(1).astype(F32)
    subw = (subln_w[l] * (1.0 - LAM_INIT))[None, :]
    ctx_k = cache_k[:, l].reshape(N_LAT_B, N_HEADS, PAST_LEN, HEAD_W)
    ctx_v = cache_v[:, l]
    a_ctx = _attention(lam, q_all, k_all, v_all, None, None, subw, latent=False)
    a_lat = _attention(lam, q_all, k_all, v_all, ctx_k, ctx_v, subw, latent=True)

    bs_full = jnp.repeat(b_spatial[l].T, D_MODEL // GMLP_GROUPS, axis=1)
    wr = jnp.pad(w_router[l], ((0, 0), (0, LANES - N_EXPERTS)))
    wr_hi = wr.astype(BF16)
    wr_lo = (wr - wr_hi.astype(F32)).astype(BF16)
    br = jnp.concatenate([b_router[l], jnp.full((LANES - N_EXPERTS,), NEG_BIG, F32)])[None, :]
    x1, h2, top_i, top_g, cnt = _mix(
        a_ctx, a_lat, ug, vg, sa, sb, xp, xs, mod3, w_spatial[l].astype(BF16), bs_full,
        w_pa[l].astype(BF16), w_pb[l].astype(BF16), w_o[l].astype(BF16),
        ln1_g[l][None, :], ln1_b[l][None, :], wr_hi, wr_lo, br)

    row_tok, blk_exp, dest_tk, n_used = _routing(
        top_i[:, :TOP_K], top_i[:, TOP_K:2 * TOP_K], cnt[0, :N_EXPERTS].astype(jnp.int32))
    xb = h2[row_tok]
    yb = _experts(blk_exp, n_used, xb, w_up[l], b_up[l][:, None, :], w_down[l], b_down[l][:, None, :])
    ys = [yb[dest_tk[:, k]] for k in range(TOP_K)]

    y_prompt, y_sample = _final(x1, ys, top_g, mod3, ln2_g[l][None, :], ln2_b[l][None, :])
    return (y_prompt.reshape(N_CTX_B, CTX_LEN, D_MODEL),
            y_sample.reshape(N_LAT_B, LAT_LEN, D_MODEL),
            new_k.reshape(N_CTX_B, DEPTH, N_HEADS, CTX_LEN, 2, HEAD_DIM),
            new_v.reshape(N_CTX_B, DEPTH, N_HEADS, CTX_LEN, HEAD_W))
```

```python
import functools
import math

import jax
import jax.numpy as jnp
from jax import lax
from jax.experimental import pallas as pl
from jax.experimental.pallas import tpu as pltpu

F32 = jnp.float32
BF16 = jnp.bfloat16

D_MODEL = 1024
N_CTX_B = 16
CTX_LEN = 256
N_LAT_B = 8
LAT_LEN = 4096
PAST_LEN = 256
GRID_W = 64
N_HEADS = 8
HEAD_DIM = 64
HEAD_W = 2 * HEAD_DIM
ROPE_HALF = HEAD_DIM // 4
ROPE_BASE = 10000.0
GMLP_GROUPS = 4
CHUNK = 128
N_EXPERTS = 32
TOP_K = 4
D_EXPERT = 1024
SWIGLU_LIMIT = 7.0
SWIGLU_ALPHA = 1.702
MOE_BLOCK = 256
LN_EPS = 1e-5
DEPTH = 1
N_SEG = 7

TM = 256
NT_CTX = N_CTX_B * CTX_LEN // TM
NT_LAT = N_LAT_B * LAT_LEN // TM
NT = NT_CTX + NT_LAT
T_ALL = NT * TM
LAT_TILES = LAT_LEN // TM
LANES = 128
KEY_CHUNK = 512
SCORE_LEAD = 2
ONES_ROWS = 16
PACK_W = D_MODEL // 2
ROWS_PER_TILE = TM * TOP_K
DMA_UNROLL = 2
NEG_BIG = -1e30
VMEM_LIMIT = 56 * 1024 * 1024

ALPHA = (2.0 * DEPTH) ** 0.25
LAM_INIT = 0.8 - 0.6 * math.exp(-0.3 * 0)
Q_SCALE = HEAD_DIM ** -0.5 * math.log2(math.e)


def _ctx_idx(i):
    return jnp.minimum(i, NT_CTX - 1)


def _lat_idx(i):
    return jnp.maximum(i - NT_CTX, 0)


def _mod_row(i):
    return jnp.where(i < NT_CTX, 0, 1 + (i - NT_CTX) // LAT_TILES)


def _layer_norm(x, g, b):
    mu = jnp.mean(x, axis=-1, keepdims=True)
    xc = x - mu
    var = jnp.mean(xc * xc, axis=-1, keepdims=True)
    return xc * lax.rsqrt(var + LN_EPS) * g + b


def _gelu(x):
    return 0.5 * x * (1.0 + lax.erf(x * (1.0 / math.sqrt(2.0))))


def _split(x):
    hi = x.astype(BF16)
    return hi, (x - hi.astype(F32)).astype(BF16)


def _pack_pairs(x):
    w = x.shape[1] // 2
    a = pltpu.bitcast(x[:, :w].astype(BF16).astype(F32), jnp.uint32)
    b = pltpu.bitcast(x[:, w:].astype(BF16).astype(F32), jnp.uint32)
    return lax.shift_right_logical(a, jnp.uint32(16)) | b


def _unpack_pairs(p):
    a = pltpu.bitcast(lax.shift_left(p, jnp.uint32(16)), F32)
    b = pltpu.bitcast(p & jnp.uint32(0xFFFF0000), F32)
    return jnp.concatenate([a, b], axis=1)


def _params(sem):
    return pltpu.CompilerParams(dimension_semantics=sem, vmem_limit_bytes=VMEM_LIMIT)


def _mod_kernel(c_ref, w_ref, b_ref, o_ref):
    c = c_ref[...]
    s = c * jax.nn.sigmoid(c)
    s_hi, s_lo = _split(s)
    w_hi, w_lo = _split(w_ref[...])
    o_ref[...] = (jnp.dot(s_hi, w_hi, preferred_element_type=F32)
                  + jnp.dot(s_lo, w_hi, preferred_element_type=F32)
                  + jnp.dot(s_hi, w_lo, preferred_element_type=F32)) + b_ref[...]


def _modulation(cond, w_ada, b_ada):
    n = cond.shape[0]
    return pl.pallas_call(
        _mod_kernel,
        grid=(6,),
        in_specs=[pl.BlockSpec((n, D_MODEL), lambda j: (0, 0)),
                  pl.BlockSpec((D_MODEL, D_MODEL), lambda j: (0, j)),
                  pl.BlockSpec((1, D_MODEL), lambda j: (0, j))],
        out_specs=pl.BlockSpec((n, D_MODEL), lambda j: (0, j)),
        out_shape=jax.ShapeDtypeStruct((n, 6 * D_MODEL), F32),
        compiler_params=_params(("arbitrary",)),
        name="adaln_mod",
    )(cond, w_ada, b_ada)


def _proj_kernel(xp_ref, xs_ref, mod_ref, w_ref, cos_ref, sin_ref, bg_ref, lg_ref, lb_ref,
                 q_ref, k_ref, v_ref, nk_ref, nv_ref, ug_ref, vg_ref, sa_ref, sb_ref):
    i = pl.program_id(0)
    is_ctx = i < NT_CTX
    x = jnp.where(is_ctx, xp_ref[...], xs_ref[...])
    mod = mod_ref[0]
    sh1 = mod[:, 0:D_MODEL]
    sc1 = mod[:, D_MODEL:2 * D_MODEL]
    h = (x * (1.0 + sc1) + sh1).astype(BF16)

    def seg(s):
        return jnp.dot(h, w_ref[:, s * D_MODEL:(s + 1) * D_MODEL], preferred_element_type=F32)

    cos = cos_ref[...]
    sin = sin_ref[...]
    lane = lax.broadcasted_iota(jnp.int32, (TM, LANES), 1)
    first = (lane % (2 * ROPE_HALF)) < ROPE_HALF

    def rope(zh):
        up = pltpu.roll(zh, LANES - ROPE_HALF, 1)
        dn = pltpu.roll(zh, ROPE_HALF, 1)
        return zh * cos + jnp.where(first, up, dn) * sin

    zq = seg(0)
    for hd in range(N_HEADS):
        q_ref[0, hd] = (rope(zq[:, hd * HEAD_W:(hd + 1) * HEAD_W]) * Q_SCALE).astype(BF16)
    zk = seg(1)
    for hd in range(N_HEADS):
        k_ref[0, hd] = rope(zk[:, hd * HEAD_W:(hd + 1) * HEAD_W]).astype(BF16)
    zv = seg(2)
    for hd in range(N_HEADS):
        v_ref[0, hd] = zv[:, hd * HEAD_W:(hd + 1) * HEAD_W].astype(BF16)

    @pl.when(is_ctx)
    def _():
        for hd in range(N_HEADS):
            nk_ref[0, hd] = zk[:, hd * HEAD_W:(hd + 1) * HEAD_W]
            nv_ref[0, hd] = zv[:, hd * HEAD_W:(hd + 1) * HEAD_W]

    ug_ref[...] = _gelu(seg(3)).astype(BF16)
    vg_ref[...] = _layer_norm(_gelu(seg(4)), lg_ref[...], lb_ref[...]).astype(BF16)
    sa_ref[...] = jax.nn.sigmoid(seg(5) + bg_ref[0:1, :]).astype(BF16)
    sb_ref[...] = jax.nn.sigmoid(seg(6) + bg_ref[1:2, :]).astype(BF16)


def _input_projection(xp, xs, mod3, w_in, cos_t, sin_t, b_gate, ln_g, ln_b):
    tile = lambda i: (i, 0)
    head_blk = (1, N_HEADS, TM, HEAD_W)
    tok_spec = pl.BlockSpec((TM, D_MODEL), tile)
    return pl.pallas_call(
        _proj_kernel,
        grid=(NT,),
        in_specs=[
            pl.BlockSpec((TM, D_MODEL), lambda i: (_ctx_idx(i), 0)),
            pl.BlockSpec((TM, D_MODEL), lambda i: (_lat_idx(i), 0)),
            pl.BlockSpec((1, 1, 6 * D_MODEL), lambda i: (_mod_row(i), 0, 0)),
            pl.BlockSpec((D_MODEL, N_SEG * D_MODEL), lambda i: (0, 0)),
            pl.BlockSpec((TM, LANES), lambda i: (jnp.where(i < NT_CTX, 0, 1 + (i - NT_CTX) % LAT_TILES), 0)),
            pl.BlockSpec((TM, LANES), lambda i: (jnp.where(i < NT_CTX, 0, 1 + (i - NT_CTX) % LAT_TILES), 0)),
            pl.BlockSpec((2, D_MODEL), lambda i: (0, 0)),
            pl.BlockSpec((1, D_MODEL), lambda i: (0, 0)),
            pl.BlockSpec((1, D_MODEL), lambda i: (0, 0)),
        ],
        out_specs=[
            pl.BlockSpec(head_blk, lambda i: (i, 0, 0, 0)),
            pl.BlockSpec(head_blk, lambda i: (i, 0, 0, 0)),
            pl.BlockSpec(head_blk, lambda i: (i, 0, 0, 0)),
            pl.BlockSpec(head_blk, lambda i: (_ctx_idx(i), 0, 0, 0)),
            pl.BlockSpec(head_blk, lambda i: (_ctx_idx(i), 0, 0, 0)),
            tok_spec, tok_spec, tok_spec, tok_spec,
        ],
        out_shape=[
            jax.ShapeDtypeStruct((NT, N_HEADS, TM, HEAD_W), BF16),
            jax.ShapeDtypeStruct((NT, N_HEADS, TM, HEAD_W), BF16),
            jax.ShapeDtypeStruct((NT, N_HEADS, TM, HEAD_W), BF16),
            jax.ShapeDtypeStruct((NT_CTX, N_HEADS, TM, HEAD_W), F32),
            jax.ShapeDtypeStruct((NT_CTX, N_HEADS, TM, HEAD_W), F32),
            jax.ShapeDtypeStruct((T_ALL, D_MODEL), BF16),
            jax.ShapeDtypeStruct((T_ALL, D_MODEL), BF16),
            jax.ShapeDtypeStruct((T_ALL, D_MODEL), BF16),
            jax.ShapeDtypeStruct((T_ALL, D_MODEL), BF16),
        ],
        compiler_params=_params(("arbitrary",)),
        name="input_projection",
    )(xp, xs, mod3, w_in, cos_t, sin_t, b_gate, ln_g, ln_b)


def _attn_kernel(n_main, has_ctx, lam_ref, q_ref, k_ref, v_ref, *rest):
    if has_ctx:
        ck_ref, cv_ref, sub_ref, o_ref, kall, vt_all = rest
    else:
        sub_ref, o_ref, kall, vt_all = rest
    n_keys = kall.shape[0]

    @pl.when(pl.program_id(2) == 0)
    def _():
        for c in range(n_main):
            kall[c * TM:(c + 1) * TM, :] = k_ref[c, 0]
            vt_all[0:HEAD_W, c * TM:(c + 1) * TM] = v_ref[c, 0].astype(F32).T.astype(BF16)
        if has_ctx:
            kall[n_main * TM:n_keys, :] = ck_ref[0, 0].astype(BF16)
            vt_all[0:HEAD_W, n_main * TM:n_keys] = cv_ref[0, 0].T.astype(BF16)
        r = lax.broadcasted_iota(jnp.int32, (ONES_ROWS, n_keys), 0)
        vt_all[HEAD_W:HEAD_W + ONES_ROWS, :] = jnp.where(r == 0, 1.0, 0.0).astype(BF16)

    q = q_ref[0, 0]
    lane = lax.broadcasted_iota(jnp.int32, (TM, HEAD_W), 1)
    zero = jnp.zeros_like(q)
    qq = jnp.concatenate([jnp.where(lane < HEAD_DIM, q, zero),
                          jnp.where(lane >= HEAD_DIM, q, zero)], axis=0)

    def scores(st, sz):
        return lax.dot_general(kall[st:st + sz, :], qq, (((1,), (1,)), ((), ())),
                               preferred_element_type=F32)

    chunks = [(st, min(KEY_CHUNK, n_keys - st)) for st in range(0, n_keys, KEY_CHUNK)]
    pending = [scores(*ch) for ch in chunks[:SCORE_LEAD]]
    m = acc = None
    for c, (st, sz) in enumerate(chunks):
        s = pending.pop(0)
        if c + SCORE_LEAD < len(chunks):
            pending.append(scores(*chunks[c + SCORE_LEAD]))
        m_c = jnp.max(s, axis=0, keepdims=True)
        m_new = m_c if c == 0 else jnp.maximum(m, m_c)
        e = jnp.exp2(s - m_new).astype(BF16)
        pv = jnp.dot(vt_all[:, st:st + sz], e, preferred_element_type=F32)
        acc = pv if c == 0 else jnp.exp2(m - m_new) * acc + pv
        m = m_new
    l = acc[HEAD_W:HEAD_W + 1, :]
    c1 = 1.0 / l[:, :TM]
    c2 = lam_ref[0] / l[:, TM:]
    a = (acc[:HEAD_W, :TM] * c1 - acc[:HEAD_W, TM:] * c2).T
    ms = jnp.mean(a * a, axis=-1, keepdims=True)
    o_ref[...] = (a * lax.rsqrt(ms + LN_EPS) * sub_ref[...]).astype(BF16)


def _attention(lam, q_all, k_all, v_all, ctx_k, ctx_v, subw, latent):
    if latent:
        n_b, n_q, n_main, base = N_LAT_B, LAT_TILES, LAT_TILES, NT_CTX
        n_keys = LAT_LEN + PAST_LEN
    else:
        n_b, n_q, n_main, base = N_CTX_B, 1, 1, 0
        n_keys = CTX_LEN
    q_spec = pl.BlockSpec((1, 1, TM, HEAD_W), lambda b, h, t: (base + b * n_q + t, h, 0, 0))
    kv_spec = pl.BlockSpec((n_main, 1, TM, HEAD_W), lambda b, h, t: (base // n_main + b, h, 0, 0))
    in_specs = [pl.BlockSpec(memory_space=pltpu.SMEM), q_spec, kv_spec, kv_spec]
    args = [lam, q_all, k_all, v_all]
    if latent:
        c_spec = pl.BlockSpec((1, 1, PAST_LEN, HEAD_W), lambda b, h, t: (b, h, 0, 0))
        in_specs += [c_spec, c_spec]
        args += [ctx_k, ctx_v]
    in_specs.append(pl.BlockSpec((1, HEAD_W), lambda b, h, t: (0, 0)))
    args.append(subw)
    return pl.pallas_call(
        functools.partial(_attn_kernel, n_main, latent),
        grid=(n_b, N_HEADS, n_q),
        in_specs=in_specs,
        out_specs=pl.BlockSpec((TM, HEAD_W), lambda b, h, t: (b * n_q + t, h)),
        out_shape=jax.ShapeDtypeStruct((n_b * n_q * TM, D_MODEL), BF16),
        scratch_shapes=[pltpu.VMEM((n_keys, HEAD_W), BF16),
                        pltpu.VMEM((HEAD_W + ONES_ROWS, n_keys), BF16)],
        compiler_params=_params(("arbitrary", "arbitrary", "arbitrary")),
        name="diff_attention_lat" if latent else "diff_attention_ctx",
    )(*args)


def _mix_kernel(ac_ref, al_ref, ug_ref, vg_ref, sa_ref, sb_ref, xp_ref, xs_ref, mod_ref,
                ws_ref, bs_ref, wpa_ref, wpb_ref, wo_ref, g_ref, b_ref, wrh_ref, wrl_ref, br_ref,
                x1_ref, h2_ref, ti_ref, tg_ref, cnt_ref, run_ref):
    i = pl.program_id(0)
    is_ctx = i < NT_CTX
    a = jnp.where(is_ctx, ac_ref[...], al_ref[...])
    x = jnp.where(is_ctx, xp_ref[...], xs_ref[...])
    mod = mod_ref[0]
    g1 = mod[:, 2 * D_MODEL:3 * D_MODEL]
    sh2 = mod[:, 3 * D_MODEL:4 * D_MODEL]
    sc2 = mod[:, 4 * D_MODEL:5 * D_MODEL]

    gw = D_MODEL // GMLP_GROUPS
    chunks = []
    for c in range(TM // CHUNK):
        groups = []
        for g in range(GMLP_GROUPS):
            vc = vg_ref[c * CHUNK:(c + 1) * CHUNK, g * gw:(g + 1) * gw]
            groups.append(jnp.dot(ws_ref[g], vc, preferred_element_type=F32))
        chunks.append(jnp.concatenate(groups, axis=1) + bs_ref[...])
    sp = jnp.concatenate(chunks, axis=0)
    gm = (ug_ref[...].astype(F32) * sp).astype(BF16)

    pa = jnp.dot(a, wpa_ref[...], preferred_element_type=F32)
    pb = jnp.dot(gm, wpb_ref[...], preferred_element_type=F32)
    merged = (sa_ref[...].astype(F32) * pa + sb_ref[...].astype(F32) * pb).astype(BF16)
    mix = jnp.dot(merged, wo_ref[...], preferred_element_type=F32)
    x1 = _layer_norm(ALPHA * x + g1 * mix, g_ref[...], b_ref[...])
    x1_ref[...] = x1
    h2 = x1 * (1.0 + sc2) + sh2
    hi, lo = _split(h2)
    h2_ref[...] = _pack_pairs(h2)
    logits = (jnp.dot(hi, wrh_ref[...], preferred_element_type=F32)
              + jnp.dot(lo, wrh_ref[...], preferred_element_type=F32)
              + jnp.dot(hi, wrl_ref[...], preferred_element_type=F32)) + br_ref[...]
    lane = lax.broadcasted_iota(jnp.int32, (TM, LANES), 1).astype(F32)
    vals, idxs = [], []
    for _ in range(TOP_K):
        mx = jnp.max(logits, axis=-1, keepdims=True)
        ix = jnp.min(jnp.where(logits == mx, lane, float(LANES)), axis=-1, keepdims=True)
        vals.append(mx)
        idxs.append(ix)
        logits = jnp.where(lane == ix, NEG_BIG * 2.0, logits)
    es = [jnp.exp(v - vals[0]) for v in vals]
    inv = 1.0 / (es[0] + es[1] + es[2] + es[3])

    @pl.when(i == 0)
    def _():
        run_ref[...] = jnp.zeros_like(run_ref)

    hot = [lane == ix for ix in idxs]
    memb = jnp.zeros((TM, LANES), F32)
    for hk in hot:
        memb = jnp.where(hk, 1.0, memb)
    row = lax.broadcasted_iota(jnp.int32, (TM, TM), 0)
    col = lax.broadcasted_iota(jnp.int32, (TM, TM), 1)
    before = jnp.where(row > col, 1.0, 0.0).astype(BF16)
    base = run_ref[...]
    rank_all = jnp.dot(before, memb.astype(BF16), preferred_element_type=F32) + base
    total = base + jnp.sum(memb, axis=0, keepdims=True)
    run_ref[...] = total
    cnt_ref[...] = total

    ti = jnp.zeros((TM, LANES), F32)
    tg = jnp.zeros((TM, LANES), F32)
    for k in range(TOP_K):
        rank_k = jnp.sum(jnp.where(hot[k], rank_all, 0.0), axis=-1, keepdims=True)
        ti = jnp.where(lane == float(k), idxs[k], ti)
        ti = jnp.where(lane == float(TOP_K + k), rank_k, ti)
        tg = jnp.where(lane == float(k), es[k] * inv, tg)
    ti_ref[...] = ti.astype(jnp.int32)
    tg_ref[...] = tg


def _mix(a_ctx, a_lat, ug, vg, sa, sb, xp, xs, mod3, ws, bs, wpa, wpb, wo, g, b, wrh, wrl, br):
    tile = lambda i: (i, 0)
    full2 = lambda i: (0, 0)
    tok = pl.BlockSpec((TM, D_MODEL), tile)
    ctx = pl.BlockSpec((TM, D_MODEL), lambda i: (_ctx_idx(i), 0))
    lat = pl.BlockSpec((TM, D_MODEL), lambda i: (_lat_idx(i), 0))
    wsq = pl.BlockSpec((D_MODEL, D_MODEL), full2)
    vec = pl.BlockSpec((1, D_MODEL), full2)
    return pl.pallas_call(
        _mix_kernel,
        grid=(NT,),
        in_specs=[ctx, lat, tok, tok, tok, tok, ctx, lat,
                  pl.BlockSpec((1, 1, 6 * D_MODEL), lambda i: (_mod_row(i), 0, 0)),
                  pl.BlockSpec((GMLP_GROUPS, CHUNK, CHUNK), lambda i: (0, 0, 0)),
                  pl.BlockSpec((CHUNK, D_MODEL), full2),
                  wsq, wsq, wsq, vec, vec,
                  pl.BlockSpec((D_MODEL, LANES), full2),
                  pl.BlockSpec((D_MODEL, LANES), full2),
                  pl.BlockSpec((1, LANES), full2)],
        out_specs=[tok, pl.BlockSpec((TM, PACK_W), tile),
                   pl.BlockSpec((TM, LANES), tile), pl.BlockSpec((TM, LANES), tile),
                   pl.BlockSpec((1, LANES), full2)],
        out_shape=[jax.ShapeDtypeStruct((T_ALL, D_MODEL), F32),
                   jax.ShapeDtypeStruct((T_ALL, PACK_W), jnp.uint32),
                   jax.ShapeDtypeStruct((T_ALL, LANES), jnp.int32),
                   jax.ShapeDtypeStruct((T_ALL, LANES), F32),
                   jax.ShapeDtypeStruct((1, LANES), F32)],
        scratch_shapes=[pltpu.VMEM((1, LANES), F32)],
        compiler_params=_params(("arbitrary",)),
        name="mix_ln1_router",
    )(a_ctx, a_lat, ug, vg, sa, sb, xp, xs, mod3, ws, bs, wpa, wpb, wo, g, b, wrh, wrl, br)


def _dispatch_kernel(dest_ref, h_ref, init_hbm, out_hbm, sem):
    del init_hbm

    def row_copy(r, k):
        return pltpu.make_async_copy(h_ref.at[pl.ds(r, 1)],
                                     out_hbm.at[pl.ds(dest_ref[r * TOP_K + k], 1)], sem)

    def issue(g, carry):
        for u in range(DMA_UNROLL):
            for k in range(TOP_K):
                row_copy(g * DMA_UNROLL + u, k).start(priority=k % 2)
        return carry

    lax.fori_loop(0, TM // DMA_UNROLL, issue, 0)
    for _ in range(TOP_K):
        pltpu.make_async_copy(h_ref, out_hbm.at[pl.ds(0, TM)], sem).wait()


def _dispatch(dest_flat, h2p, n_rows):
    return pl.pallas_call(
        _dispatch_kernel,
        grid=(NT,),
        in_specs=[pl.BlockSpec((ROWS_PER_TILE,), lambda i: (i,), memory_space=pltpu.SMEM),
                  pl.BlockSpec((TM, PACK_W), lambda i: (i, 0)),
                  pl.BlockSpec(memory_space=pl.ANY)],
        out_specs=pl.BlockSpec(memory_space=pl.ANY),
        out_shape=jax.ShapeDtypeStruct((n_rows, PACK_W), jnp.uint32),
        scratch_shapes=[pltpu.SemaphoreType.DMA(())],
        input_output_aliases={2: 0},
        compiler_params=_params(("arbitrary",)),
        name="moe_dispatch",
    )(dest_flat, h2p, jnp.zeros((n_rows, PACK_W), jnp.uint32))


def _expert_kernel(be_ref, nu_ref, x_ref, wu_ref, bu_ref, wd_ref, bd_ref, y_ref, wu_bf, wd_bf):
    i = pl.program_id(0)
    used = i < nu_ref[0]
    new_expert = jnp.logical_or(i == 0, be_ref[i] != be_ref[jnp.maximum(i - 1, 0)])

    @pl.when(jnp.logical_and(used, new_expert))
    def _():
        wu_bf[...] = wu_ref[0].astype(BF16)
        wd_bf[...] = wd_ref[0].astype(BF16)

    @pl.when(used)
    def _():
        x = _unpack_pairs(x_ref[...]).astype(BF16)
        hu = jnp.dot(x, wu_bf[...], preferred_element_type=F32) + bu_ref[0]
        glu = jnp.minimum(hu[:, :D_EXPERT], SWIGLU_LIMIT)
        lin = jnp.clip(hu[:, D_EXPERT:], -SWIGLU_LIMIT, SWIGLU_LIMIT)
        act = glu * jax.nn.sigmoid(SWIGLU_ALPHA * glu) * (lin + 1.0)
        y = jnp.dot(act.astype(BF16), wd_bf[...], preferred_element_type=F32) + bd_ref[0]
        y_ref[...] = _pack_pairs(y)

    @pl.when(jnp.logical_not(used))
    def _():
        y_ref[...] = jnp.zeros_like(y_ref)


def _experts(blk_exp, n_used, xb, w_up, b_up, w_down, b_down):
    n_blocks = xb.shape[0] // MOE_BLOCK
    grid_spec = pltpu.PrefetchScalarGridSpec(
        num_scalar_prefetch=2,
        grid=(n_blocks,),
        in_specs=[pl.BlockSpec((MOE_BLOCK, PACK_W), lambda i, be, nu: (i, 0)),
                  pl.BlockSpec((1, D_MODEL, 2 * D_EXPERT), lambda i, be, nu: (be[i], 0, 0)),
                  pl.BlockSpec((1, 1, 2 * D_EXPERT), lambda i, be, nu: (be[i], 0, 0)),
                  pl.BlockSpec((1, D_EXPERT, D_MODEL), lambda i, be, nu: (be[i], 0, 0)),
                  pl.BlockSpec((1, 1, D_MODEL), lambda i, be, nu: (be[i], 0, 0))],
        out_specs=pl.BlockSpec((MOE_BLOCK, PACK_W), lambda i, be, nu: (i, 0)),
        scratch_shapes=[pltpu.VMEM((D_MODEL, 2 * D_EXPERT), BF16),
                        pltpu.VMEM((D_EXPERT, D_MODEL), BF16)],
    )
    return pl.pallas_call(
        _expert_kernel,
        grid_spec=grid_spec,
        out_shape=jax.ShapeDtypeStruct((n_blocks * MOE_BLOCK, PACK_W), jnp.uint32),
        compiler_params=_params(("arbitrary",)),
        name="expert_ffn",
    )(blk_exp, n_used, xb, w_up, b_up, w_down, b_down)


def _final_kernel(dcur_ref, dnxt_ref, x1_ref, tg_ref, mod_ref, g_ref, b_ref, yb_hbm,
                  yp_ref, ys_ref, ybuf, sem):
    i = pl.program_id(0)

    def gather(dest_ref, slot):
        def issue(g, carry):
            for u in range(DMA_UNROLL):
                r = g * DMA_UNROLL + u
                for k in range(TOP_K):
                    pltpu.make_async_copy(yb_hbm.at[pl.ds(dest_ref[r * TOP_K + k], 1)],
                                          ybuf.at[slot, pl.ds(k * TM + r, 1)],
                                          sem.at[slot]).start(priority=k % 2)
            return carry
        lax.fori_loop(0, TM // DMA_UNROLL, issue, 0)

    @pl.when(i == 0)
    def _():
        gather(dcur_ref, 0)

    @pl.when(i + 1 < NT)
    def _():
        gather(dnxt_ref, (i + 1) % 2)

    slot = i % 2
    pltpu.make_async_copy(yb_hbm.at[pl.ds(0, ROWS_PER_TILE)], ybuf.at[slot], sem.at[slot]).wait()

    g2 = mod_ref[0][:, 5 * D_MODEL:6 * D_MODEL]
    gates = tg_ref[...]
    ffn = jnp.zeros((TM, D_MODEL), F32)
    for k in range(TOP_K):
        ffn = ffn + gates[:, k:k + 1] * _unpack_pairs(ybuf[slot, k * TM:(k + 1) * TM, :])
    y = _layer_norm(ALPHA * x1_ref[...] + g2 * ffn, g_ref[...], b_ref[...])

    @pl.when(i < NT_CTX)
    def _():
        yp_ref[...] = y

    @pl.when(i >= NT_CTX)
    def _():
        ys_ref[...] = y


def _final(dest_flat, x1, top_g, mod3, g, b, ybp):
    tok = pl.BlockSpec((TM, D_MODEL), lambda i: (i, 0))
    vec = pl.BlockSpec((1, D_MODEL), lambda i: (0, 0))
    return pl.pallas_call(
        _final_kernel,
        grid=(NT,),
        in_specs=[pl.BlockSpec((ROWS_PER_TILE,), lambda i: (i,), memory_space=pltpu.SMEM),
                  pl.BlockSpec((ROWS_PER_TILE,), lambda i: (jnp.minimum(i + 1, NT - 1),),
                               memory_space=pltpu.SMEM),
                  tok, pl.BlockSpec((TM, LANES), lambda i: (i, 0)),
                  pl.BlockSpec((1, 1, 6 * D_MODEL), lambda i: (_mod_row(i), 0, 0)), vec, vec,
                  pl.BlockSpec(memory_space=pl.ANY)],
        scratch_shapes=[pltpu.VMEM((2, ROWS_PER_TILE, PACK_W), jnp.uint32),
                        pltpu.SemaphoreType.DMA((2,))],
        out_specs=[pl.BlockSpec((TM, D_MODEL), lambda i: (_ctx_idx(i), 0)),
                   pl.BlockSpec((TM, D_MODEL), lambda i: (_lat_idx(i), 0))],
        out_shape=[jax.ShapeDtypeStruct((NT_CTX * TM, D_MODEL), F32),
                   jax.ShapeDtypeStruct((NT_LAT * TM, D_MODEL), F32)],
        compiler_params=_params(("arbitrary",)),
        name="residual_ln2",
    )(dest_flat, dest_flat, x1, top_g, mod3, g, b, ybp)


def _rope_tables():
    t = jnp.arange(LAT_LEN)
    r = (t // GRID_W).astype(F32)
    col = (t % GRID_W).astype(F32)
    inv = jnp.power(ROPE_BASE, -jnp.arange(ROPE_HALF, dtype=F32) / ROPE_HALF)
    ang_r = r[:, None] * inv
    ang_c = col[:, None] * inv
    ang = jnp.concatenate([ang_r, ang_r, ang_c, ang_c], axis=-1)
    cos = jnp.tile(jnp.cos(ang), (1, 2))
    sin = jnp.tile(jnp.sin(ang), (1, 2))
    sign = jnp.where((jnp.arange(LANES) % (2 * ROPE_HALF)) < ROPE_HALF, -1.0, 1.0).astype(F32)
    cos = jnp.concatenate([jnp.ones((TM, LANES), F32), cos], axis=0)
    sin = jnp.concatenate([jnp.zeros((TM, LANES), F32), sin * sign], axis=0)
    return cos, sin


def _routing(top_idx, rank, counts):
    n_tok = top_idx.shape[0]
    n_assign = n_tok * TOP_K
    padded = (counts + MOE_BLOCK - 1) // MOE_BLOCK * MOE_BLOCK
    pend = jnp.cumsum(padded)
    pstart = pend - padded
    dest_tk = pstart[top_idx] + rank
    n_blocks = -(-n_assign // MOE_BLOCK) + N_EXPERTS
    blk_start = jnp.arange(n_blocks, dtype=jnp.int32) * MOE_BLOCK
    blk_exp = jnp.minimum(jnp.sum((pend[None, :] <= blk_start[:, None]).astype(jnp.int32), axis=1),
                          N_EXPERTS - 1)
    n_used = (pend[-1] // MOE_BLOCK).astype(jnp.int32).reshape(1)
    return blk_exp, dest_tk.reshape(n_assign), n_used, n_blocks * MOE_BLOCK


def kernel(x_prompt, x_sample, c, cache_k, cache_v, c_ctx, w_ada, b_ada, w_in, lambda_q1, lambda_k1, lambda_q2, lambda_k2, subln_w, gmlp_ln_g, gmlp_ln_b, w_spatial, b_spatial, b_gate, w_pa, w_pb, w_o, ln1_g, ln1_b, w_router, b_router, w_up, b_up, w_down, b_down, ln2_g, ln2_b):
    l = 0
    xp = x_prompt.reshape(N_CTX_B * CTX_LEN, D_MODEL)
    xs = x_sample.reshape(N_LAT_B * LAT_LEN, D_MODEL)

    cond = jnp.concatenate([c_ctx[None, :], c, jnp.zeros((16 - 1 - N_LAT_B, D_MODEL), F32)], axis=0)
    mod3 = _modulation(cond, w_ada[l], b_ada[l][None, :]).reshape(16, 1, 6 * D_MODEL)

    cos_t, sin_t = _rope_tables()
    (q_all, k_all, v_all, new_k, new_v, ug, vg, sa, sb) = _input_projection(
        xp, xs, mod3, w_in[l].astype(BF16), cos_t, sin_t, b_gate[l],
        gmlp_ln_g[l][None, :], gmlp_ln_b[l][None, :])

    lam = (jnp.exp(jnp.sum(lambda_q1[l] * lambda_k1[l])) - jnp.exp(jnp.sum(lambda_q2[l] * lambda_k2[l]))
           + LAM_INIT).reshape(1).astype(F32)
    subw = (subln_w[l] * (1.0 - LAM_INIT))[None, :]
    ctx_k = cache_k[:, l].reshape(N_LAT_B, N_HEADS, PAST_LEN, HEAD_W)
    ctx_v = cache_v[:, l]
    a_ctx = _attention(lam, q_all, k_all, v_all, None, None, subw, latent=False)
    a_lat = _attention(lam, q_all, k_all, v_all, ctx_k, ctx_v, subw, latent=True)

    bs_full = jnp.repeat(b_spatial[l].T, D_MODEL // GMLP_GROUPS, axis=1)
    wr = jnp.pad(w_router[l], ((0, 0), (0, LANES - N_EXPERTS)))
    wr_hi = wr.astype(BF16)
    wr_lo = (wr - wr_hi.astype(F32)).astype(BF16)
    br = jnp.concatenate([b_router[l], jnp.full((LANES - N_EXPERTS,), NEG_BIG, F32)])[None, :]
    x1, h2, top_i, top_g, cnt = _mix(
        a_ctx, a_lat, ug, vg, sa, sb, xp, xs, mod3, w_spatial[l].astype(BF16), bs_full,
        w_pa[l].astype(BF16), w_pb[l].astype(BF16), w_o[l].astype(BF16),
        ln1_g[l][None, :], ln1_b[l][None, :], wr_hi, wr_lo, br)

    blk_exp, dest_flat, n_used, n_rows = _routing(
        top_i[:, :TOP_K], top_i[:, TOP_K:2 * TOP_K], cnt[0, :N_EXPERTS].astype(jnp.int32))
    xb = _dispatch(dest_flat, h2, n_rows)
    yb = _experts(blk_exp, n_used, xb, w_up[l], b_up[l][:, None, :], w_down[l], b_down[l][:, None, :])
    y_prompt, y_sample = _final(dest_flat, x1, top_g, mod3, ln2_g[l][None, :], ln2_b[l][None, :], yb)
    return (y_prompt.reshape(N_CTX_B, CTX_LEN, D_MODEL),
            y_sample.reshape(N_LAT_B, LAT_LEN, D_MODEL),
            new_k.reshape(N_CTX_B, DEPTH, N_HEADS, CTX_LEN, 2, HEAD_DIM),
            new_v.reshape(N_CTX_B, DEPTH, N_HEADS, CTX_LEN, HEAD_W))
```

```python
import functools
import math

import jax
import jax.numpy as jnp
from jax import lax
from jax.experimental import pallas as pl
from jax.experimental.pallas import tpu as pltpu

F32 = jnp.float32
BF16 = jnp.bfloat16

D_MODEL = 1024
N_CTX_B = 16
CTX_LEN = 256
N_LAT_B = 8
LAT_LEN = 4096
PAST_LEN = 256
GRID_W = 64
N_HEADS = 8
HEAD_DIM = 64
HEAD_W = 2 * HEAD_DIM
ROPE_HALF = HEAD_DIM // 4
ROPE_BASE = 10000.0
GMLP_GROUPS = 4
CHUNK = 128
N_EXPERTS = 32
TOP_K = 4
D_EXPERT = 1024
SWIGLU_LIMIT = 7.0
SWIGLU_ALPHA = 1.702
MOE_BLOCK = 256
LN_EPS = 1e-5
DEPTH = 1
N_SEG = 7

TM = 256
NT_CTX = N_CTX_B * CTX_LEN // TM
NT_LAT = N_LAT_B * LAT_LEN // TM
NT = NT_CTX + NT_LAT
T_ALL = NT * TM
LAT_TILES = LAT_LEN // TM
LANES = 128
KEY_CHUNK = 512
SCORE_LEAD = 2
Q_TILES = 2
ONES_ROWS = 16
PACK_W = D_MODEL // 2
ROWS_PER_TILE = TM * TOP_K
DMA_UNROLL = 4
NEG_BIG = -1e30
VMEM_LIMIT = 56 * 1024 * 1024

ALPHA = (2.0 * DEPTH) ** 0.25
LAM_INIT = 0.8 - 0.6 * math.exp(-0.3 * 0)
Q_SCALE = HEAD_DIM ** -0.5 * math.log2(math.e)


def _ctx_idx(i):
    return jnp.minimum(i, NT_CTX - 1)


def _lat_idx(i):
    return jnp.maximum(i - NT_CTX, 0)


def _mod_row(i):
    return jnp.where(i < NT_CTX, 0, 1 + (i - NT_CTX) // LAT_TILES)


def _layer_norm(x, g, b):
    mu = jnp.mean(x, axis=-1, keepdims=True)
    xc = x - mu
    var = jnp.mean(xc * xc, axis=-1, keepdims=True)
    return xc * lax.rsqrt(var + LN_EPS) * g + b


def _gelu(x):
    return 0.5 * x * (1.0 + lax.erf(x * (1.0 / math.sqrt(2.0))))


def _split(x):
    hi = x.astype(BF16)
    return hi, (x - hi.astype(F32)).astype(BF16)


def _pack_pairs(x):
    w = x.shape[1] // 2
    a = pltpu.bitcast(x[:, :w].astype(BF16).astype(F32), jnp.uint32)
    b = pltpu.bitcast(x[:, w:].astype(BF16).astype(F32), jnp.uint32)
    return lax.shift_right_logical(a, jnp.uint32(16)) | b


def _unpack_pairs(p):
    a = pltpu.bitcast(lax.shift_left(p, jnp.uint32(16)), F32)
    b = pltpu.bitcast(p & jnp.uint32(0xFFFF0000), F32)
    return jnp.concatenate([a, b], axis=1)


def _params(sem):
    return pltpu.CompilerParams(dimension_semantics=sem, vmem_limit_bytes=VMEM_LIMIT)


def _mod_kernel(c_ref, w_ref, b_ref, o_ref):
    c = c_ref[...]
    s = c * jax.nn.sigmoid(c)
    s_hi, s_lo = _split(s)
    w_hi, w_lo = _split(w_ref[...])
    o_ref[...] = (jnp.dot(s_hi, w_hi, preferred_element_type=F32)
                  + jnp.dot(s_lo, w_hi, preferred_element_type=F32)
                  + jnp.dot(s_hi, w_lo, preferred_element_type=F32)) + b_ref[...]


def _modulation(cond, w_ada, b_ada):
    n = cond.shape[0]
    return pl.pallas_call(
        _mod_kernel,
        grid=(6,),
        in_specs=[pl.BlockSpec((n, D_MODEL), lambda j: (0, 0)),
                  pl.BlockSpec((D_MODEL, D_MODEL), lambda j: (0, j)),
                  pl.BlockSpec((1, D_MODEL), lambda j: (0, j))],
        out_specs=pl.BlockSpec((n, D_MODEL), lambda j: (0, j)),
        out_shape=jax.ShapeDtypeStruct((n, 6 * D_MODEL), F32),
        compiler_params=_params(("arbitrary",)),
        name="adaln_mod",
    )(cond, w_ada, b_ada)


def _proj_kernel(xp_ref, xs_ref, mod_ref, w_ref, cos_ref, sin_ref, bg_ref, lg_ref, lb_ref,
                 q_ref, k_ref, v_ref, nk_ref, nv_ref, ug_ref, vg_ref, sa_ref, sb_ref):
    i = pl.program_id(0)
    is_ctx = i < NT_CTX
    x = jnp.where(is_ctx, xp_ref[...], xs_ref[...])
    mod = mod_ref[0]
    sh1 = mod[:, 0:D_MODEL]
    sc1 = mod[:, D_MODEL:2 * D_MODEL]
    h = (x * (1.0 + sc1) + sh1).astype(BF16)

    def seg(s):
        return jnp.dot(h, w_ref[:, s * D_MODEL:(s + 1) * D_MODEL], preferred_element_type=F32)

    cos = cos_ref[...]
    sin = sin_ref[...]
    lane = lax.broadcasted_iota(jnp.int32, (TM, LANES), 1)
    first = (lane % (2 * ROPE_HALF)) < ROPE_HALF

    def rope(zh):
        up = pltpu.roll(zh, LANES - ROPE_HALF, 1)
        dn = pltpu.roll(zh, ROPE_HALF, 1)
        return zh * cos + jnp.where(first, up, dn) * sin

    zq = seg(0)
    for hd in range(N_HEADS):
        q_ref[0, hd] = (rope(zq[:, hd * HEAD_W:(hd + 1) * HEAD_W]) * Q_SCALE).astype(BF16)
    zk = seg(1)
    for hd in range(N_HEADS):
        k_ref[0, hd] = rope(zk[:, hd * HEAD_W:(hd + 1) * HEAD_W]).astype(BF16)
    zv = seg(2)
    for hd in range(N_HEADS):
        v_ref[0, hd] = zv[:, hd * HEAD_W:(hd + 1) * HEAD_W].astype(BF16)

    @pl.when(is_ctx)
    def _():
        for hd in range(N_HEADS):
            nk_ref[0, hd] = zk[:, hd * HEAD_W:(hd + 1) * HEAD_W]
            nv_ref[0, hd] = zv[:, hd * HEAD_W:(hd + 1) * HEAD_W]

    ug_ref[...] = _gelu(seg(3)).astype(BF16)
    vg_ref[...] = _layer_norm(_gelu(seg(4)), lg_ref[...], lb_ref[...]).astype(BF16)
    sa_ref[...] = jax.nn.sigmoid(seg(5) + bg_ref[0:1, :]).astype(BF16)
    sb_ref[...] = jax.nn.sigmoid(seg(6) + bg_ref[1:2, :]).astype(BF16)


def _input_projection(xp, xs, mod3, w_in, cos_t, sin_t, b_gate, ln_g, ln_b):
    tile = lambda i: (i, 0)
    head_blk = (1, N_HEADS, TM, HEAD_W)
    tok_spec = pl.BlockSpec((TM, D_MODEL), tile)
    return pl.pallas_call(
        _proj_kernel,
        grid=(NT,),
        in_specs=[
            pl.BlockSpec((TM, D_MODEL), lambda i: (_ctx_idx(i), 0)),
            pl.BlockSpec((TM, D_MODEL), lambda i: (_lat_idx(i), 0)),
            pl.BlockSpec((1, 1, 6 * D_MODEL), lambda i: (_mod_row(i), 0, 0)),
            pl.BlockSpec((D_MODEL, N_SEG * D_MODEL), lambda i: (0, 0)),
            pl.BlockSpec((TM, LANES), lambda i: (jnp.where(i < NT_CTX, 0, 1 + (i - NT_CTX) % LAT_TILES), 0)),
            pl.BlockSpec((TM, LANES), lambda i: (jnp.where(i < NT_CTX, 0, 1 + (i - NT_CTX) % LAT_TILES), 0)),
            pl.BlockSpec((2, D_MODEL), lambda i: (0, 0)),
            pl.BlockSpec((1, D_MODEL), lambda i: (0, 0)),
            pl.BlockSpec((1, D_MODEL), lambda i: (0, 0)),
        ],
        out_specs=[
            pl.BlockSpec(head_blk, lambda i: (i, 0, 0, 0)),
            pl.BlockSpec(head_blk, lambda i: (i, 0, 0, 0)),
            pl.BlockSpec(head_blk, lambda i: (i, 0, 0, 0)),
            pl.BlockSpec(head_blk, lambda i: (_ctx_idx(i), 0, 0, 0)),
            pl.BlockSpec(head_blk, lambda i: (_ctx_idx(i), 0, 0, 0)),
            tok_spec, tok_spec, tok_spec, tok_spec,
        ],
        out_shape=[
            jax.ShapeDtypeStruct((NT, N_HEADS, TM, HEAD_W), BF16),
            jax.ShapeDtypeStruct((NT, N_HEADS, TM, HEAD_W), BF16),
            jax.ShapeDtypeStruct((NT, N_HEADS, TM, HEAD_W), BF16),
            jax.ShapeDtypeStruct((NT_CTX, N_HEADS, TM, HEAD_W), F32),
            jax.ShapeDtypeStruct((NT_CTX, N_HEADS, TM, HEAD_W), F32),
            jax.ShapeDtypeStruct((T_ALL, D_MODEL), BF16),
            jax.ShapeDtypeStruct((T_ALL, D_MODEL), BF16),
            jax.ShapeDtypeStruct((T_ALL, D_MODEL), BF16),
            jax.ShapeDtypeStruct((T_ALL, D_MODEL), BF16),
        ],
        compiler_params=_params(("arbitrary",)),
        name="input_projection",
    )(xp, xs, mod3, w_in, cos_t, sin_t, b_gate, ln_g, ln_b)


def _attn_kernel(n_main, has_ctx, lam_ref, q_ref, k_ref, v_ref, *rest):
    if has_ctx:
        ck_ref, cv_ref, sub_ref, o_ref, kall, vt_all = rest
    else:
        sub_ref, o_ref, kall, vt_all = rest
    n_keys = kall.shape[0]

    @pl.when(pl.program_id(2) == 0)
    def _():
        for c in range(n_main):
            kall[c * TM:(c + 1) * TM, :] = k_ref[c, 0]
            vt_all[0:HEAD_W, c * TM:(c + 1) * TM] = v_ref[c, 0].astype(F32).T.astype(BF16)
        if has_ctx:
            kall[n_main * TM:n_keys, :] = ck_ref[0, 0].astype(BF16)
            vt_all[0:HEAD_W, n_main * TM:n_keys] = cv_ref[0, 0].T.astype(BF16)
        r = lax.broadcasted_iota(jnp.int32, (ONES_ROWS, n_keys), 0)
        vt_all[HEAD_W:HEAD_W + ONES_ROWS, :] = jnp.where(r == 0, 1.0, 0.0).astype(BF16)

    tq = q_ref.shape[0] * TM
    q = q_ref[:, 0].reshape(tq, HEAD_W)
    lane = lax.broadcasted_iota(jnp.int32, (tq, HEAD_W), 1)
    zero = jnp.zeros_like(q)
    qq = jnp.concatenate([jnp.where(lane < HEAD_DIM, q, zero),
                          jnp.where(lane >= HEAD_DIM, q, zero)], axis=0)

    def scores(st, sz):
        return lax.dot_general(kall[st:st + sz, :], qq, (((1,), (1,)), ((), ())),
                               preferred_element_type=F32)

    chunks = [(st, min(KEY_CHUNK, n_keys - st)) for st in range(0, n_keys, KEY_CHUNK)]
    pending = [scores(*ch) for ch in chunks[:SCORE_LEAD]]
    m = acc = None
    for c, (st, sz) in enumerate(chunks):
        s = pending.pop(0)
        if c + SCORE_LEAD < len(chunks):
            pending.append(scores(*chunks[c + SCORE_LEAD]))
        m_c = jnp.max(s, axis=0, keepdims=True)
        m_new = m_c if c == 0 else jnp.maximum(m, m_c)
        e = jnp.exp2(s - m_new).astype(BF16)
        pv = jnp.dot(vt_all[:, st:st + sz], e, preferred_element_type=F32)
        acc = pv if c == 0 else jnp.exp2(m - m_new) * acc + pv
        m = m_new
    l = acc[HEAD_W:HEAD_W + 1, :]
    c1 = 1.0 / l[:, :tq]
    c2 = lam_ref[0] / l[:, tq:]
    a = (acc[:HEAD_W, :tq] * c1 - acc[:HEAD_W, tq:] * c2).T
    ms = jnp.mean(a * a, axis=-1, keepdims=True)
    o_ref[...] = (a * lax.rsqrt(ms + LN_EPS) * sub_ref[...]).astype(BF16)


def _attention(lam, q_all, k_all, v_all, ctx_k, ctx_v, subw, latent):
    if latent:
        n_b, n_q, n_main, base, qt = N_LAT_B, LAT_TILES // Q_TILES, LAT_TILES, NT_CTX, Q_TILES
        n_keys = LAT_LEN + PAST_LEN
    else:
        n_b, n_q, n_main, base, qt = N_CTX_B, 1, 1, 0, 1
        n_keys = CTX_LEN
    q_spec = pl.BlockSpec((qt, 1, TM, HEAD_W), lambda b, h, t: (base // qt + b * n_q + t, h, 0, 0))
    kv_spec = pl.BlockSpec((n_main, 1, TM, HEAD_W), lambda b, h, t: (base // n_main + b, h, 0, 0))
    in_specs = [pl.BlockSpec(memory_space=pltpu.SMEM), q_spec, kv_spec, kv_spec]
    args = [lam, q_all, k_all, v_all]
    if latent:
        c_spec = pl.BlockSpec((1, 1, PAST_LEN, HEAD_W), lambda b, h, t: (b, h, 0, 0))
        in_specs += [c_spec, c_spec]
        args += [ctx_k, ctx_v]
    in_specs.append(pl.BlockSpec((1, HEAD_W), lambda b, h, t: (0, 0)))
    args.append(subw)
    return pl.pallas_call(
        functools.partial(_attn_kernel, n_main, latent),
        grid=(n_b, N_HEADS, n_q),
        in_specs=in_specs,
        out_specs=pl.BlockSpec((qt * TM, HEAD_W), lambda b, h, t: (b * n_q + t, h)),
        out_shape=jax.ShapeDtypeStruct((n_b * n_q * qt * TM, D_MODEL), BF16),
        scratch_shapes=[pltpu.VMEM((n_keys, HEAD_W), BF16),
                        pltpu.VMEM((HEAD_W + ONES_ROWS, n_keys), BF16)],
        compiler_params=_params(("arbitrary", "arbitrary", "arbitrary")),
        name="diff_attention_lat" if latent else "diff_attention_ctx",
    )(*args)


def _mix_kernel(ac_ref, al_ref, ug_ref, vg_ref, sa_ref, sb_ref, xp_ref, xs_ref, mod_ref,
                ws_ref, bs_ref, wpa_ref, wpb_ref, wo_ref, g_ref, b_ref, wrh_ref, wrl_ref, br_ref,
                x1_ref, h2_ref, ti_ref, tg_ref, cnt_ref, run_ref):
    i = pl.program_id(0)
    is_ctx = i < NT_CTX
    a = jnp.where(is_ctx, ac_ref[...], al_ref[...])
    x = jnp.where(is_ctx, xp_ref[...], xs_ref[...])
    mod = mod_ref[0]
    g1 = mod[:, 2 * D_MODEL:3 * D_MODEL]
    sh2 = mod[:, 3 * D_MODEL:4 * D_MODEL]
    sc2 = mod[:, 4 * D_MODEL:5 * D_MODEL]

    gw = D_MODEL // GMLP_GROUPS
    chunks = []
    for c in range(TM // CHUNK):
        groups = []
        for g in range(GMLP_GROUPS):
            vc = vg_ref[c * CHUNK:(c + 1) * CHUNK, g * gw:(g + 1) * gw]
            groups.append(jnp.dot(ws_ref[g], vc, preferred_element_type=F32))
        chunks.append(jnp.concatenate(groups, axis=1) + bs_ref[...])
    sp = jnp.concatenate(chunks, axis=0)
    gm = (ug_ref[...].astype(F32) * sp).astype(BF16)

    pa = jnp.dot(a, wpa_ref[...], preferred_element_type=F32)
    pb = jnp.dot(gm, wpb_ref[...], preferred_element_type=F32)
    merged = (sa_ref[...].astype(F32) * pa + sb_ref[...].astype(F32) * pb).astype(BF16)
    mix = jnp.dot(merged, wo_ref[...], preferred_element_type=F32)
    x1 = _layer_norm(ALPHA * x + g1 * mix, g_ref[...], b_ref[...])
    x1_ref[...] = x1
    h2 = x1 * (1.0 + sc2) + sh2
    hi, lo = _split(h2)
    h2_ref[...] = _pack_pairs(h2)
    logits = (jnp.dot(hi, wrh_ref[...], preferred_element_type=F32)
              + jnp.dot(lo, wrh_ref[...], preferred_element_type=F32)
              + jnp.dot(hi, wrl_ref[...], preferred_element_type=F32)) + br_ref[...]
    lane = lax.broadcasted_iota(jnp.int32, (TM, LANES), 1).astype(F32)
    vals, idxs = [], []
    for _ in range(TOP_K):
        mx = jnp.max(logits, axis=-1, keepdims=True)
        ix = jnp.min(jnp.where(logits == mx, lane, float(LANES)), axis=-1, keepdims=True)
        vals.append(mx)
        idxs.append(ix)
        logits = jnp.where(lane == ix, NEG_BIG * 2.0, logits)
    es = [jnp.exp(v - vals[0]) for v in vals]
    inv = 1.0 / (es[0] + es[1] + es[2] + es[3])

    @pl.when(i == 0)
    def _():
        run_ref[...] = jnp.zeros_like(run_ref)

    hot = [lane == ix for ix in idxs]
    memb = jnp.zeros((TM, LANES), F32)
    for hk in hot:
        memb = jnp.where(hk, 1.0, memb)
    row = lax.broadcasted_iota(jnp.int32, (TM, TM), 0)
    col = lax.broadcasted_iota(jnp.int32, (TM, TM), 1)
    before = jnp.where(row > col, 1.0, 0.0).astype(BF16)
    base = run_ref[...]
    rank_all = jnp.dot(before, memb.astype(BF16), preferred_element_type=F32) + base
    total = base + jnp.sum(memb, axis=0, keepdims=True)
    run_ref[...] = total
    cnt_ref[...] = total

    ti = jnp.zeros((TM, LANES), F32)
    tg = jnp.zeros((TM, LANES), F32)
    for k in range(TOP_K):
        rank_k = jnp.sum(jnp.where(hot[k], rank_all, 0.0), axis=-1, keepdims=True)
        ti = jnp.where(lane == float(k), idxs[k], ti)
        ti = jnp.where(lane == float(TOP_K + k), rank_k, ti)
        tg = jnp.where(lane == float(k), es[k] * inv, tg)
    ti_ref[...] = ti.astype(jnp.int32)
    tg_ref[...] = tg


def _mix(a_ctx, a_lat, ug, vg, sa, sb, xp, xs, mod3, ws, bs, wpa, wpb, wo, g, b, wrh, wrl, br):
    tile = lambda i: (i, 0)
    full2 = lambda i: (0, 0)
    tok = pl.BlockSpec((TM, D_MODEL), tile)
    ctx = pl.BlockSpec((TM, D_MODEL), lambda i: (_ctx_idx(i), 0))
    lat = pl.BlockSpec((TM, D_MODEL), lambda i: (_lat_idx(i), 0))
    wsq = pl.BlockSpec((D_MODEL, D_MODEL), full2)
    vec = pl.BlockSpec((1, D_MODEL), full2)
    return pl.pallas_call(
        _mix_kernel,
        grid=(NT,),
        in_specs=[ctx, lat, tok, tok, tok, tok, ctx, lat,
                  pl.BlockSpec((1, 1, 6 * D_MODEL), lambda i: (_mod_row(i), 0, 0)),
                  pl.BlockSpec((GMLP_GROUPS, CHUNK, CHUNK), lambda i: (0, 0, 0)),
                  pl.BlockSpec((CHUNK, D_MODEL), full2),
                  wsq, wsq, wsq, vec, vec,
                  pl.BlockSpec((D_MODEL, LANES), full2),
                  pl.BlockSpec((D_MODEL, LANES), full2),
                  pl.BlockSpec((1, LANES), full2)],
        out_specs=[tok, pl.BlockSpec((TM, PACK_W), tile),
                   pl.BlockSpec((TM, LANES), tile), pl.BlockSpec((TM, LANES), tile),
                   pl.BlockSpec((1, LANES), full2)],
        out_shape=[jax.ShapeDtypeStruct((T_ALL, D_MODEL), F32),
                   jax.ShapeDtypeStruct((T_ALL, PACK_W), jnp.uint32),
                   jax.ShapeDtypeStruct((T_ALL, LANES), jnp.int32),
                   jax.ShapeDtypeStruct((T_ALL, LANES), F32),
                   jax.ShapeDtypeStruct((1, LANES), F32)],
        scratch_shapes=[pltpu.VMEM((1, LANES), F32)],
        compiler_params=_params(("arbitrary",)),
        name="mix_ln1_router",
    )(a_ctx, a_lat, ug, vg, sa, sb, xp, xs, mod3, ws, bs, wpa, wpb, wo, g, b, wrh, wrl, br)


def _dispatch_kernel(dest_ref, h_ref, init_hbm, out_hbm, sem):
    del init_hbm

    def row_copy(r, k):
        return pltpu.make_async_copy(h_ref.at[pl.ds(r, 1)],
                                     out_hbm.at[pl.ds(dest_ref[r * TOP_K + k], 1)], sem)

    def issue(g, carry):
        for u in range(DMA_UNROLL):
            for k in range(TOP_K):
                row_copy(g * DMA_UNROLL + u, k).start(priority=k % 2)
        return carry

    lax.fori_loop(0, TM // DMA_UNROLL, issue, 0)
    for _ in range(TOP_K):
        pltpu.make_async_copy(h_ref, out_hbm.at[pl.ds(0, TM)], sem).wait()


def _dispatch(dest_flat, h2p, n_rows):
    return pl.pallas_call(
        _dispatch_kernel,
        grid=(NT,),
        in_specs=[pl.BlockSpec((ROWS_PER_TILE,), lambda i: (i,), memory_space=pltpu.SMEM),
                  pl.BlockSpec((TM, PACK_W), lambda i: (i, 0)),
                  pl.BlockSpec(memory_space=pl.ANY)],
        out_specs=pl.BlockSpec(memory_space=pl.ANY),
        out_shape=jax.ShapeDtypeStruct((n_rows, PACK_W), jnp.uint32),
        scratch_shapes=[pltpu.SemaphoreType.DMA(())],
        input_output_aliases={2: 0},
        compiler_params=_params(("arbitrary",)),
        name="moe_dispatch",
    )(dest_flat, h2p, jnp.zeros((n_rows, PACK_W), jnp.uint32))


def _expert_kernel(be_ref, nu_ref, x_ref, wu_ref, bu_ref, wd_ref, bd_ref, y_ref, wu_bf, wd_bf):
    i = pl.program_id(0)
    used = i < nu_ref[0]
    new_expert = jnp.logical_or(i == 0, be_ref[i] != be_ref[jnp.maximum(i - 1, 0)])

    @pl.when(jnp.logical_and(used, new_expert))
    def _():
        wu_bf[...] = wu_ref[0].astype(BF16)
        wd_bf[...] = wd_ref[0].astype(BF16)

    @pl.when(used)
    def _():
        x = _unpack_pairs(x_ref[...]).astype(BF16)
        hu = jnp.dot(x, wu_bf[...], preferred_element_type=F32) + bu_ref[0]
        glu = jnp.minimum(hu[:, :D_EXPERT], SWIGLU_LIMIT)
        lin = jnp.clip(hu[:, D_EXPERT:], -SWIGLU_LIMIT, SWIGLU_LIMIT)
        act = glu * jax.nn.sigmoid(SWIGLU_ALPHA * glu) * (lin + 1.0)
        y = jnp.dot(act.astype(BF16), wd_bf[...], preferred_element_type=F32) + bd_ref[0]
        y_ref[...] = _pack_pairs(y)

    @pl.when(jnp.logical_not(used))
    def _():
        y_ref[...] = jnp.zeros_like(y_ref)


def _experts(blk_exp, n_used, xb, w_up, b_up, w_down, b_down):
    n_blocks = xb.shape[0] // MOE_BLOCK
    grid_spec = pltpu.PrefetchScalarGridSpec(
        num_scalar_prefetch=2,
        grid=(n_blocks,),
        in_specs=[pl.BlockSpec((MOE_BLOCK, PACK_W), lambda i, be, nu: (i, 0)),
                  pl.BlockSpec((1, D_MODEL, 2 * D_EXPERT), lambda i, be, nu: (be[i], 0, 0)),
                  pl.BlockSpec((1, 1, 2 * D_EXPERT), lambda i, be, nu: (be[i], 0, 0)),
                  pl.BlockSpec((1, D_EXPERT, D_MODEL), lambda i, be, nu: (be[i], 0, 0)),
                  pl.BlockSpec((1, 1, D_MODEL), lambda i, be, nu: (be[i], 0, 0))],
        out_specs=pl.BlockSpec((MOE_BLOCK, PACK_W), lambda i, be, nu: (i, 0)),
        scratch_shapes=[pltpu.VMEM((D_MODEL, 2 * D_EXPERT), BF16),
                        pltpu.VMEM((D_EXPERT, D_MODEL), BF16)],
    )
    return pl.pallas_call(
        _expert_kernel,
        grid_spec=grid_spec,
        out_shape=jax.ShapeDtypeStruct((n_blocks * MOE_BLOCK, PACK_W), jnp.uint32),
        compiler_params=_params(("arbitrary",)),
        name="expert_ffn",
    )(blk_exp, n_used, xb, w_up, b_up, w_down, b_down)


def _final_kernel(dcur_ref, dnxt_ref, x1_ref, tg_ref, mod_ref, g_ref, b_ref, yb_hbm,
                  yp_ref, ys_ref, ybuf, sem):
    i = pl.program_id(0)

    def gather(dest_ref, slot):
        def issue(g, carry):
            for u in range(DMA_UNROLL):
                r = g * DMA_UNROLL + u
                for k in range(TOP_K):
                    pltpu.make_async_copy(yb_hbm.at[pl.ds(dest_ref[r * TOP_K + k], 1)],
                                          ybuf.at[slot, pl.ds(k * TM + r, 1)],
                                          sem.at[slot]).start(priority=k % 2)
            return carry
        lax.fori_loop(0, TM // DMA_UNROLL, issue, 0)

    @pl.when(i == 0)
    def _():
        gather(dcur_ref, 0)

    @pl.when(i + 1 < NT)
    def _():
        gather(dnxt_ref, (i + 1) % 2)

    slot = i % 2
    pltpu.make_async_copy(yb_hbm.at[pl.ds(0, ROWS_PER_TILE)], ybuf.at[slot], sem.at[slot]).wait()

    g2 = mod_ref[0][:, 5 * D_MODEL:6 * D_MODEL]
    gates = tg_ref[...]
    ffn = jnp.zeros((TM, D_MODEL), F32)
    for k in range(TOP_K):
        ffn = ffn + gates[:, k:k + 1] * _unpack_pairs(ybuf[slot, k * TM:(k + 1) * TM, :])
    y = _layer_norm(ALPHA * x1_ref[...] + g2 * ffn, g_ref[...], b_ref[...])

    @pl.when(i < NT_CTX)
    def _():
        yp_ref[...] = y

    @pl.when(i >= NT_CTX)
    def _():
        ys_ref[...] = y


def _final(dest_flat, x1, top_g, mod3, g, b, ybp):
    tok = pl.BlockSpec((TM, D_MODEL), lambda i: (i, 0))
    vec = pl.BlockSpec((1, D_MODEL), lambda i: (0, 0))
    return pl.pallas_call(
        _final_kernel,
        grid=(NT,),
        in_specs=[pl.BlockSpec((ROWS_PER_TILE,), lambda i: (i,), memory_space=pltpu.SMEM),
                  pl.BlockSpec((ROWS_PER_TILE,), lambda i: (jnp.minimum(i + 1, NT - 1),),
                               memory_space=pltpu.SMEM),
                  tok, pl.BlockSpec((TM, LANES), lambda i: (i, 0)),
                  pl.BlockSpec((1, 1, 6 * D_MODEL), lambda i: (_mod_row(i), 0, 0)), vec, vec,
                  pl.BlockSpec(memory_space=pl.ANY)],
        scratch_shapes=[pltpu.VMEM((2, ROWS_PER_TILE, PACK_W), jnp.uint32),
                        pltpu.SemaphoreType.DMA((2,))],
        out_specs=[pl.BlockSpec((TM, D_MODEL), lambda i: (_ctx_idx(i), 0)),
                   pl.BlockSpec((TM, D_MODEL), lambda i: (_lat_idx(i), 0))],
        out_shape=[jax.ShapeDtypeStruct((NT_CTX * TM, D_MODEL), F32),
                   jax.ShapeDtypeStruct((NT_LAT * TM, D_MODEL), F32)],
        compiler_params=_params(("arbitrary",)),
        name="residual_ln2",
    )(dest_flat, dest_flat, x1, top_g, mod3, g, b, ybp)


def _rope_tables():
    t = jnp.arange(LAT_LEN)
    r = (t // GRID_W).astype(F32)
    col = (t % GRID_W).astype(F32)
    inv = jnp.power(ROPE_BASE, -jnp.arange(ROPE_HALF, dtype=F32) / ROPE_HALF)
    ang_r = r[:, None] * inv
    ang_c = col[:, None] * inv
    ang = jnp.concatenate([ang_r, ang_r, ang_c, ang_c], axis=-1)
    cos = jnp.tile(jnp.cos(ang), (1, 2))
    sin = jnp.tile(jnp.sin(ang), (1, 2))
    sign = jnp.where((jnp.arange(LANES) % (2 * ROPE_HALF)) < ROPE_HALF, -1.0, 1.0).astype(F32)
    cos = jnp.concatenate([jnp.ones((TM, LANES), F32), cos], axis=0)
    sin = jnp.concatenate([jnp.zeros((TM, LANES), F32), sin * sign], axis=0)
    return cos, sin


def _routing(top_idx, rank, counts):
    n_tok = top_idx.shape[0]
    n_assign = n_tok * TOP_K
    padded = (counts + MOE_BLOCK - 1) // MOE_BLOCK * MOE_BLOCK
    pend = jnp.cumsum(padded)
    pstart = pend - padded
    dest_tk = pstart[top_idx] + rank
    n_blocks = -(-n_assign // MOE_BLOCK) + N_EXPERTS
    blk_start = jnp.arange(n_blocks, dtype=jnp.int32) * MOE_BLOCK
    blk_exp = jnp.minimum(jnp.sum((pend[None, :] <= blk_start[:, None]).astype(jnp.int32), axis=1),
                          N_EXPERTS - 1)
    n_used = (pend[-1] // MOE_BLOCK).astype(jnp.int32).reshape(1)
    return blk_exp, dest_tk.reshape(n_assign), n_used, n_blocks * MOE_BLOCK


def kernel(x_prompt, x_sample, c, cache_k, cache_v, c_ctx, w_ada, b_ada, w_in, lambda_q1, lambda_k1, lambda_q2, lambda_k2, subln_w, gmlp_ln_g, gmlp_ln_b, w_spatial, b_spatial, b_gate, w_pa, w_pb, w_o, ln1_g, ln1_b, w_router, b_router, w_up, b_up, w_down, b_down, ln2_g, ln2_b):
    l = 0
    xp = x_prompt.reshape(N_CTX_B * CTX_LEN, D_MODEL)
    xs = x_sample.reshape(N_LAT_B * LAT_LEN, D_MODEL)

    cond = jnp.concatenate([c_ctx[None, :], c, jnp.zeros((16 - 1 - N_LAT_B, D_MODEL), F32)], axis=0)
    mod3 = _modulation(cond, w_ada[l], b_ada[l][None, :]).reshape(16, 1, 6 * D_MODEL)

    cos_t, sin_t = _rope_tables()
    (q_all, k_all, v_all, new_k, new_v, ug, vg, sa, sb) = _input_projection(
        xp, xs, mod3, w_in[l].astype(BF16), cos_t, sin_t, b_gate[l],
        gmlp_ln_g[l][None, :], gmlp_ln_b[l][None, :])

    lam = (jnp.exp(jnp.sum(lambda_q1[l] * lambda_k1[l])) - jnp.exp(jnp.sum(lambda_q2[l] * lambda_k2[l]))
           + LAM_INIT).reshape(1).astype(F32)
    subw = (subln_w[l] * (1.0 - LAM_INIT))[None, :]
    ctx_k = cache_k[:, l].reshape(N_LAT_B, N_HEADS, PAST_LEN, HEAD_W)
    ctx_v = cache_v[:, l]
    a_ctx = _attention(lam, q_all, k_all, v_all, None, None, subw, latent=False)
    a_lat = _attention(lam, q_all, k_all, v_all, ctx_k, ctx_v, subw, latent=True)

    bs_full = jnp.repeat(b_spatial[l].T, D_MODEL // GMLP_GROUPS, axis=1)
    wr = jnp.pad(w_router[l], ((0, 0), (0, LANES - N_EXPERTS)))
    wr_hi = wr.astype(BF16)
    wr_lo = (wr - wr_hi.astype(F32)).astype(BF16)
    br = jnp.concatenate([b_router[l], jnp.full((LANES - N_EXPERTS,), NEG_BIG, F32)])[None, :]
    x1, h2, top_i, top_g, cnt = _mix(
        a_ctx, a_lat, ug, vg, sa, sb, xp, xs, mod3, w_spatial[l].astype(BF16), bs_full,
        w_pa[l].astype(BF16), w_pb[l].astype(BF16), w_o[l].astype(BF16),
        ln1_g[l][None, :], ln1_b[l][None, :], wr_hi, wr_lo, br)

    blk_exp, dest_flat, n_used, n_rows = _routing(
        top_i[:, :TOP_K], top_i[:, TOP_K:2 * TOP_K], cnt[0, :N_EXPERTS].astype(jnp.int32))
    xb = _dispatch(dest_flat, h2, n_rows)
    yb = _experts(blk_exp, n_used, xb, w_up[l], b_up[l][:, None, :], w_down[l], b_down[l][:, None, :])
    y_prompt, y_sample = _final(dest_flat, x1, top_g, mod3, ln2_g[l][None, :], ln2_b[l][None, :], yb)
    return (y_prompt.reshape(N_CTX_B, CTX_LEN, D_MODEL),
            y_sample.reshape(N_LAT_B, LAT_LEN, D_MODEL),
            new_k.reshape(N_CTX_B, DEPTH, N_HEADS, CTX_LEN, 2, HEAD_DIM),
            new_v.reshape(N_CTX_B, DEPTH, N_HEADS, CTX_LEN, HEAD_W))
```

```python
import functools
import math

import jax
import jax.numpy as jnp
from jax import lax
from jax.experimental import pallas as pl
from jax.experimental.pallas import tpu as pltpu

F32 = jnp.float32
BF16 = jnp.bfloat16

D_MODEL = 1024
N_CTX_B = 16
CTX_LEN = 256
N_LAT_B = 8
LAT_LEN = 4096
PAST_LEN = 256
GRID_W = 64
N_HEADS = 8
HEAD_DIM = 64
HEAD_W = 2 * HEAD_DIM
ROPE_HALF = HEAD_DIM // 4
ROPE_BASE = 10000.0
GMLP_GROUPS = 4
CHUNK = 128
N_EXPERTS = 32
TOP_K = 4
D_EXPERT = 1024
SWIGLU_LIMIT = 7.0
SWIGLU_ALPHA = 1.702
MOE_BLOCK = 256
LN_EPS = 1e-5
DEPTH = 1
N_SEG = 7

TM = 256
NT_CTX = N_CTX_B * CTX_LEN // TM
NT_LAT = N_LAT_B * LAT_LEN // TM
NT = NT_CTX + NT_LAT
T_ALL = NT * TM
LAT_TILES = LAT_LEN // TM
LANES = 128
KEY_CHUNK = 512
SCORE_LEAD = 2
Q_TILES = 2
HEADS_PER_STEP = 2
ONES_ROWS = 16
PACK_W = D_MODEL // 2
ROWS_PER_TILE = TM * TOP_K
SUBLANES = 8
NEG_BIG = -1e30
VMEM_LIMIT = 56 * 1024 * 1024

ALPHA = (2.0 * DEPTH) ** 0.25
LAM_INIT = 0.8 - 0.6 * math.exp(-0.3 * 0)
Q_SCALE = HEAD_DIM ** -0.5 * math.log2(math.e)


def _ctx_idx(i):
    return jnp.minimum(i, NT_CTX - 1)


def _lat_idx(i):
    return jnp.maximum(i - NT_CTX, 0)


def _mod_row(i):
    return jnp.where(i < NT_CTX, 0, 1 + (i - NT_CTX) // LAT_TILES)


def _layer_norm(x, g, b):
    mu = jnp.mean(x, axis=-1, keepdims=True)
    xc = x - mu
    var = jnp.mean(xc * xc, axis=-1, keepdims=True)
    return xc * lax.rsqrt(var + LN_EPS) * g + b


def _gelu(x):
    return 0.5 * x * (1.0 + lax.erf(x * (1.0 / math.sqrt(2.0))))


def _split(x):
    hi = x.astype(BF16)
    return hi, (x - hi.astype(F32)).astype(BF16)


def _pack_pairs(x):
    w = x.shape[1] // 2
    a = pltpu.bitcast(x[:, :w].astype(BF16).astype(F32), jnp.uint32)
    b = pltpu.bitcast(x[:, w:].astype(BF16).astype(F32), jnp.uint32)
    return lax.shift_right_logical(a, jnp.uint32(16)) | b


def _unpack_pairs(p):
    a = pltpu.bitcast(lax.shift_left(p, jnp.uint32(16)), F32)
    b = pltpu.bitcast(p & jnp.uint32(0xFFFF0000), F32)
    return jnp.concatenate([a, b], axis=1)


def _params(sem):
    return pltpu.CompilerParams(dimension_semantics=sem, vmem_limit_bytes=VMEM_LIMIT)


def _mod_kernel(c_ref, w_ref, b_ref, o_ref):
    c = c_ref[...]
    s = c * jax.nn.sigmoid(c)
    s_hi, s_lo = _split(s)
    w_hi, w_lo = _split(w_ref[...])
    o_ref[...] = (jnp.dot(s_hi, w_hi, preferred_element_type=F32)
                  + jnp.dot(s_lo, w_hi, preferred_element_type=F32)
                  + jnp.dot(s_hi, w_lo, preferred_element_type=F32)) + b_ref[...]


def _modulation(cond, w_ada, b_ada):
    n = cond.shape[0]
    return pl.pallas_call(
        _mod_kernel,
        grid=(6,),
        in_specs=[pl.BlockSpec((n, D_MODEL), lambda j: (0, 0)),
                  pl.BlockSpec((D_MODEL, D_MODEL), lambda j: (0, j)),
                  pl.BlockSpec((1, D_MODEL), lambda j: (0, j))],
        out_specs=pl.BlockSpec((n, D_MODEL), lambda j: (0, j)),
        out_shape=jax.ShapeDtypeStruct((n, 6 * D_MODEL), F32),
        compiler_params=_params(("arbitrary",)),
        name="adaln_mod",
    )(cond, w_ada, b_ada)


def _proj_kernel(xp_ref, xs_ref, mod_ref, w_ref, cos_ref, sin_ref, bg_ref, lg_ref, lb_ref,
                 q_ref, k_ref, v_ref, nk_ref, nv_ref, ug_ref, vg_ref, sa_ref, sb_ref):
    i = pl.program_id(0)
    is_ctx = i < NT_CTX
    x = jnp.where(is_ctx, xp_ref[...], xs_ref[...])
    mod = mod_ref[0]
    sh1 = mod[:, 0:D_MODEL]
    sc1 = mod[:, D_MODEL:2 * D_MODEL]
    h = (x * (1.0 + sc1) + sh1).astype(BF16)

    def seg(s):
        return jnp.dot(h, w_ref[:, s * D_MODEL:(s + 1) * D_MODEL], preferred_element_type=F32)

    cos = cos_ref[...]
    sin = sin_ref[...]
    lane = lax.broadcasted_iota(jnp.int32, (TM, LANES), 1)
    first = (lane % (2 * ROPE_HALF)) < ROPE_HALF

    def rope(zh):
        up = pltpu.roll(zh, LANES - ROPE_HALF, 1)
        dn = pltpu.roll(zh, ROPE_HALF, 1)
        return zh * cos + jnp.where(first, up, dn) * sin

    zq = seg(0)
    for hd in range(N_HEADS):
        q_ref[0, hd] = (rope(zq[:, hd * HEAD_W:(hd + 1) * HEAD_W]) * Q_SCALE).astype(BF16)
    zk = seg(1)
    for hd in range(N_HEADS):
        k_ref[0, hd] = rope(zk[:, hd * HEAD_W:(hd + 1) * HEAD_W]).astype(BF16)
    zv = seg(2)
    for hd in range(N_HEADS):
        v_ref[0, hd] = zv[:, hd * HEAD_W:(hd + 1) * HEAD_W].astype(BF16)

    @pl.when(is_ctx)
    def _():
        for hd in range(N_HEADS):
            nk_ref[0, hd] = zk[:, hd * HEAD_W:(hd + 1) * HEAD_W]
            nv_ref[0, hd] = zv[:, hd * HEAD_W:(hd + 1) * HEAD_W]

    ug_ref[...] = _gelu(seg(3)).astype(BF16)
    vg_ref[...] = _layer_norm(_gelu(seg(4)), lg_ref[...], lb_ref[...]).astype(BF16)
    sa_ref[...] = jax.nn.sigmoid(seg(5) + bg_ref[0:1, :]).astype(BF16)
    sb_ref[...] = jax.nn.sigmoid(seg(6) + bg_ref[1:2, :]).astype(BF16)


def _input_projection(xp, xs, mod3, w_in, cos_t, sin_t, b_gate, ln_g, ln_b):
    tile = lambda i: (i, 0)
    head_blk = (1, N_HEADS, TM, HEAD_W)
    tok_spec = pl.BlockSpec((TM, D_MODEL), tile)
    return pl.pallas_call(
        _proj_kernel,
        grid=(NT,),
        in_specs=[
            pl.BlockSpec((TM, D_MODEL), lambda i: (_ctx_idx(i), 0)),
            pl.BlockSpec((TM, D_MODEL), lambda i: (_lat_idx(i), 0)),
            pl.BlockSpec((1, 1, 6 * D_MODEL), lambda i: (_mod_row(i), 0, 0)),
            pl.BlockSpec((D_MODEL, N_SEG * D_MODEL), lambda i: (0, 0)),
            pl.BlockSpec((TM, LANES), lambda i: (jnp.where(i < NT_CTX, 0, 1 + (i - NT_CTX) % LAT_TILES), 0)),
            pl.BlockSpec((TM, LANES), lambda i: (jnp.where(i < NT_CTX, 0, 1 + (i - NT_CTX) % LAT_TILES), 0)),
            pl.BlockSpec((2, D_MODEL), lambda i: (0, 0)),
            pl.BlockSpec((1, D_MODEL), lambda i: (0, 0)),
            pl.BlockSpec((1, D_MODEL), lambda i: (0, 0)),
        ],
        out_specs=[
            pl.BlockSpec(head_blk, lambda i: (i, 0, 0, 0)),
            pl.BlockSpec(head_blk, lambda i: (i, 0, 0, 0)),
            pl.BlockSpec(head_blk, lambda i: (i, 0, 0, 0)),
            pl.BlockSpec(head_blk, lambda i: (_ctx_idx(i), 0, 0, 0)),
            pl.BlockSpec(head_blk, lambda i: (_ctx_idx(i), 0, 0, 0)),
            tok_spec, tok_spec, tok_spec, tok_spec,
        ],
        out_shape=[
            jax.ShapeDtypeStruct((NT, N_HEADS, TM, HEAD_W), BF16),
            jax.ShapeDtypeStruct((NT, N_HEADS, TM, HEAD_W), BF16),
            jax.ShapeDtypeStruct((NT, N_HEADS, TM, HEAD_W), BF16),
            jax.ShapeDtypeStruct((NT_CTX, N_HEADS, TM, HEAD_W), F32),
            jax.ShapeDtypeStruct((NT_CTX, N_HEADS, TM, HEAD_W), F32),
            jax.ShapeDtypeStruct((T_ALL, D_MODEL), BF16),
            jax.ShapeDtypeStruct((T_ALL, D_MODEL), BF16),
            jax.ShapeDtypeStruct((T_ALL, D_MODEL), BF16),
            jax.ShapeDtypeStruct((T_ALL, D_MODEL), BF16),
        ],
        compiler_params=_params(("arbitrary",)),
        name="input_projection",
    )(xp, xs, mod3, w_in, cos_t, sin_t, b_gate, ln_g, ln_b)


def _attn_kernel(n_main, has_ctx, lam_ref, q_ref, k_ref, v_ref, *rest):
    if has_ctx:
        ck_ref, cv_ref, sub_ref, o_ref, kall, vt_all = rest
    else:
        sub_ref, o_ref, kall, vt_all = rest
    n_heads, n_keys = kall.shape[0], kall.shape[1]

    @pl.when(pl.program_id(2) == 0)
    def _():
        r = lax.broadcasted_iota(jnp.int32, (ONES_ROWS, n_keys), 0)
        for j in range(n_heads):
            for c in range(n_main):
                kall[j, c * TM:(c + 1) * TM, :] = k_ref[c, j]
                vt_all[j, 0:HEAD_W, c * TM:(c + 1) * TM] = v_ref[c, j].astype(F32).T.astype(BF16)
            if has_ctx:
                kall[j, n_main * TM:n_keys, :] = ck_ref[0, j].astype(BF16)
                vt_all[j, 0:HEAD_W, n_main * TM:n_keys] = cv_ref[0, j].T.astype(BF16)
            vt_all[j, HEAD_W:HEAD_W + ONES_ROWS, :] = jnp.where(r == 0, 1.0, 0.0).astype(BF16)

    tq = q_ref.shape[0] * TM
    lane = lax.broadcasted_iota(jnp.int32, (tq, HEAD_W), 1)
    qqs = []
    for j in range(n_heads):
        q = q_ref[:, j].reshape(tq, HEAD_W)
        zero = jnp.zeros_like(q)
        qqs.append(jnp.concatenate([jnp.where(lane < HEAD_DIM, q, zero),
                                    jnp.where(lane >= HEAD_DIM, q, zero)], axis=0))

    def scores(j, st, sz):
        return lax.dot_general(kall[j, st:st + sz, :], qqs[j], (((1,), (1,)), ((), ())),
                               preferred_element_type=F32)

    chunks = [(st, min(KEY_CHUNK, n_keys - st)) for st in range(0, n_keys, KEY_CHUNK)]
    pending = [[scores(j, *ch) for ch in chunks[:SCORE_LEAD]] for j in range(n_heads)]
    m = [None] * n_heads
    acc = [None] * n_heads
    for c, (st, sz) in enumerate(chunks):
        for j in range(n_heads):
            s = pending[j].pop(0)
            if c + SCORE_LEAD < len(chunks):
                pending[j].append(scores(j, *chunks[c + SCORE_LEAD]))
            m_c = jnp.max(s, axis=0, keepdims=True)
            m_new = m_c if c == 0 else jnp.maximum(m[j], m_c)
            e = jnp.exp2(s - m_new).astype(BF16)
            pv = jnp.dot(vt_all[j, :, st:st + sz], e, preferred_element_type=F32)
            acc[j] = pv if c == 0 else jnp.exp2(m[j] - m_new) * acc[j] + pv
            m[j] = m_new
    for j in range(n_heads):
        l = acc[j][HEAD_W:HEAD_W + 1, :]
        c1 = 1.0 / l[:, :tq]
        c2 = lam_ref[0] / l[:, tq:]
        a = (acc[j][:HEAD_W, :tq] * c1 - acc[j][:HEAD_W, tq:] * c2).T
        ms = jnp.mean(a * a, axis=-1, keepdims=True)
        o_ref[:, j * HEAD_W:(j + 1) * HEAD_W] = (a * lax.rsqrt(ms + LN_EPS) * sub_ref[...]).astype(BF16)


def _attention(lam, q_all, k_all, v_all, ctx_k, ctx_v, subw, latent):
    if latent:
        n_b, n_q, n_main, base, qt = N_LAT_B, LAT_TILES // Q_TILES, LAT_TILES, NT_CTX, Q_TILES
        n_keys, hp = LAT_LEN + PAST_LEN, HEADS_PER_STEP
    else:
        n_b, n_q, n_main, base, qt = N_CTX_B, 1, 1, 0, 1
        n_keys, hp = CTX_LEN, N_HEADS
    q_spec = pl.BlockSpec((qt, hp, TM, HEAD_W), lambda b, h, t: (base // qt + b * n_q + t, h, 0, 0))
    kv_spec = pl.BlockSpec((n_main, hp, TM, HEAD_W), lambda b, h, t: (base // n_main + b, h, 0, 0))
    in_specs = [pl.BlockSpec(memory_space=pltpu.SMEM), q_spec, kv_spec, kv_spec]
    args = [lam, q_all, k_all, v_all]
    if latent:
        c_spec = pl.BlockSpec((1, hp, PAST_LEN, HEAD_W), lambda b, h, t: (b, h, 0, 0))
        in_specs += [c_spec, c_spec]
        args += [ctx_k, ctx_v]
    in_specs.append(pl.BlockSpec((1, HEAD_W), lambda b, h, t: (0, 0)))
    args.append(subw)
    return pl.pallas_call(
        functools.partial(_attn_kernel, n_main, latent),
        grid=(n_b, N_HEADS // hp, n_q),
        in_specs=in_specs,
        out_specs=pl.BlockSpec((qt * TM, hp * HEAD_W), lambda b, h, t: (b * n_q + t, h)),
        out_shape=jax.ShapeDtypeStruct((n_b * n_q * qt * TM, D_MODEL), BF16),
        scratch_shapes=[pltpu.VMEM((hp, n_keys, HEAD_W), BF16),
                        pltpu.VMEM((hp, HEAD_W + ONES_ROWS, n_keys), BF16)],
        compiler_params=_params(("arbitrary", "arbitrary", "arbitrary")),
        name="diff_attention_lat" if latent else "diff_attention_ctx",
    )(*args)


def _mix_kernel(ac_ref, al_ref, ug_ref, vg_ref, sa_ref, sb_ref, xp_ref, xs_ref, mod_ref,
                ws_ref, bs_ref, wpa_ref, wpb_ref, wo_ref, g_ref, b_ref, wrh_ref, wrl_ref, br_ref,
                x1_ref, h2_ref, ti_ref, tg_ref, cnt_ref, run_ref):
    i = pl.program_id(0)
    is_ctx = i < NT_CTX
    a = jnp.where(is_ctx, ac_ref[...], al_ref[...])
    x = jnp.where(is_ctx, xp_ref[...], xs_ref[...])
    mod = mod_ref[0]
    g1 = mod[:, 2 * D_MODEL:3 * D_MODEL]
    sh2 = mod[:, 3 * D_MODEL:4 * D_MODEL]
    sc2 = mod[:, 4 * D_MODEL:5 * D_MODEL]

    gw = D_MODEL // GMLP_GROUPS
    chunks = []
    for c in range(TM // CHUNK):
        groups = []
        for g in range(GMLP_GROUPS):
            vc = vg_ref[c * CHUNK:(c + 1) * CHUNK, g * gw:(g + 1) * gw]
            groups.append(jnp.dot(ws_ref[g], vc, preferred_element_type=F32))
        chunks.append(jnp.concatenate(groups, axis=1) + bs_ref[...])
    sp = jnp.concatenate(chunks, axis=0)
    gm = (ug_ref[...].astype(F32) * sp).astype(BF16)

    pa = jnp.dot(a, wpa_ref[...], preferred_element_type=F32)
    pb = jnp.dot(gm, wpb_ref[...], preferred_element_type=F32)
    merged = (sa_ref[...].astype(F32) * pa + sb_ref[...].astype(F32) * pb).astype(BF16)
    mix = jnp.dot(merged, wo_ref[...], preferred_element_type=F32)
    x1 = _layer_norm(ALPHA * x + g1 * mix, g_ref[...], b_ref[...])
    x1_ref[...] = x1
    h2 = x1 * (1.0 + sc2) + sh2
    hi, lo = _split(h2)
    h2_ref[...] = _pack_pairs(h2)
    logits = (jnp.dot(hi, wrh_ref[...], preferred_element_type=F32)
              + jnp.dot(lo, wrh_ref[...], preferred_element_type=F32)
              + jnp.dot(hi, wrl_ref[...], preferred_element_type=F32)) + br_ref[...]
    lane = lax.broadcasted_iota(jnp.int32, (TM, LANES), 1).astype(F32)
    vals, idxs = [], []
    for _ in range(TOP_K):
        mx = jnp.max(logits, axis=-1, keepdims=True)
        ix = jnp.min(jnp.where(logits == mx, lane, float(LANES)), axis=-1, keepdims=True)
        vals.append(mx)
        idxs.append(ix)
        logits = jnp.where(lane == ix, NEG_BIG * 2.0, logits)
    es = [jnp.exp(v - vals[0]) for v in vals]
    inv = 1.0 / (es[0] + es[1] + es[2] + es[3])

    @pl.when(i == 0)
    def _():
        run_ref[...] = jnp.zeros_like(run_ref)

    hot = [lane == ix for ix in idxs]
    memb = jnp.zeros((TM, LANES), F32)
    for hk in hot:
        memb = jnp.where(hk, 1.0, memb)
    row = lax.broadcasted_iota(jnp.int32, (TM, TM), 0)
    col = lax.broadcasted_iota(jnp.int32, (TM, TM), 1)
    before = jnp.where(row > col, 1.0, 0.0).astype(BF16)
    base = run_ref[...]
    rank_all = jnp.dot(before, memb.astype(BF16), preferred_element_type=F32) + base
    total = base + jnp.sum(memb, axis=0, keepdims=True)
    run_ref[...] = total
    cnt_ref[...] = total

    ti = jnp.zeros((TM, LANES), F32)
    tg = jnp.zeros((TM, LANES), F32)
    for k in range(TOP_K):
        rank_k = jnp.sum(jnp.where(hot[k], rank_all, 0.0), axis=-1, keepdims=True)
        ti = jnp.where(lane == float(k), idxs[k], ti)
        ti = jnp.where(lane == float(TOP_K + k), rank_k, ti)
        tg = jnp.where(lane == float(k), es[k] * inv, tg)
    ti_ref[...] = ti.astype(jnp.int32)
    tg_ref[...] = tg


def _mix(a_ctx, a_lat, ug, vg, sa, sb, xp, xs, mod3, ws, bs, wpa, wpb, wo, g, b, wrh, wrl, br):
    tile = lambda i: (i, 0)
    full2 = lambda i: (0, 0)
    tok = pl.BlockSpec((TM, D_MODEL), tile)
    ctx = pl.BlockSpec((TM, D_MODEL), lambda i: (_ctx_idx(i), 0))
    lat = pl.BlockSpec((TM, D_MODEL), lambda i: (_lat_idx(i), 0))
    wsq = pl.BlockSpec((D_MODEL, D_MODEL), full2)
    vec = pl.BlockSpec((1, D_MODEL), full2)
    return pl.pallas_call(
        _mix_kernel,
        grid=(NT,),
        in_specs=[ctx, lat, tok, tok, tok, tok, ctx, lat,
                  pl.BlockSpec((1, 1, 6 * D_MODEL), lambda i: (_mod_row(i), 0, 0)),
                  pl.BlockSpec((GMLP_GROUPS, CHUNK, CHUNK), lambda i: (0, 0, 0)),
                  pl.BlockSpec((CHUNK, D_MODEL), full2),
                  wsq, wsq, wsq, vec, vec,
                  pl.BlockSpec((D_MODEL, LANES), full2),
                  pl.BlockSpec((D_MODEL, LANES), full2),
                  pl.BlockSpec((1, LANES), full2)],
        out_specs=[tok, pl.BlockSpec((TM, PACK_W), tile),
                   pl.BlockSpec((TM, LANES), tile), pl.BlockSpec((TM, LANES), tile),
                   pl.BlockSpec((1, LANES), full2)],
        out_shape=[jax.ShapeDtypeStruct((T_ALL, D_MODEL), F32),
                   jax.ShapeDtypeStruct((T_ALL, PACK_W), jnp.uint32),
                   jax.ShapeDtypeStruct((T_ALL, LANES), jnp.int32),
                   jax.ShapeDtypeStruct((T_ALL, LANES), F32),
                   jax.ShapeDtypeStruct((1, LANES), F32)],
        scratch_shapes=[pltpu.VMEM((1, LANES), F32)],
        compiler_params=_params(("arbitrary",)),
        name="mix_ln1_router",
    )(a_ctx, a_lat, ug, vg, sa, sb, xp, xs, mod3, ws, bs, wpa, wpb, wo, g, b, wrh, wrl, br)


def _dispatch_kernel(dest_ref, h_ref, init_hbm, out_hbm, sem):
    del init_hbm

    def issue(g, carry):
        base = g * (SUBLANES * TOP_K)
        for u in range(SUBLANES):
            for k in range(TOP_K):
                dest = dest_ref[base + (u * TOP_K + k)]
                pltpu.make_async_copy(h_ref.at[g, pl.ds(u, 1)], out_hbm.at[pl.ds(dest, 1)],
                                      sem).start(priority=k % 2)
        return carry

    lax.fori_loop(0, TM // SUBLANES, issue, 0)
    for _ in range(TOP_K):
        pltpu.make_async_copy(out_hbm.at[pl.ds(0, TM)], out_hbm.at[pl.ds(0, TM)], sem).wait()


def _dispatch(dest_flat, h2p, n_rows):
    return pl.pallas_call(
        _dispatch_kernel,
        grid=(NT,),
        in_specs=[pl.BlockSpec((ROWS_PER_TILE,), lambda i: (i,), memory_space=pltpu.SMEM),
                  pl.BlockSpec((TM // SUBLANES, SUBLANES, PACK_W), lambda i: (i, 0, 0)),
                  pl.BlockSpec(memory_space=pl.ANY)],
        out_specs=pl.BlockSpec(memory_space=pl.ANY),
        out_shape=jax.ShapeDtypeStruct((n_rows, PACK_W), jnp.uint32),
        scratch_shapes=[pltpu.SemaphoreType.DMA(())],
        input_output_aliases={2: 0},
        compiler_params=_params(("arbitrary",)),
        name="moe_dispatch",
    )(dest_flat, h2p.reshape(T_ALL // SUBLANES, SUBLANES, PACK_W),
      jnp.zeros((n_rows, PACK_W), jnp.uint32))


def _expert_kernel(be_ref, nu_ref, x_ref, wu_ref, bu_ref, wd_ref, bd_ref, y_ref, wu_bf, wd_bf):
    i = pl.program_id(0)
    used = i < nu_ref[0]
    new_expert = jnp.logical_or(i == 0, be_ref[i] != be_ref[jnp.maximum(i - 1, 0)])

    @pl.when(jnp.logical_and(used, new_expert))
    def _():
        wu_bf[...] = wu_ref[0].astype(BF16)
        wd_bf[...] = wd_ref[0].astype(BF16)

    @pl.when(used)
    def _():
        x = _unpack_pairs(x_ref[...]).astype(BF16)
        hu = jnp.dot(x, wu_bf[...], preferred_element_type=F32) + bu_ref[0]
        glu = jnp.minimum(hu[:, :D_EXPERT], SWIGLU_LIMIT)
        lin = jnp.clip(hu[:, D_EXPERT:], -SWIGLU_LIMIT, SWIGLU_LIMIT)
        act = glu * jax.nn.sigmoid(SWIGLU_ALPHA * glu) * (lin + 1.0)
        y = jnp.dot(act.astype(BF16), wd_bf[...], preferred_element_type=F32) + bd_ref[0]
        y_ref[...] = _pack_pairs(y)

    @pl.when(jnp.logical_not(used))
    def _():
        y_ref[...] = jnp.zeros_like(y_ref)


def _experts(blk_exp, n_used, xb, w_up, b_up, w_down, b_down):
    n_blocks = xb.shape[0] // MOE_BLOCK
    grid_spec = pltpu.PrefetchScalarGridSpec(
        num_scalar_prefetch=2,
        grid=(n_blocks,),
        in_specs=[pl.BlockSpec((MOE_BLOCK, PACK_W), lambda i, be, nu: (i, 0)),
                  pl.BlockSpec((1, D_MODEL, 2 * D_EXPERT), lambda i, be, nu: (be[i], 0, 0)),
                  pl.BlockSpec((1, 1, 2 * D_EXPERT), lambda i, be, nu: (be[i], 0, 0)),
                  pl.BlockSpec((1, D_EXPERT, D_MODEL), lambda i, be, nu: (be[i], 0, 0)),
                  pl.BlockSpec((1, 1, D_MODEL), lambda i, be, nu: (be[i], 0, 0))],
        out_specs=pl.BlockSpec((MOE_BLOCK, PACK_W), lambda i, be, nu: (i, 0)),
        scratch_shapes=[pltpu.VMEM((D_MODEL, 2 * D_EXPERT), BF16),
                        pltpu.VMEM((D_EXPERT, D_MODEL), BF16)],
    )
    return pl.pallas_call(
        _expert_kernel,
        grid_spec=grid_spec,
        out_shape=jax.ShapeDtypeStruct((n_blocks * MOE_BLOCK, PACK_W), jnp.uint32),
        compiler_params=_params(("arbitrary",)),
        name="expert_ffn",
    )(blk_exp, n_used, xb, w_up, b_up, w_down, b_down)


def _final_kernel(dcur_ref, dnxt_ref, x1_ref, tg_ref, mod_ref, g_ref, b_ref, yb_hbm,
                  yp_ref, ys_ref, ybuf, sem):
    i = pl.program_id(0)

    def gather(dest_ref, slot):
        def issue(g, carry):
            base = g * (SUBLANES * TOP_K)
            for u in range(SUBLANES):
                for k in range(TOP_K):
                    dest = dest_ref[base + (u * TOP_K + k)]
                    pltpu.make_async_copy(yb_hbm.at[pl.ds(dest, 1)],
                                          ybuf.at[slot, k, g, pl.ds(u, 1)],
                                          sem.at[slot]).start(priority=k % 2)
            return carry
        lax.fori_loop(0, TM // SUBLANES, issue, 0)

    def wait_rows(slot):
        for k in range(TOP_K):
            pltpu.make_async_copy(yb_hbm.at[pl.ds(0, TM)], yb_hbm.at[pl.ds(0, TM)],
                                  sem.at[slot]).wait()

    def combine(slot):
        g2 = mod_ref[0][:, 5 * D_MODEL:6 * D_MODEL]
        gates = tg_ref[...]
        ffn = jnp.zeros((TM, D_MODEL), F32)
        for k in range(TOP_K):
            yk = ybuf[slot, k].reshape(TM, PACK_W)
            ffn = ffn + gates[:, k:k + 1] * _unpack_pairs(yk)
        y = _layer_norm(ALPHA * x1_ref[...] + g2 * ffn, g_ref[...], b_ref[...])

        @pl.when(i < NT_CTX)
        def _():
            yp_ref[...] = y

        @pl.when(i >= NT_CTX)
        def _():
            ys_ref[...] = y

    @pl.when(i == 0)
    def _():
        gather(dcur_ref, 0)

    for slot in range(2):
        @pl.when(i % 2 == slot)
        def _():
            @pl.when(i + 1 < NT)
            def _():
                gather(dnxt_ref, 1 - slot)
            wait_rows(slot)
            combine(slot)


def _final(dest_flat, x1, top_g, mod3, g, b, ybp):
    tok = pl.BlockSpec((TM, D_MODEL), lambda i: (i, 0))
    vec = pl.BlockSpec((1, D_MODEL), lambda i: (0, 0))
    return pl.pallas_call(
        _final_kernel,
        grid=(NT,),
        in_specs=[pl.BlockSpec((ROWS_PER_TILE,), lambda i: (i,), memory_space=pltpu.SMEM),
                  pl.BlockSpec((ROWS_PER_TILE,), lambda i: (jnp.minimum(i + 1, NT - 1),),
                               memory_space=pltpu.SMEM),
                  tok, pl.BlockSpec((TM, LANES), lambda i: (i, 0)),
                  pl.BlockSpec((1, 1, 6 * D_MODEL), lambda i: (_mod_row(i), 0, 0)), vec, vec,
                  pl.BlockSpec(memory_space=pl.ANY)],
        scratch_shapes=[pltpu.VMEM((2, TOP_K, TM // SUBLANES, SUBLANES, PACK_W), jnp.uint32),
                        pltpu.SemaphoreType.DMA((2,))],
        out_specs=[pl.BlockSpec((TM, D_MODEL), lambda i: (_ctx_idx(i), 0)),
                   pl.BlockSpec((TM, D_MODEL), lambda i: (_lat_idx(i), 0))],
        out_shape=[jax.ShapeDtypeStruct((NT_CTX * TM, D_MODEL), F32),
                   jax.ShapeDtypeStruct((NT_LAT * TM, D_MODEL), F32)],
        compiler_params=_params(("arbitrary",)),
        name="residual_ln2",
    )(dest_flat, dest_flat, x1, top_g, mod3, g, b, ybp)


def _rope_tables():
    t = jnp.arange(LAT_LEN)
    r = (t // GRID_W).astype(F32)
    col = (t % GRID_W).astype(F32)
    inv = jnp.power(ROPE_BASE, -jnp.arange(ROPE_HALF, dtype=F32) / ROPE_HALF)
    ang_r = r[:, None] * inv
    ang_c = col[:, None] * inv
    ang = jnp.concatenate([ang_r, ang_r, ang_c, ang_c], axis=-1)
    cos = jnp.tile(jnp.cos(ang), (1, 2))
    sin = jnp.tile(jnp.sin(ang), (1, 2))
    sign = jnp.where((jnp.arange(LANES) % (2 * ROPE_HALF)) < ROPE_HALF, -1.0, 1.0).astype(F32)
    cos = jnp.concatenate([jnp.ones((TM, LANES), F32), cos], axis=0)
    sin = jnp.concatenate([jnp.zeros((TM, LANES), F32), sin * sign], axis=0)
    return cos, sin


def _routing(top_idx, rank, counts):
    n_tok = top_idx.shape[0]
    n_assign = n_tok * TOP_K
    padded = (counts + MOE_BLOCK - 1) // MOE_BLOCK * MOE_BLOCK
    pend = jnp.cumsum(padded)
    pstart = pend - padded
    dest_tk = pstart[top_idx] + rank
    n_blocks = -(-n_assign // MOE_BLOCK) + N_EXPERTS
    blk_start = jnp.arange(n_blocks, dtype=jnp.int32) * MOE_BLOCK
    blk_exp = jnp.minimum(jnp.sum((pend[None, :] <= blk_start[:, None]).astype(jnp.int32), axis=1),
                          N_EXPERTS - 1)
    n_used = (pend[-1] // MOE_BLOCK).astype(jnp.int32).reshape(1)
    return blk_exp, dest_tk.reshape(n_assign), n_used, n_blocks * MOE_BLOCK


def kernel(x_prompt, x_sample, c, cache_k, cache_v, c_ctx, w_ada, b_ada, w_in, lambda_q1, lambda_k1, lambda_q2, lambda_k2, subln_w, gmlp_ln_g, gmlp_ln_b, w_spatial, b_spatial, b_gate, w_pa, w_pb, w_o, ln1_g, ln1_b, w_router, b_router, w_up, b_up, w_down, b_down, ln2_g, ln2_b):
    l = 0
    xp = x_prompt.reshape(N_CTX_B * CTX_LEN, D_MODEL)
    xs = x_sample.reshape(N_LAT_B * LAT_LEN, D_MODEL)

    cond = jnp.concatenate([c_ctx[None, :], c, jnp.zeros((16 - 1 - N_LAT_B, D_MODEL), F32)], axis=0)
    mod3 = _modulation(cond, w_ada[l], b_ada[l][None, :]).reshape(16, 1, 6 * D_MODEL)

    cos_t, sin_t = _rope_tables()
    (q_all, k_all, v_all, new_k, new_v, ug, vg, sa, sb) = _input_projection(
        xp, xs, mod3, w_in[l].astype(BF16), cos_t, sin_t, b_gate[l],
        gmlp_ln_g[l][None, :], gmlp_ln_b[l][None, :])

    lam = (jnp.exp(jnp.sum(lambda_q1[l] * lambda_k1[l])) - jnp.exp(jnp.sum(lambda_q2[l] * lambda_k2[l]))
           + LAM_INIT).reshape(1).astype(F32)
    subw = (subln_w[l] * (1.0 - LAM_INIT))[None, :]
    ctx_k = cache_k[:, l].reshape(N_LAT_B, N_HEADS, PAST_LEN, HEAD_W)
    ctx_v = cache_v[:, l]
    a_ctx = _attention(lam, q_all, k_all, v_all, None, None, subw, latent=False)
    a_lat = _attention(lam, q_all, k_all, v_all, ctx_k, ctx_v, subw, latent=True)

    bs_full = jnp.repeat(b_spatial[l].T, D_MODEL // GMLP_GROUPS, axis=1)
    wr = jnp.pad(w_router[l], ((0, 0), (0, LANES - N_EXPERTS)))
    wr_hi = wr.astype(BF16)
    wr_lo = (wr - wr_hi.astype(F32)).astype(BF16)
    br = jnp.concatenate([b_router[l], jnp.full((LANES - N_EXPERTS,), NEG_BIG, F32)])[None, :]
    x1, h2, top_i, top_g, cnt = _mix(
        a_ctx, a_lat, ug, vg, sa, sb, xp, xs, mod3, w_spatial[l].astype(BF16), bs_full,
        w_pa[l].astype(BF16), w_pb[l].astype(BF16), w_o[l].astype(BF16),
        ln1_g[l][None, :], ln1_b[l][None, :], wr_hi, wr_lo, br)

    blk_exp, dest_flat, n_used, n_rows = _routing(
        top_i[:, :TOP_K], top_i[:, TOP_K:2 * TOP_K], cnt[0, :N_EXPERTS].astype(jnp.int32))
    xb = _dispatch(dest_flat, h2, n_rows)
    yb = _experts(blk_exp, n_used, xb, w_up[l], b_up[l][:, None, :], w_down[l], b_down[l][:, None, :])
    y_prompt, y_sample = _final(dest_flat, x1, top_g, mod3, ln2_g[l][None, :], ln2_b[l][None, :], yb)
    return (y_prompt.reshape(N_CTX_B, CTX_LEN, D_MODEL),
            y_sample.reshape(N_LAT_B, LAT_LEN, D_MODEL),
            new_k.reshape(N_CTX_B, DEPTH, N_HEADS, CTX_LEN, 2, HEAD_DIM),
            new_v.reshape(N_CTX_B, DEPTH, N_HEADS, CTX_LEN, HEAD_W))
```

```python
import functools
import math

import jax
import jax.numpy as jnp
from jax import lax
from jax.experimental import pallas as pl
from jax.experimental.pallas import tpu as pltpu

F32 = jnp.float32
BF16 = jnp.bfloat16

D_MODEL = 1024
N_CTX_B = 16
CTX_LEN = 256
N_LAT_B = 8
LAT_LEN = 4096
PAST_LEN = 256
GRID_W = 64
N_HEADS = 8
HEAD_DIM = 64
HEAD_W = 2 * HEAD_DIM
ROPE_HALF = HEAD_DIM // 4
ROPE_BASE = 10000.0
GMLP_GROUPS = 4
CHUNK = 128
N_EXPERTS = 32
TOP_K = 4
D_EXPERT = 1024
SWIGLU_LIMIT = 7.0
SWIGLU_ALPHA = 1.702
MOE_BLOCK = 256
LN_EPS = 1e-5
DEPTH = 1
N_SEG = 7

TM = 256
NT_CTX = N_CTX_B * CTX_LEN // TM
NT_LAT = N_LAT_B * LAT_LEN // TM
NT = NT_CTX + NT_LAT
T_ALL = NT * TM
LAT_TILES = LAT_LEN // TM
LANES = 128
KEY_CHUNK = 512
SCORE_LEAD = 1
Q_TILES = 2
HEADS_PER_STEP = 2
ONES_ROWS = 16
PACK_W = D_MODEL // 2
ROWS_PER_TILE = TM * TOP_K
SUBLANES = 8
NEG_BIG = -1e30
VMEM_LIMIT = 56 * 1024 * 1024

ALPHA = (2.0 * DEPTH) ** 0.25
LAM_INIT = 0.8 - 0.6 * math.exp(-0.3 * 0)
Q_SCALE = HEAD_DIM ** -0.5 * math.log2(math.e)


def _ctx_idx(i):
    return jnp.minimum(i, NT_CTX - 1)


def _lat_idx(i):
    return jnp.maximum(i - NT_CTX, 0)


def _mod_row(i):
    return jnp.where(i < NT_CTX, 0, 1 + (i - NT_CTX) // LAT_TILES)


def _layer_norm(x, g, b):
    mu = jnp.mean(x, axis=-1, keepdims=True)
    xc = x - mu
    var = jnp.mean(xc * xc, axis=-1, keepdims=True)
    return xc * lax.rsqrt(var + LN_EPS) * g + b


def _gelu(x):
    return 0.5 * x * (1.0 + lax.erf(x * (1.0 / math.sqrt(2.0))))


def _split(x):
    hi = x.astype(BF16)
    return hi, (x - hi.astype(F32)).astype(BF16)


def _pack_pairs(x):
    w = x.shape[1] // 2
    a = pltpu.bitcast(x[:, :w].astype(BF16).astype(F32), jnp.uint32)
    b = pltpu.bitcast(x[:, w:].astype(BF16).astype(F32), jnp.uint32)
    return lax.shift_right_logical(a, jnp.uint32(16)) | b


def _unpack_pairs(p):
    a = pltpu.bitcast(lax.shift_left(p, jnp.uint32(16)), F32)
    b = pltpu.bitcast(p & jnp.uint32(0xFFFF0000), F32)
    return jnp.concatenate([a, b], axis=1)


def _params(sem):
    return pltpu.CompilerParams(dimension_semantics=sem, vmem_limit_bytes=VMEM_LIMIT)


def _mod_kernel(c_ref, w_ref, b_ref, o_ref):
    c = c_ref[...]
    s = c * jax.nn.sigmoid(c)
    s_hi, s_lo = _split(s)
    w_hi, w_lo = _split(w_ref[...])
    o_ref[...] = (jnp.dot(s_hi, w_hi, preferred_element_type=F32)
                  + jnp.dot(s_lo, w_hi, preferred_element_type=F32)
                  + jnp.dot(s_hi, w_lo, preferred_element_type=F32)) + b_ref[...]


def _modulation(cond, w_ada, b_ada):
    n = cond.shape[0]
    return pl.pallas_call(
        _mod_kernel,
        grid=(6,),
        in_specs=[pl.BlockSpec((n, D_MODEL), lambda j: (0, 0)),
                  pl.BlockSpec((D_MODEL, D_MODEL), lambda j: (0, j)),
                  pl.BlockSpec((1, D_MODEL), lambda j: (0, j))],
        out_specs=pl.BlockSpec((n, D_MODEL), lambda j: (0, j)),
        out_shape=jax.ShapeDtypeStruct((n, 6 * D_MODEL), F32),
        compiler_params=_params(("arbitrary",)),
        name="adaln_mod",
    )(cond, w_ada, b_ada)


def _proj_kernel(xp_ref, xs_ref, mod_ref, w_ref, cos_ref, sin_ref, bg_ref, lg_ref, lb_ref,
                 q_ref, k_ref, v_ref, nk_ref, nv_ref, ug_ref, vg_ref, sa_ref, sb_ref):
    i = pl.program_id(0)
    is_ctx = i < NT_CTX
    x = jnp.where(is_ctx, xp_ref[...], xs_ref[...])
    mod = mod_ref[0]
    sh1 = mod[:, 0:D_MODEL]
    sc1 = mod[:, D_MODEL:2 * D_MODEL]
    h = (x * (1.0 + sc1) + sh1).astype(BF16)

    def seg(s):
        return jnp.dot(h, w_ref[:, s * D_MODEL:(s + 1) * D_MODEL], preferred_element_type=F32)

    cos = cos_ref[...]
    sin = sin_ref[...]
    lane = lax.broadcasted_iota(jnp.int32, (TM, LANES), 1)
    first = (lane % (2 * ROPE_HALF)) < ROPE_HALF

    def rope(zh):
        up = pltpu.roll(zh, LANES - ROPE_HALF, 1)
        dn = pltpu.roll(zh, ROPE_HALF, 1)
        return zh * cos + jnp.where(first, up, dn) * sin

    zq = seg(0)
    for hd in range(N_HEADS):
        q_ref[0, hd] = (rope(zq[:, hd * HEAD_W:(hd + 1) * HEAD_W]) * Q_SCALE).astype(BF16)
    zk = seg(1)
    for hd in range(N_HEADS):
        k_ref[0, hd] = rope(zk[:, hd * HEAD_W:(hd + 1) * HEAD_W]).astype(BF16)
    zv = seg(2)
    for hd in range(N_HEADS):
        v_ref[0, hd] = zv[:, hd * HEAD_W:(hd + 1) * HEAD_W].astype(BF16)

    @pl.when(is_ctx)
    def _():
        for hd in range(N_HEADS):
            nk_ref[0, hd] = zk[:, hd * HEAD_W:(hd + 1) * HEAD_W]
            nv_ref[0, hd] = zv[:, hd * HEAD_W:(hd + 1) * HEAD_W]

    ug_ref[...] = _gelu(seg(3)).astype(BF16)
    vg_ref[...] = _layer_norm(_gelu(seg(4)), lg_ref[...], lb_ref[...]).astype(BF16)
    sa_ref[...] = jax.nn.sigmoid(seg(5) + bg_ref[0:1, :]).astype(BF16)
    sb_ref[...] = jax.nn.sigmoid(seg(6) + bg_ref[1:2, :]).astype(BF16)


def _input_projection(xp, xs, mod3, w_in, cos_t, sin_t, b_gate, ln_g, ln_b):
    tile = lambda i: (i, 0)
    head_blk = (1, N_HEADS, TM, HEAD_W)
    tok_spec = pl.BlockSpec((TM, D_MODEL), tile)
    return pl.pallas_call(
        _proj_kernel,
        grid=(NT,),
        in_specs=[
            pl.BlockSpec((TM, D_MODEL), lambda i: (_ctx_idx(i), 0)),
            pl.BlockSpec((TM, D_MODEL), lambda i: (_lat_idx(i), 0)),
            pl.BlockSpec((1, 1, 6 * D_MODEL), lambda i: (_mod_row(i), 0, 0)),
            pl.BlockSpec((D_MODEL, N_SEG * D_MODEL), lambda i: (0, 0)),
            pl.BlockSpec((TM, LANES), lambda i: (jnp.where(i < NT_CTX, 0, 1 + (i - NT_CTX) % LAT_TILES), 0)),
            pl.BlockSpec((TM, LANES), lambda i: (jnp.where(i < NT_CTX, 0, 1 + (i - NT_CTX) % LAT_TILES), 0)),
            pl.BlockSpec((2, D_MODEL), lambda i: (0, 0)),
            pl.BlockSpec((1, D_MODEL), lambda i: (0, 0)),
            pl.BlockSpec((1, D_MODEL), lambda i: (0, 0)),
        ],
        out_specs=[
            pl.BlockSpec(head_blk, lambda i: (i, 0, 0, 0)),
            pl.BlockSpec(head_blk, lambda i: (i, 0, 0, 0)),
            pl.BlockSpec(head_blk, lambda i: (i, 0, 0, 0)),
            pl.BlockSpec(head_blk, lambda i: (_ctx_idx(i), 0, 0, 0)),
            pl.BlockSpec(head_blk, lambda i: (_ctx_idx(i), 0, 0, 0)),
            tok_spec, tok_spec, tok_spec, tok_spec,
        ],
        out_shape=[
            jax.ShapeDtypeStruct((NT, N_HEADS, TM, HEAD_W), BF16),
            jax.ShapeDtypeStruct((NT, N_HEADS, TM, HEAD_W), BF16),
            jax.ShapeDtypeStruct((NT, N_HEADS, TM, HEAD_W), BF16),
            jax.ShapeDtypeStruct((NT_CTX, N_HEADS, TM, HEAD_W), F32),
            jax.ShapeDtypeStruct((NT_CTX, N_HEADS, TM, HEAD_W), F32),
            jax.ShapeDtypeStruct((T_ALL, D_MODEL), BF16),
            jax.ShapeDtypeStruct((T_ALL, D_MODEL), BF16),
            jax.ShapeDtypeStruct((T_ALL, D_MODEL), BF16),
            jax.ShapeDtypeStruct((T_ALL, D_MODEL), BF16),
        ],
        compiler_params=_params(("arbitrary",)),
        name="input_projection",
    )(xp, xs, mod3, w_in, cos_t, sin_t, b_gate, ln_g, ln_b)


def _attn_kernel(n_main, has_ctx, lam_ref, q_ref, k_ref, v_ref, *rest):
    if has_ctx:
        ck_ref, cv_ref, sub_ref, o_ref, kall, vt_all = rest
    else:
        sub_ref, o_ref, kall, vt_all = rest
    n_heads, n_keys = kall.shape[0], kall.shape[1]

    @pl.when(pl.program_id(2) == 0)
    def _():
        r = lax.broadcasted_iota(jnp.int32, (ONES_ROWS, n_keys), 0)
        for j in range(n_heads):
            for c in range(n_main):
                kall[j, c * TM:(c + 1) * TM, :] = k_ref[c, j]
                vt_all[j, 0:HEAD_W, c * TM:(c + 1) * TM] = v_ref[c, j].astype(F32).T.astype(BF16)
            if has_ctx:
                kall[j, n_main * TM:n_keys, :] = ck_ref[0, j].astype(BF16)
                vt_all[j, 0:HEAD_W, n_main * TM:n_keys] = cv_ref[0, j].T.astype(BF16)
            vt_all[j, HEAD_W:HEAD_W + ONES_ROWS, :] = jnp.where(r == 0, 1.0, 0.0).astype(BF16)

    tq = q_ref.shape[0] * TM
    lane = lax.broadcasted_iota(jnp.int32, (tq, HEAD_W), 1)
    qqs = []
    for j in range(n_heads):
        q = q_ref[:, j].reshape(tq, HEAD_W)
        zero = jnp.zeros_like(q)
        qqs.append(jnp.concatenate([jnp.where(lane < HEAD_DIM, q, zero),
                                    jnp.where(lane >= HEAD_DIM, q, zero)], axis=0))

    def scores(j, st, sz):
        return lax.dot_general(kall[j, st:st + sz, :], qqs[j], (((1,), (1,)), ((), ())),
                               preferred_element_type=F32)

    chunks = [(st, min(KEY_CHUNK, n_keys - st)) for st in range(0, n_keys, KEY_CHUNK)]
    pending = [[scores(j, *ch) for ch in chunks[:SCORE_LEAD]] for j in range(n_heads)]
    m = [None] * n_heads
    acc = [None] * n_heads
    for c, (st, sz) in enumerate(chunks):
        for j in range(n_heads):
            s = pending[j].pop(0)
            if c + SCORE_LEAD < len(chunks):
                pending[j].append(scores(j, *chunks[c + SCORE_LEAD]))
            m_c = jnp.max(s, axis=0, keepdims=True)
            m_new = m_c if c == 0 else jnp.maximum(m[j], m_c)
            e = jnp.exp2(s - m_new).astype(BF16)
            pv = jnp.dot(vt_all[j, :, st:st + sz], e, preferred_element_type=F32)
            acc[j] = pv if c == 0 else jnp.exp2(m[j] - m_new) * acc[j] + pv
            m[j] = m_new
    for j in range(n_heads):
        l = acc[j][HEAD_W:HEAD_W + 1, :]
        c1 = 1.0 / l[:, :tq]
        c2 = lam_ref[0] / l[:, tq:]
        a = (acc[j][:HEAD_W, :tq] * c1 - acc[j][:HEAD_W, tq:] * c2).T
        ms = jnp.mean(a * a, axis=-1, keepdims=True)
        o_ref[:, j * HEAD_W:(j + 1) * HEAD_W] = (a * lax.rsqrt(ms + LN_EPS) * sub_ref[...]).astype(BF16)


def _attention(lam, q_all, k_all, v_all, ctx_k, ctx_v, subw, latent):
    if latent:
        n_b, n_q, n_main, base, qt = N_LAT_B, LAT_TILES // Q_TILES, LAT_TILES, NT_CTX, Q_TILES
        n_keys, hp = LAT_LEN + PAST_LEN, HEADS_PER_STEP
    else:
        n_b, n_q, n_main, base, qt = N_CTX_B, 1, 1, 0, 1
        n_keys, hp = CTX_LEN, N_HEADS
    q_spec = pl.BlockSpec((qt, hp, TM, HEAD_W), lambda b, h, t: (base // qt + b * n_q + t, h, 0, 0))
    kv_spec = pl.BlockSpec((n_main, hp, TM, HEAD_W), lambda b, h, t: (base // n_main + b, h, 0, 0))
    in_specs = [pl.BlockSpec(memory_space=pltpu.SMEM), q_spec, kv_spec, kv_spec]
    args = [lam, q_all, k_all, v_all]
    if latent:
        c_spec = pl.BlockSpec((1, hp, PAST_LEN, HEAD_W), lambda b, h, t: (b, h, 0, 0))
        in_specs += [c_spec, c_spec]
        args += [ctx_k, ctx_v]
    in_specs.append(pl.BlockSpec((1, HEAD_W), lambda b, h, t: (0, 0)))
    args.append(subw)
    return pl.pallas_call(
        functools.partial(_attn_kernel, n_main, latent),
        grid=(n_b, N_HEADS // hp, n_q),
        in_specs=in_specs,
        out_specs=pl.BlockSpec((qt * TM, hp * HEAD_W), lambda b, h, t: (b * n_q + t, h)),
        out_shape=jax.ShapeDtypeStruct((n_b * n_q * qt * TM, D_MODEL), BF16),
        scratch_shapes=[pltpu.VMEM((hp, n_keys, HEAD_W), BF16),
                        pltpu.VMEM((hp, HEAD_W + ONES_ROWS, n_keys), BF16)],
        compiler_params=_params(("arbitrary", "arbitrary", "arbitrary")),
        name="diff_attention_lat" if latent else "diff_attention_ctx",
    )(*args)


def _mix_kernel(ac_ref, al_ref, ug_ref, vg_ref, sa_ref, sb_ref, xp_ref, xs_ref, mod_ref,
                ws_ref, bs_ref, wpa_ref, wpb_ref, wo_ref, g_ref, b_ref, wrc_ref, br_ref,
                x1_ref, h2_ref, ti_ref, tg_ref, cnt_ref, run_ref):
    i = pl.program_id(0)
    is_ctx = i < NT_CTX
    a = jnp.where(is_ctx, ac_ref[...], al_ref[...])
    x = jnp.where(is_ctx, xp_ref[...], xs_ref[...])
    mod = mod_ref[0]
    g1 = mod[:, 2 * D_MODEL:3 * D_MODEL]
    sh2 = mod[:, 3 * D_MODEL:4 * D_MODEL]
    sc2 = mod[:, 4 * D_MODEL:5 * D_MODEL]

    gw = D_MODEL // GMLP_GROUPS
    chunks = []
    for c in range(TM // CHUNK):
        groups = []
        for g in range(GMLP_GROUPS):
            vc = vg_ref[c * CHUNK:(c + 1) * CHUNK, g * gw:(g + 1) * gw]
            groups.append(jnp.dot(ws_ref[g], vc, preferred_element_type=F32))
        chunks.append(jnp.concatenate(groups, axis=1) + bs_ref[...])
    sp = jnp.concatenate(chunks, axis=0)
    gm = (ug_ref[...].astype(F32) * sp).astype(BF16)

    pa = jnp.dot(a, wpa_ref[...], preferred_element_type=F32)
    pb = jnp.dot(gm, wpb_ref[...], preferred_element_type=F32)
    merged = (sa_ref[...].astype(F32) * pa + sb_ref[...].astype(F32) * pb).astype(BF16)
    mix = jnp.dot(merged, wo_ref[...], preferred_element_type=F32)
    x1 = _layer_norm(ALPHA * x + g1 * mix, g_ref[...], b_ref[...])
    x1_ref[...] = x1
    h2 = x1 * (1.0 + sc2) + sh2
    hi, lo = _split(h2)
    h2_ref[...] = _pack_pairs(h2)
    both = jnp.dot(hi, wrc_ref[...], preferred_element_type=F32)
    logits = (both[:, :LANES] + jnp.dot(lo, wrc_ref[:, :LANES], preferred_element_type=F32)
              + both[:, LANES:]) + br_ref[...]
    lane = lax.broadcasted_iota(jnp.int32, (TM, LANES), 1).astype(F32)
    vals, idxs = [], []
    for _ in range(TOP_K):
        mx = jnp.max(logits, axis=-1, keepdims=True)
        ix = jnp.min(jnp.where(logits == mx, lane, float(LANES)), axis=-1, keepdims=True)
        vals.append(mx)
        idxs.append(ix)
        logits = jnp.where(lane == ix, NEG_BIG * 2.0, logits)
    es = [jnp.exp(v - vals[0]) for v in vals]
    inv = 1.0 / (es[0] + es[1] + es[2] + es[3])

    @pl.when(i == 0)
    def _():
        run_ref[...] = jnp.zeros_like(run_ref)

    hot = [lane == ix for ix in idxs]
    memb = jnp.zeros((TM, LANES), F32)
    for hk in hot:
        memb = jnp.where(hk, 1.0, memb)
    row = lax.broadcasted_iota(jnp.int32, (TM, TM), 0)
    col = lax.broadcasted_iota(jnp.int32, (TM, TM), 1)
    before = jnp.where(row > col, 1.0, 0.0).astype(BF16)
    base = run_ref[...]
    rank_all = jnp.dot(before, memb.astype(BF16), preferred_element_type=F32) + base
    total = base + jnp.sum(memb, axis=0, keepdims=True)
    run_ref[...] = total
    cnt_ref[...] = total

    ti = jnp.zeros((TM, LANES), F32)
    tg = jnp.zeros((TM, LANES), F32)
    for k in range(TOP_K):
        rank_k = jnp.sum(jnp.where(hot[k], rank_all, 0.0), axis=-1, keepdims=True)
        ti = jnp.where(lane == float(k), idxs[k], ti)
        ti = jnp.where(lane == float(TOP_K + k), rank_k, ti)
        tg = jnp.where(lane == float(k), es[k] * inv, tg)
    ti_ref[...] = ti.astype(jnp.int32)
    tg_ref[...] = tg


def _mix(a_ctx, a_lat, ug, vg, sa, sb, xp, xs, mod3, ws, bs, wpa, wpb, wo, g, b, wrc, br):
    tile = lambda i: (i, 0)
    full2 = lambda i: (0, 0)
    tok = pl.BlockSpec((TM, D_MODEL), tile)
    ctx = pl.BlockSpec((TM, D_MODEL), lambda i: (_ctx_idx(i), 0))
    lat = pl.BlockSpec((TM, D_MODEL), lambda i: (_lat_idx(i), 0))
    wsq = pl.BlockSpec((D_MODEL, D_MODEL), full2)
    vec = pl.BlockSpec((1, D_MODEL), full2)
    return pl.pallas_call(
        _mix_kernel,
        grid=(NT,),
        in_specs=[ctx, lat, tok, tok, tok, tok, ctx, lat,
                  pl.BlockSpec((1, 1, 6 * D_MODEL), lambda i: (_mod_row(i), 0, 0)),
                  pl.BlockSpec((GMLP_GROUPS, CHUNK, CHUNK), lambda i: (0, 0, 0)),
                  pl.BlockSpec((CHUNK, D_MODEL), full2),
                  wsq, wsq, wsq, vec, vec,
                  pl.BlockSpec((D_MODEL, 2 * LANES), full2),
                  pl.BlockSpec((1, LANES), full2)],
        out_specs=[tok, pl.BlockSpec((TM, PACK_W), tile),
                   pl.BlockSpec((TM, LANES), tile), pl.BlockSpec((TM, LANES), tile),
                   pl.BlockSpec((1, LANES), full2)],
        out_shape=[jax.ShapeDtypeStruct((T_ALL, D_MODEL), F32),
                   jax.ShapeDtypeStruct((T_ALL, PACK_W), jnp.uint32),
                   jax.ShapeDtypeStruct((T_ALL, LANES), jnp.int32),
                   jax.ShapeDtypeStruct((T_ALL, LANES), F32),
                   jax.ShapeDtypeStruct((1, LANES), F32)],
        scratch_shapes=[pltpu.VMEM((1, LANES), F32)],
        compiler_params=_params(("arbitrary",)),
        name="mix_ln1_router",
    )(a_ctx, a_lat, ug, vg, sa, sb, xp, xs, mod3, ws, bs, wpa, wpb, wo, g, b, wrc, br)


def _dispatch_kernel(dest_ref, h_ref, init_hbm, out_hbm, sem):
    del init_hbm

    def issue(g, carry):
        base = g * (SUBLANES * TOP_K)
        for u in range(SUBLANES):
            for k in range(TOP_K):
                dest = dest_ref[base + (u * TOP_K + k)]
                pltpu.make_async_copy(h_ref.at[g, pl.ds(u, 1)], out_hbm.at[pl.ds(dest, 1)],
                                      sem).start(priority=k % 2)
        return carry

    lax.fori_loop(0, TM // SUBLANES, issue, 0)
    for _ in range(TOP_K):
        pltpu.make_async_copy(out_hbm.at[pl.ds(0, TM)], out_hbm.at[pl.ds(0, TM)], sem).wait()


def _dispatch(dest_flat, h2p, n_rows):
    return pl.pallas_call(
        _dispatch_kernel,
        grid=(NT,),
        in_specs=[pl.BlockSpec((ROWS_PER_TILE,), lambda i: (i,), memory_space=pltpu.SMEM),
                  pl.BlockSpec((TM // SUBLANES, SUBLANES, PACK_W), lambda i: (i, 0, 0)),
                  pl.BlockSpec(memory_space=pl.ANY)],
        out_specs=pl.BlockSpec(memory_space=pl.ANY),
        out_shape=jax.ShapeDtypeStruct((n_rows, PACK_W), jnp.uint32),
        scratch_shapes=[pltpu.SemaphoreType.DMA(())],
        input_output_aliases={2: 0},
        compiler_params=_params(("arbitrary",)),
        name="moe_dispatch",
    )(dest_flat, h2p.reshape(T_ALL // SUBLANES, SUBLANES, PACK_W),
      jnp.zeros((n_rows, PACK_W), jnp.uint32))


def _expert_kernel(be_ref, nu_ref, x_ref, wu_ref, bu_ref, wd_ref, bd_ref, y_ref, wu_bf, wd_bf):
    i = pl.program_id(0)
    used = i < nu_ref[0]
    new_expert = jnp.logical_or(i == 0, be_ref[i] != be_ref[jnp.maximum(i - 1, 0)])

    @pl.when(jnp.logical_and(used, new_expert))
    def _():
        wu_bf[...] = wu_ref[0].astype(BF16)
        wd_bf[...] = wd_ref[0].astype(BF16)

    @pl.when(used)
    def _():
        x = _unpack_pairs(x_ref[...]).astype(BF16)
        hu = jnp.dot(x, wu_bf[...], preferred_element_type=F32) + bu_ref[0]
        glu = jnp.minimum(hu[:, :D_EXPERT], SWIGLU_LIMIT)
        lin = jnp.clip(hu[:, D_EXPERT:], -SWIGLU_LIMIT, SWIGLU_LIMIT)
        act = glu * jax.nn.sigmoid(SWIGLU_ALPHA * glu) * (lin + 1.0)
        y = jnp.dot(act.astype(BF16), wd_bf[...], preferred_element_type=F32) + bd_ref[0]
        y_ref[...] = _pack_pairs(y)

    @pl.when(jnp.logical_not(used))
    def _():
        y_ref[...] = jnp.zeros_like(y_ref)


def _experts(blk_exp, n_used, xb, w_up, b_up, w_down, b_down):
    n_blocks = xb.shape[0] // MOE_BLOCK
    grid_spec = pltpu.PrefetchScalarGridSpec(
        num_scalar_prefetch=2,
        grid=(n_blocks,),
        in_specs=[pl.BlockSpec((MOE_BLOCK, PACK_W), lambda i, be, nu: (i, 0)),
                  pl.BlockSpec((1, D_MODEL, 2 * D_EXPERT), lambda i, be, nu: (be[i], 0, 0)),
                  pl.BlockSpec((1, 1, 2 * D_EXPERT), lambda i, be, nu: (be[i], 0, 0)),
                  pl.BlockSpec((1, D_EXPERT, D_MODEL), lambda i, be, nu: (be[i], 0, 0)),
                  pl.BlockSpec((1, 1, D_MODEL), lambda i, be, nu: (be[i], 0, 0))],
        out_specs=pl.BlockSpec((MOE_BLOCK, PACK_W), lambda i, be, nu: (i, 0)),
        scratch_shapes=[pltpu.VMEM((D_MODEL, 2 * D_EXPERT), BF16),
                        pltpu.VMEM((D_EXPERT, D_MODEL), BF16)],
    )
    return pl.pallas_call(
        _expert_kernel,
        grid_spec=grid_spec,
        out_shape=jax.ShapeDtypeStruct((n_blocks * MOE_BLOCK, PACK_W), jnp.uint32),
        compiler_params=_params(("arbitrary",)),
        name="expert_ffn",
    )(blk_exp, n_used, xb, w_up, b_up, w_down, b_down)


def _final_kernel(dcur_ref, dnxt_ref, x1_ref, tg_ref, mod_ref, g_ref, b_ref, yb_hbm,
                  yp_ref, ys_ref, ybuf, sem):
    i = pl.program_id(0)

    def gather(dest_ref, slot):
        def issue(g, carry):
            base = g * (SUBLANES * TOP_K)
            for u in range(SUBLANES):
                for k in range(TOP_K):
                    dest = dest_ref[base + (u * TOP_K + k)]
                    pltpu.make_async_copy(yb_hbm.at[pl.ds(dest, 1)],
                                          ybuf.at[slot, k, g, pl.ds(u, 1)],
                                          sem.at[slot]).start(priority=k % 2)
            return carry
        lax.fori_loop(0, TM // SUBLANES, issue, 0)

    def wait_rows(slot):
        for k in range(TOP_K):
            pltpu.make_async_copy(yb_hbm.at[pl.ds(0, TM)], yb_hbm.at[pl.ds(0, TM)],
                                  sem.at[slot]).wait()

    def combine(slot):
        g2 = mod_ref[0][:, 5 * D_MODEL:6 * D_MODEL]
        gates = tg_ref[...]
        ffn = jnp.zeros((TM, D_MODEL), F32)
        for k in range(TOP_K):
            yk = ybuf[slot, k].reshape(TM, PACK_W)
            ffn = ffn + gates[:, k:k + 1] * _unpack_pairs(yk)
        y = _layer_norm(ALPHA * x1_ref[...] + g2 * ffn, g_ref[...], b_ref[...])

        @pl.when(i < NT_CTX)
        def _():
            yp_ref[...] = y

        @pl.when(i >= NT_CTX)
        def _():
            ys_ref[...] = y

    @pl.when(i == 0)
    def _():
        gather(dcur_ref, 0)

    for slot in range(2):
        @pl.when(i % 2 == slot)
        def _():
            @pl.when(i + 1 < NT)
            def _():
                gather(dnxt_ref, 1 - slot)
            wait_rows(slot)
            combine(slot)


def _final(dest_flat, x1, top_g, mod3, g, b, ybp):
    tok = pl.BlockSpec((TM, D_MODEL), lambda i: (i, 0))
    vec = pl.BlockSpec((1, D_MODEL), lambda i: (0, 0))
    return pl.pallas_call(
        _final_kernel,
        grid=(NT,),
        in_specs=[pl.BlockSpec((ROWS_PER_TILE,), lambda i: (i,), memory_space=pltpu.SMEM),
                  pl.BlockSpec((ROWS_PER_TILE,), lambda i: (jnp.minimum(i + 1, NT - 1),),
                               memory_space=pltpu.SMEM),
                  tok, pl.BlockSpec((TM, LANES), lambda i: (i, 0)),
                  pl.BlockSpec((1, 1, 6 * D_MODEL), lambda i: (_mod_row(i), 0, 0)), vec, vec,
                  pl.BlockSpec(memory_space=pl.ANY)],
        scratch_shapes=[pltpu.VMEM((2, TOP_K, TM // SUBLANES, SUBLANES, PACK_W), jnp.uint32),
                        pltpu.SemaphoreType.DMA((2,))],
        out_specs=[pl.BlockSpec((TM, D_MODEL), lambda i: (_ctx_idx(i), 0)),
                   pl.BlockSpec((TM, D_MODEL), lambda i: (_lat_idx(i), 0))],
        out_shape=[jax.ShapeDtypeStruct((NT_CTX * TM, D_MODEL), F32),
                   jax.ShapeDtypeStruct((NT_LAT * TM, D_MODEL), F32)],
        compiler_params=_params(("arbitrary",)),
        name="residual_ln2",
    )(dest_flat, dest_flat, x1, top_g, mod3, g, b, ybp)


def _rope_tables():
    t = jnp.arange(LAT_LEN)
    r = (t // GRID_W).astype(F32)
    col = (t % GRID_W).astype(F32)
    inv = jnp.power(ROPE_BASE, -jnp.arange(ROPE_HALF, dtype=F32) / ROPE_HALF)
    ang_r = r[:, None] * inv
    ang_c = col[:, None] * inv
    ang = jnp.concatenate([ang_r, ang_r, ang_c, ang_c], axis=-1)
    cos = jnp.tile(jnp.cos(ang), (1, 2))
    sin = jnp.tile(jnp.sin(ang), (1, 2))
    sign = jnp.where((jnp.arange(LANES) % (2 * ROPE_HALF)) < ROPE_HALF, -1.0, 1.0).astype(F32)
    cos = jnp.concatenate([jnp.ones((TM, LANES), F32), cos], axis=0)
    sin = jnp.concatenate([jnp.zeros((TM, LANES), F32), sin * sign], axis=0)
    return cos, sin


def _routing(top_idx, rank, counts):
    n_tok = top_idx.shape[0]
    n_assign = n_tok * TOP_K
    padded = (counts + MOE_BLOCK - 1) // MOE_BLOCK * MOE_BLOCK
    pend = jnp.cumsum(padded)
    pstart = pend - padded
    dest_tk = pstart[top_idx] + rank
    n_blocks = -(-n_assign // MOE_BLOCK) + N_EXPERTS
    blk_start = jnp.arange(n_blocks, dtype=jnp.int32) * MOE_BLOCK
    blk_exp = jnp.minimum(jnp.sum((pend[None, :] <= blk_start[:, None]).astype(jnp.int32), axis=1),
                          N_EXPERTS - 1)
    n_used = (pend[-1] // MOE_BLOCK).astype(jnp.int32).reshape(1)
    return blk_exp, dest_tk.reshape(n_assign), n_used, n_blocks * MOE_BLOCK


def kernel(x_prompt, x_sample, c, cache_k, cache_v, c_ctx, w_ada, b_ada, w_in, lambda_q1, lambda_k1, lambda_q2, lambda_k2, subln_w, gmlp_ln_g, gmlp_ln_b, w_spatial, b_spatial, b_gate, w_pa, w_pb, w_o, ln1_g, ln1_b, w_router, b_router, w_up, b_up, w_down, b_down, ln2_g, ln2_b):
    l = 0
    xp = x_prompt.reshape(N_CTX_B * CTX_LEN, D_MODEL)
    xs = x_sample.reshape(N_LAT_B * LAT_LEN, D_MODEL)

    cond = jnp.concatenate([c_ctx[None, :], c, jnp.zeros((16 - 1 - N_LAT_B, D_MODEL), F32)], axis=0)
    mod3 = _modulation(cond, w_ada[l], b_ada[l][None, :]).reshape(16, 1, 6 * D_MODEL)

    cos_t, sin_t = _rope_tables()
    (q_all, k_all, v_all, new_k, new_v, ug, vg, sa, sb) = _input_projection(
        xp, xs, mod3, w_in[l].astype(BF16), cos_t, sin_t, b_gate[l],
        gmlp_ln_g[l][None, :], gmlp_ln_b[l][None, :])

    lam = (jnp.exp(jnp.sum(lambda_q1[l] * lambda_k1[l])) - jnp.exp(jnp.sum(lambda_q2[l] * lambda_k2[l]))
           + LAM_INIT).reshape(1).astype(F32)
    subw = (subln_w[l] * (1.0 - LAM_INIT))[None, :]
    ctx_k = cache_k[:, l].reshape(N_LAT_B, N_HEADS, PAST_LEN, HEAD_W)
    ctx_v = cache_v[:, l]
    a_ctx = _attention(lam, q_all, k_all, v_all, None, None, subw, latent=False)
    a_lat = _attention(lam, q_all, k_all, v_all, ctx_k, ctx_v, subw, latent=True)

    bs_full = jnp.repeat(b_spatial[l].T, D_MODEL // GMLP_GROUPS, axis=1)
    wr = jnp.pad(w_router[l], ((0, 0), (0, LANES - N_EXPERTS)))
    wr_hi = wr.astype(BF16)
    wr_lo = (wr - wr_hi.astype(F32)).astype(BF16)
    br = jnp.concatenate([b_router[l], jnp.full((LANES - N_EXPERTS,), NEG_BIG, F32)])[None, :]
    x1, h2, top_i, top_g, cnt = _mix(
        a_ctx, a_lat, ug, vg, sa, sb, xp, xs, mod3, w_spatial[l].astype(BF16), bs_full,
        w_pa[l].astype(BF16), w_pb[l].astype(BF16), w_o[l].astype(BF16),
        ln1_g[l][None, :], ln1_b[l][None, :], jnp.concatenate([wr_hi, wr_lo], axis=1), br)

    blk_exp, dest_flat, n_used, n_rows = _routing(
        top_i[:, :TOP_K], top_i[:, TOP_K:2 * TOP_K], cnt[0, :N_EXPERTS].astype(jnp.int32))
    xb = _dispatch(dest_flat, h2, n_rows)
    yb = _experts(blk_exp, n_used, xb, w_up[l], b_up[l][:, None, :], w_down[l], b_down[l][:, None, :])
    y_prompt, y_sample = _final(dest_flat, x1, top_g, mod3, ln2_g[l][None, :], ln2_b[l][None, :], yb)
    return (y_prompt.reshape(N_CTX_B, CTX_LEN, D_MODEL),
            y_sample.reshape(N_LAT_B, LAT_LEN, D_MODEL),
            new_k.reshape(N_CTX_B, DEPTH, N_HEADS, CTX_LEN, 2, HEAD_DIM),
            new_v.reshape(N_CTX_B, DEPTH, N_HEADS, CTX_LEN, HEAD_W))
```

```python
import functools
import math

import jax
import jax.numpy as jnp
from jax import lax
from jax.experimental import pallas as pl
from jax.experimental.pallas import tpu as pltpu

F32 = jnp.float32
BF16 = jnp.bfloat16

D_MODEL = 1024
N_CTX_B = 16
CTX_LEN = 256
N_LAT_B = 8
LAT_LEN = 4096
PAST_LEN = 256
GRID_W = 64
N_HEADS = 8
HEAD_DIM = 64
HEAD_W = 2 * HEAD_DIM
ROPE_HALF = HEAD_DIM // 4
ROPE_BASE = 10000.0
GMLP_GROUPS = 4
CHUNK = 128
N_EXPERTS = 32
TOP_K = 4
D_EXPERT = 1024
SWIGLU_LIMIT = 7.0
SWIGLU_ALPHA = 1.702
MOE_BLOCK = 512
LN_EPS = 1e-5
DEPTH = 1
N_SEG = 7

TM = 256
NT_CTX = N_CTX_B * CTX_LEN // TM
NT_LAT = N_LAT_B * LAT_LEN // TM
NT = NT_CTX + NT_LAT
T_ALL = NT * TM
LAT_TILES = LAT_LEN // TM
LANES = 128
KEY_CHUNK = 512
SCORE_LEAD = 1
Q_TILES = 2
HEADS_PER_STEP = 2
ONES_ROWS = 16
PACK_W = D_MODEL // 2
ROWS_PER_TILE = TM * TOP_K
SUBLANES = 8
NEG_BIG = -1e30
VMEM_LIMIT = 56 * 1024 * 1024

ALPHA = (2.0 * DEPTH) ** 0.25
LAM_INIT = 0.8 - 0.6 * math.exp(-0.3 * 0)
Q_SCALE = HEAD_DIM ** -0.5 * math.log2(math.e)


def _ctx_idx(i):
    return jnp.minimum(i, NT_CTX - 1)


def _lat_idx(i):
    return jnp.maximum(i - NT_CTX, 0)


def _mod_row(i):
    return jnp.where(i < NT_CTX, 0, 1 + (i - NT_CTX) // LAT_TILES)


def _layer_norm(x, g, b):
    mu = jnp.mean(x, axis=-1, keepdims=True)
    xc = x - mu
    var = jnp.mean(xc * xc, axis=-1, keepdims=True)
    return xc * lax.rsqrt(var + LN_EPS) * g + b


def _gelu(x):
    return 0.5 * x * (1.0 + lax.erf(x * (1.0 / math.sqrt(2.0))))


def _split(x):
    hi = x.astype(BF16)
    return hi, (x - hi.astype(F32)).astype(BF16)


def _pack_pairs(x):
    w = x.shape[1] // 2
    a = pltpu.bitcast(x[:, :w].astype(BF16).astype(F32), jnp.uint32)
    b = pltpu.bitcast(x[:, w:].astype(BF16).astype(F32), jnp.uint32)
    return lax.shift_right_logical(a, jnp.uint32(16)) | b


def _unpack_pairs(p):
    a = pltpu.bitcast(lax.shift_left(p, jnp.uint32(16)), F32)
    b = pltpu.bitcast(p & jnp.uint32(0xFFFF0000), F32)
    return jnp.concatenate([a, b], axis=1)


def _params(sem):
    return pltpu.CompilerParams(dimension_semantics=sem, vmem_limit_bytes=VMEM_LIMIT)


def _mod_kernel(c_ref, w_ref, b_ref, o_ref):
    c = c_ref[...]
    s = c * jax.nn.sigmoid(c)
    s_hi, s_lo = _split(s)
    w_hi, w_lo = _split(w_ref[...])
    o_ref[...] = (jnp.dot(s_hi, w_hi, preferred_element_type=F32)
                  + jnp.dot(s_lo, w_hi, preferred_element_type=F32)
                  + jnp.dot(s_hi, w_lo, preferred_element_type=F32)) + b_ref[...]


def _modulation(cond, w_ada, b_ada):
    n = cond.shape[0]
    return pl.pallas_call(
        _mod_kernel,
        grid=(6,),
        in_specs=[pl.BlockSpec((n, D_MODEL), lambda j: (0, 0)),
                  pl.BlockSpec((D_MODEL, D_MODEL), lambda j: (0, j)),
                  pl.BlockSpec((1, D_MODEL), lambda j: (0, j))],
        out_specs=pl.BlockSpec((n, D_MODEL), lambda j: (0, j)),
        out_shape=jax.ShapeDtypeStruct((n, 6 * D_MODEL), F32),
        compiler_params=_params(("arbitrary",)),
        name="adaln_mod",
    )(cond, w_ada, b_ada)


def _proj_kernel(xp_ref, xs_ref, mod_ref, w_ref, cos_ref, sin_ref, bg_ref, lg_ref, lb_ref,
                 q_ref, k_ref, v_ref, nk_ref, nv_ref, ug_ref, vg_ref, sa_ref, sb_ref):
    i = pl.program_id(0)
    is_ctx = i < NT_CTX
    x = jnp.where(is_ctx, xp_ref[...], xs_ref[...])
    mod = mod_ref[0]
    sh1 = mod[:, 0:D_MODEL]
    sc1 = mod[:, D_MODEL:2 * D_MODEL]
    h = (x * (1.0 + sc1) + sh1).astype(BF16)

    def seg(s):
        return jnp.dot(h, w_ref[:, s * D_MODEL:(s + 1) * D_MODEL], preferred_element_type=F32)

    cos = cos_ref[...]
    sin = sin_ref[...]
    lane = lax.broadcasted_iota(jnp.int32, (TM, LANES), 1)
    first = (lane % (2 * ROPE_HALF)) < ROPE_HALF

    def rope(zh):
        up = pltpu.roll(zh, LANES - ROPE_HALF, 1)
        dn = pltpu.roll(zh, ROPE_HALF, 1)
        return zh * cos + jnp.where(first, up, dn) * sin

    zq = seg(0)
    for hd in range(N_HEADS):
        q_ref[0, hd] = (rope(zq[:, hd * HEAD_W:(hd + 1) * HEAD_W]) * Q_SCALE).astype(BF16)
    zk = seg(1)
    for hd in range(N_HEADS):
        k_ref[0, hd] = rope(zk[:, hd * HEAD_W:(hd + 1) * HEAD_W]).astype(BF16)
    zv = seg(2)
    for hd in range(N_HEADS):
        v_ref[0, hd] = zv[:, hd * HEAD_W:(hd + 1) * HEAD_W].astype(BF16)

    @pl.when(is_ctx)
    def _():
        for hd in range(N_HEADS):
            nk_ref[0, hd] = zk[:, hd * HEAD_W:(hd + 1) * HEAD_W]
            nv_ref[0, hd] = zv[:, hd * HEAD_W:(hd + 1) * HEAD_W]

    ug_ref[...] = _gelu(seg(3)).astype(BF16)
    vg_ref[...] = _layer_norm(_gelu(seg(4)), lg_ref[...], lb_ref[...]).astype(BF16)
    sa_ref[...] = jax.nn.sigmoid(seg(5) + bg_ref[0:1, :]).astype(BF16)
    sb_ref[...] = jax.nn.sigmoid(seg(6) + bg_ref[1:2, :]).astype(BF16)


def _input_projection(xp, xs, mod3, w_in, cos_t, sin_t, b_gate, ln_g, ln_b):
    tile = lambda i: (i, 0)
    head_blk = (1, N_HEADS, TM, HEAD_W)
    tok_spec = pl.BlockSpec((TM, D_MODEL), tile)
    return pl.pallas_call(
        _proj_kernel,
        grid=(NT,),
        in_specs=[
            pl.BlockSpec((TM, D_MODEL), lambda i: (_ctx_idx(i), 0)),
            pl.BlockSpec((TM, D_MODEL), lambda i: (_lat_idx(i), 0)),
            pl.BlockSpec((1, 1, 6 * D_MODEL), lambda i: (_mod_row(i), 0, 0)),
            pl.BlockSpec((D_MODEL, N_SEG * D_MODEL), lambda i: (0, 0)),
            pl.BlockSpec((TM, LANES), lambda i: (jnp.where(i < NT_CTX, 0, 1 + (i - NT_CTX) % LAT_TILES), 0)),
            pl.BlockSpec((TM, LANES), lambda i: (jnp.where(i < NT_CTX, 0, 1 + (i - NT_CTX) % LAT_TILES), 0)),
            pl.BlockSpec((2, D_MODEL), lambda i: (0, 0)),
            pl.BlockSpec((1, D_MODEL), lambda i: (0, 0)),
            pl.BlockSpec((1, D_MODEL), lambda i: (0, 0)),
        ],
        out_specs=[
            pl.BlockSpec(head_blk, lambda i: (i, 0, 0, 0)),
            pl.BlockSpec(head_blk, lambda i: (i, 0, 0, 0)),
            pl.BlockSpec(head_blk, lambda i: (i, 0, 0, 0)),
            pl.BlockSpec(head_blk, lambda i: (_ctx_idx(i), 0, 0, 0)),
            pl.BlockSpec(head_blk, lambda i: (_ctx_idx(i), 0, 0, 0)),
            tok_spec, tok_spec, tok_spec, tok_spec,
        ],
        out_shape=[
            jax.ShapeDtypeStruct((NT, N_HEADS, TM, HEAD_W), BF16),
            jax.ShapeDtypeStruct((NT, N_HEADS, TM, HEAD_W), BF16),
            jax.ShapeDtypeStruct((NT, N_HEADS, TM, HEAD_W), BF16),
            jax.ShapeDtypeStruct((NT_CTX, N_HEADS, TM, HEAD_W), F32),
            jax.ShapeDtypeStruct((NT_CTX, N_HEADS, TM, HEAD_W), F32),
            jax.ShapeDtypeStruct((T_ALL, D_MODEL), BF16),
            jax.ShapeDtypeStruct((T_ALL, D_MODEL), BF16),
            jax.ShapeDtypeStruct((T_ALL, D_MODEL), BF16),
            jax.ShapeDtypeStruct((T_ALL, D_MODEL), BF16),
        ],
        compiler_params=_params(("arbitrary",)),
        name="input_projection",
    )(xp, xs, mod3, w_in, cos_t, sin_t, b_gate, ln_g, ln_b)


def _attn_kernel(n_main, has_ctx, lam_ref, q_ref, k_ref, v_ref, *rest):
    if has_ctx:
        ck_ref, cv_ref, sub_ref, o_ref, kall, vt_all = rest
    else:
        sub_ref, o_ref, kall, vt_all = rest
    n_heads, n_keys = kall.shape[0], kall.shape[1]

    @pl.when(pl.program_id(2) == 0)
    def _():
        r = lax.broadcasted_iota(jnp.int32, (ONES_ROWS, n_keys), 0)
        for j in range(n_heads):
            for c in range(n_main):
                kall[j, c * TM:(c + 1) * TM, :] = k_ref[c, j]
                vt_all[j, 0:HEAD_W, c * TM:(c + 1) * TM] = v_ref[c, j].astype(F32).T.astype(BF16)
            if has_ctx:
                kall[j, n_main * TM:n_keys, :] = ck_ref[0, j].astype(BF16)
                vt_all[j, 0:HEAD_W, n_main * TM:n_keys] = cv_ref[0, j].T.astype(BF16)
            vt_all[j, HEAD_W:HEAD_W + ONES_ROWS, :] = jnp.where(r == 0, 1.0, 0.0).astype(BF16)

    tq = q_ref.shape[0] * TM
    lane = lax.broadcasted_iota(jnp.int32, (tq, HEAD_W), 1)
    qqs = []
    for j in range(n_heads):
        q = q_ref[:, j].reshape(tq, HEAD_W)
        zero = jnp.zeros_like(q)
        qqs.append(jnp.concatenate([jnp.where(lane < HEAD_DIM, q, zero),
                                    jnp.where(lane >= HEAD_DIM, q, zero)], axis=0))

    def scores(j, st, sz):
        return lax.dot_general(kall[j, st:st + sz, :], qqs[j], (((1,), (1,)), ((), ())),
                               preferred_element_type=F32)

    chunks = [(st, min(KEY_CHUNK, n_keys - st)) for st in range(0, n_keys, KEY_CHUNK)]
    pending = [[scores(j, *ch) for ch in chunks[:SCORE_LEAD]] for j in range(n_heads)]
    m = [None] * n_heads
    acc = [None] * n_heads
    for c, (st, sz) in enumerate(chunks):
        for j in range(n_heads):
            if c + SCORE_LEAD < len(chunks):
                pending[j].append(scores(j, *chunks[c + SCORE_LEAD]))
            s = pending[j].pop(0)
            m_c = jnp.max(s, axis=0, keepdims=True)
            m_new = m_c if c == 0 else jnp.maximum(m[j], m_c)
            e = jnp.exp2(s - m_new).astype(BF16)
            pv = jnp.dot(vt_all[j, :, st:st + sz], e, preferred_element_type=F32)
            acc[j] = pv if c == 0 else jnp.exp2(m[j] - m_new) * acc[j] + pv
            m[j] = m_new
    for j in range(n_heads):
        l = acc[j][HEAD_W:HEAD_W + 1, :]
        c1 = 1.0 / l[:, :tq]
        c2 = lam_ref[0] / l[:, tq:]
        a = (acc[j][:HEAD_W, :tq] * c1 - acc[j][:HEAD_W, tq:] * c2).T
        ms = jnp.mean(a * a, axis=-1, keepdims=True)
        o_ref[:, j * HEAD_W:(j + 1) * HEAD_W] = (a * lax.rsqrt(ms + LN_EPS) * sub_ref[...]).astype(BF16)


def _attention(lam, q_all, k_all, v_all, ctx_k, ctx_v, subw, latent):
    if latent:
        n_b, n_q, n_main, base, qt = N_LAT_B, LAT_TILES // Q_TILES, LAT_TILES, NT_CTX, Q_TILES
        n_keys, hp = LAT_LEN + PAST_LEN, HEADS_PER_STEP
    else:
        n_b, n_q, n_main, base, qt = N_CTX_B, 1, 1, 0, 1
        n_keys, hp = CTX_LEN, N_HEADS
    q_spec = pl.BlockSpec((qt, hp, TM, HEAD_W), lambda b, h, t: (base // qt + b * n_q + t, h, 0, 0))
    kv_spec = pl.BlockSpec((n_main, hp, TM, HEAD_W), lambda b, h, t: (base // n_main + b, h, 0, 0))
    in_specs = [pl.BlockSpec(memory_space=pltpu.SMEM), q_spec, kv_spec, kv_spec]
    args = [lam, q_all, k_all, v_all]
    if latent:
        c_spec = pl.BlockSpec((1, hp, PAST_LEN, HEAD_W), lambda b, h, t: (b, h, 0, 0))
        in_specs += [c_spec, c_spec]
        args += [ctx_k, ctx_v]
    in_specs.append(pl.BlockSpec((1, HEAD_W), lambda b, h, t: (0, 0)))
    args.append(subw)
    return pl.pallas_call(
        functools.partial(_attn_kernel, n_main, latent),
        grid=(n_b, N_HEADS // hp, n_q),
        in_specs=in_specs,
        out_specs=pl.BlockSpec((qt * TM, hp * HEAD_W), lambda b, h, t: (b * n_q + t, h)),
        out_shape=jax.ShapeDtypeStruct((n_b * n_q * qt * TM, D_MODEL), BF16),
        scratch_shapes=[pltpu.VMEM((hp, n_keys, HEAD_W), BF16),
                        pltpu.VMEM((hp, HEAD_W + ONES_ROWS, n_keys), BF16)],
        compiler_params=_params(("arbitrary", "arbitrary", "arbitrary")),
        name="diff_attention_lat" if latent else "diff_attention_ctx",
    )(*args)


def _mix_kernel(ac_ref, al_ref, ug_ref, vg_ref, sa_ref, sb_ref, xp_ref, xs_ref, mod_ref,
                ws_ref, bs_ref, wpa_ref, wpb_ref, wo_ref, g_ref, b_ref, wrc_ref, br_ref,
                x1_ref, h2_ref, ti_ref, tg_ref, cnt_ref, run_ref):
    i = pl.program_id(0)
    is_ctx = i < NT_CTX
    a = jnp.where(is_ctx, ac_ref[...], al_ref[...])
    x = jnp.where(is_ctx, xp_ref[...], xs_ref[...])
    mod = mod_ref[0]
    g1 = mod[:, 2 * D_MODEL:3 * D_MODEL]
    sh2 = mod[:, 3 * D_MODEL:4 * D_MODEL]
    sc2 = mod[:, 4 * D_MODEL:5 * D_MODEL]

    gw = D_MODEL // GMLP_GROUPS
    chunks = []
    for c in range(TM // CHUNK):
        groups = []
        for g in range(GMLP_GROUPS):
            vc = vg_ref[c * CHUNK:(c + 1) * CHUNK, g * gw:(g + 1) * gw]
            groups.append(jnp.dot(ws_ref[g], vc, preferred_element_type=F32))
        chunks.append(jnp.concatenate(groups, axis=1) + bs_ref[...])
    sp = jnp.concatenate(chunks, axis=0)
    gm = (ug_ref[...].astype(F32) * sp).astype(BF16)

    pa = jnp.dot(a, wpa_ref[...], preferred_element_type=F32)
    pb = jnp.dot(gm, wpb_ref[...], preferred_element_type=F32)
    merged = (sa_ref[...].astype(F32) * pa + sb_ref[...].astype(F32) * pb).astype(BF16)
    mix = jnp.dot(merged, wo_ref[...], preferred_element_type=F32)
    x1 = _layer_norm(ALPHA * x + g1 * mix, g_ref[...], b_ref[...])
    x1_ref[...] = x1
    h2 = x1 * (1.0 + sc2) + sh2
    hi, lo = _split(h2)
    h2_ref[...] = _pack_pairs(h2)
    both = jnp.dot(hi, wrc_ref[...], preferred_element_type=F32)
    logits = (both[:, :LANES] + jnp.dot(lo, wrc_ref[:, :LANES], preferred_element_type=F32)
              + both[:, LANES:]) + br_ref[...]
    lane = lax.broadcasted_iota(jnp.int32, (TM, LANES), 1).astype(F32)
    vals, idxs = [], []
    for _ in range(TOP_K):
        mx = jnp.max(logits, axis=-1, keepdims=True)
        ix = jnp.min(jnp.where(logits == mx, lane, float(LANES)), axis=-1, keepdims=True)
        vals.append(mx)
        idxs.append(ix)
        logits = jnp.where(lane == ix, NEG_BIG * 2.0, logits)
    es = [jnp.exp(v - vals[0]) for v in vals]
    inv = 1.0 / (es[0] + es[1] + es[2] + es[3])

    @pl.when(i == 0)
    def _():
        run_ref[...] = jnp.zeros_like(run_ref)

    hot = [lane == ix for ix in idxs]
    memb = jnp.zeros((TM, LANES), F32)
    for hk in hot:
        memb = jnp.where(hk, 1.0, memb)
    row = lax.broadcasted_iota(jnp.int32, (TM, TM), 0)
    col = lax.broadcasted_iota(jnp.int32, (TM, TM), 1)
    before = jnp.where(row > col, 1.0, 0.0).astype(BF16)
    base = run_ref[...]
    rank_all = jnp.dot(before, memb.astype(BF16), preferred_element_type=F32) + base
    total = base + jnp.sum(memb, axis=0, keepdims=True)
    run_ref[...] = total
    cnt_ref[...] = total

    ti = jnp.zeros((TM, LANES), F32)
    tg = jnp.zeros((TM, LANES), F32)
    for k in range(TOP_K):
        rank_k = jnp.sum(jnp.where(hot[k], rank_all, 0.0), axis=-1, keepdims=True)
        ti = jnp.where(lane == float(k), idxs[k], ti)
        ti = jnp.where(lane == float(TOP_K + k), rank_k, ti)
        tg = jnp.where(lane == float(k), es[k] * inv, tg)
    ti_ref[...] = ti.astype(jnp.int32)
    tg_ref[...] = tg


def _mix(a_ctx, a_lat, ug, vg, sa, sb, xp, xs, mod3, ws, bs, wpa, wpb, wo, g, b, wrc, br):
    tile = lambda i: (i, 0)
    full2 = lambda i: (0, 0)
    tok = pl.BlockSpec((TM, D_MODEL), tile)
    ctx = pl.BlockSpec((TM, D_MODEL), lambda i: (_ctx_idx(i), 0))
    lat = pl.BlockSpec((TM, D_MODEL), lambda i: (_lat_idx(i), 0))
    wsq = pl.BlockSpec((D_MODEL, D_MODEL), full2)
    vec = pl.BlockSpec((1, D_MODEL), full2)
    return pl.pallas_call(
        _mix_kernel,
        grid=(NT,),
        in_specs=[ctx, lat, tok, tok, tok, tok, ctx, lat,
                  pl.BlockSpec((1, 1, 6 * D_MODEL), lambda i: (_mod_row(i), 0, 0)),
                  pl.BlockSpec((GMLP_GROUPS, CHUNK, CHUNK), lambda i: (0, 0, 0)),
                  pl.BlockSpec((CHUNK, D_MODEL), full2),
                  wsq, wsq, wsq, vec, vec,
                  pl.BlockSpec((D_MODEL, 2 * LANES), full2),
                  pl.BlockSpec((1, LANES), full2)],
        out_specs=[tok, pl.BlockSpec((TM, PACK_W), tile),
                   pl.BlockSpec((TM, LANES), tile), pl.BlockSpec((TM, LANES), tile),
                   pl.BlockSpec((1, LANES), full2)],
        out_shape=[jax.ShapeDtypeStruct((T_ALL, D_MODEL), F32),
                   jax.ShapeDtypeStruct((T_ALL, PACK_W), jnp.uint32),
                   jax.ShapeDtypeStruct((T_ALL, LANES), jnp.int32),
                   jax.ShapeDtypeStruct((T_ALL, LANES), F32),
                   jax.ShapeDtypeStruct((1, LANES), F32)],
        scratch_shapes=[pltpu.VMEM((1, LANES), F32)],
        compiler_params=_params(("arbitrary",)),
        name="mix_ln1_router",
    )(a_ctx, a_lat, ug, vg, sa, sb, xp, xs, mod3, ws, bs, wpa, wpb, wo, g, b, wrc, br)


def _dispatch_kernel(dest_ref, h_ref, init_hbm, out_hbm, sem):
    del init_hbm

    def issue(g, carry):
        base = g * (SUBLANES * TOP_K)
        for u in range(SUBLANES):
            for k in range(TOP_K):
                dest = dest_ref[base + (u * TOP_K + k)]
                pltpu.make_async_copy(h_ref.at[g, pl.ds(u, 1)], out_hbm.at[pl.ds(dest, 1)],
                                      sem).start(priority=k % 2)
        return carry

    lax.fori_loop(0, TM // SUBLANES, issue, 0)
    for _ in range(TOP_K):
        pltpu.make_async_copy(out_hbm.at[pl.ds(0, TM)], out_hbm.at[pl.ds(0, TM)], sem).wait()


def _dispatch(dest_flat, h2p, n_rows):
    return pl.pallas_call(
        _dispatch_kernel,
        grid=(NT,),
        in_specs=[pl.BlockSpec((ROWS_PER_TILE,), lambda i: (i,), memory_space=pltpu.SMEM),
                  pl.BlockSpec((TM // SUBLANES, SUBLANES, PACK_W), lambda i: (i, 0, 0)),
                  pl.BlockSpec(memory_space=pl.ANY)],
        out_specs=pl.BlockSpec(memory_space=pl.ANY),
        out_shape=jax.ShapeDtypeStruct((n_rows, PACK_W), jnp.uint32),
        scratch_shapes=[pltpu.SemaphoreType.DMA(())],
        input_output_aliases={2: 0},
        compiler_params=_params(("arbitrary",)),
        name="moe_dispatch",
    )(dest_flat, h2p.reshape(T_ALL // SUBLANES, SUBLANES, PACK_W),
      jnp.zeros((n_rows, PACK_W), jnp.uint32))


def _expert_kernel(be_ref, nu_ref, x_ref, wu_ref, bu_ref, wd_ref, bd_ref, y_ref, wu_bf, wd_bf):
    i = pl.program_id(0)
    used = i < nu_ref[0]
    new_expert = jnp.logical_or(i == 0, be_ref[i] != be_ref[jnp.maximum(i - 1, 0)])

    @pl.when(jnp.logical_and(used, new_expert))
    def _():
        wu_bf[...] = wu_ref[0].astype(BF16)
        wd_bf[...] = wd_ref[0].astype(BF16)

    @pl.when(used)
    def _():
        x = _unpack_pairs(x_ref[...]).astype(BF16)
        hu = jnp.dot(x, wu_bf[...], preferred_element_type=F32) + bu_ref[0]
        glu = jnp.minimum(hu[:, :D_EXPERT], SWIGLU_LIMIT)
        lin = jnp.clip(hu[:, D_EXPERT:], -SWIGLU_LIMIT, SWIGLU_LIMIT)
        act = glu * jax.nn.sigmoid(SWIGLU_ALPHA * glu) * (lin + 1.0)
        y = jnp.dot(act.astype(BF16), wd_bf[...], preferred_element_type=F32) + bd_ref[0]
        y_ref[...] = _pack_pairs(y)

    @pl.when(jnp.logical_not(used))
    def _():
        y_ref[...] = jnp.zeros_like(y_ref)


def _experts(blk_exp, n_used, xb, w_up, b_up, w_down, b_down):
    n_blocks = xb.shape[0] // MOE_BLOCK
    grid_spec = pltpu.PrefetchScalarGridSpec(
        num_scalar_prefetch=2,
        grid=(n_blocks,),
        in_specs=[pl.BlockSpec((MOE_BLOCK, PACK_W), lambda i, be, nu: (i, 0)),
                  pl.BlockSpec((1, D_MODEL, 2 * D_EXPERT), lambda i, be, nu: (be[i], 0, 0)),
                  pl.BlockSpec((1, 1, 2 * D_EXPERT), lambda i, be, nu: (be[i], 0, 0)),
                  pl.BlockSpec((1, D_EXPERT, D_MODEL), lambda i, be, nu: (be[i], 0, 0)),
                  pl.BlockSpec((1, 1, D_MODEL), lambda i, be, nu: (be[i], 0, 0))],
        out_specs=pl.BlockSpec((MOE_BLOCK, PACK_W), lambda i, be, nu: (i, 0)),
        scratch_shapes=[pltpu.VMEM((D_MODEL, 2 * D_EXPERT), BF16),
                        pltpu.VMEM((D_EXPERT, D_MODEL), BF16)],
    )
    return pl.pallas_call(
        _expert_kernel,
        grid_spec=grid_spec,
        out_shape=jax.ShapeDtypeStruct((n_blocks * MOE_BLOCK, PACK_W), jnp.uint32),
        compiler_params=_params(("arbitrary",)),
        name="expert_ffn",
    )(blk_exp, n_used, xb, w_up, b_up, w_down, b_down)


def _final_kernel(dcur_ref, dnxt_ref, x1_ref, tg_ref, mod_ref, g_ref, b_ref, yb_hbm,
                  yp_ref, ys_ref, ybuf, sem):
    i = pl.program_id(0)

    def gather(dest_ref, slot):
        def issue(g, carry):
            base = g * (SUBLANES * TOP_K)
            for u in range(SUBLANES):
                for k in range(TOP_K):
                    dest = dest_ref[base + (u * TOP_K + k)]
                    pltpu.make_async_copy(yb_hbm.at[pl.ds(dest, 1)],
                                          ybuf.at[slot, k, g, pl.ds(u, 1)],
                                          sem.at[slot]).start(priority=k % 2)
            return carry
        lax.fori_loop(0, TM // SUBLANES, issue, 0)

    def wait_rows(slot):
        for k in range(TOP_K):
            pltpu.make_async_copy(yb_hbm.at[pl.ds(0, TM)], yb_hbm.at[pl.ds(0, TM)],
                                  sem.at[slot]).wait()

    def combine(slot):
        g2 = mod_ref[0][:, 5 * D_MODEL:6 * D_MODEL]
        gates = tg_ref[...]
        ffn = jnp.zeros((TM, D_MODEL), F32)
        for k in range(TOP_K):
            yk = ybuf[slot, k].reshape(TM, PACK_W)
            ffn = ffn + gates[:, k:k + 1] * _unpack_pairs(yk)
        y = _layer_norm(ALPHA * x1_ref[...] + g2 * ffn, g_ref[...], b_ref[...])

        @pl.when(i < NT_CTX)
        def _():
            yp_ref[...] = y

        @pl.when(i >= NT_CTX)
        def _():
            ys_ref[...] = y

    @pl.when(i == 0)
    def _():
        gather(dcur_ref, 0)

    for slot in range(2):
        @pl.when(i % 2 == slot)
        def _():
            @pl.when(i + 1 < NT)
            def _():
                gather(dnxt_ref, 1 - slot)
            wait_rows(slot)
            combine(slot)


def _final(dest_flat, x1, top_g, mod3, g, b, ybp):
    tok = pl.BlockSpec((TM, D_MODEL), lambda i: (i, 0))
    vec = pl.BlockSpec((1, D_MODEL), lambda i: (0, 0))
    return pl.pallas_call(
        _final_kernel,
        grid=(NT,),
        in_specs=[pl.BlockSpec((ROWS_PER_TILE,), lambda i: (i,), memory_space=pltpu.SMEM),
                  pl.BlockSpec((ROWS_PER_TILE,), lambda i: (jnp.minimum(i + 1, NT - 1),),
                               memory_space=pltpu.SMEM),
                  tok, pl.BlockSpec((TM, LANES), lambda i: (i, 0)),
                  pl.BlockSpec((1, 1, 6 * D_MODEL), lambda i: (_mod_row(i), 0, 0)), vec, vec,
                  pl.BlockSpec(memory_space=pl.ANY)],
        scratch_shapes=[pltpu.VMEM((2, TOP_K, TM // SUBLANES, SUBLANES, PACK_W), jnp.uint32),
                        pltpu.SemaphoreType.DMA((2,))],
        out_specs=[pl.BlockSpec((TM, D_MODEL), lambda i: (_ctx_idx(i), 0)),
                   pl.BlockSpec((TM, D_MODEL), lambda i: (_lat_idx(i), 0))],
        out_shape=[jax.ShapeDtypeStruct((NT_CTX * TM, D_MODEL), F32),
                   jax.ShapeDtypeStruct((NT_LAT * TM, D_MODEL), F32)],
        compiler_params=_params(("arbitrary",)),
        name="residual_ln2",
    )(dest_flat, dest_flat, x1, top_g, mod3, g, b, ybp)


def _rope_tables():
    t = jnp.arange(LAT_LEN)
    r = (t // GRID_W).astype(F32)
    col = (t % GRID_W).astype(F32)
    inv = jnp.power(ROPE_BASE, -jnp.arange(ROPE_HALF, dtype=F32) / ROPE_HALF)
    ang_r = r[:, None] * inv
    ang_c = col[:, None] * inv
    ang = jnp.concatenate([ang_r, ang_r, ang_c, ang_c], axis=-1)
    cos = jnp.tile(jnp.cos(ang), (1, 2))
    sin = jnp.tile(jnp.sin(ang), (1, 2))
    sign = jnp.where((jnp.arange(LANES) % (2 * ROPE_HALF)) < ROPE_HALF, -1.0, 1.0).astype(F32)
    cos = jnp.concatenate([jnp.ones((TM, LANES), F32), cos], axis=0)
    sin = jnp.concatenate([jnp.zeros((TM, LANES), F32), sin * sign], axis=0)
    return cos, sin


def _routing(top_idx, rank, counts):
    n_tok = top_idx.shape[0]
    n_assign = n_tok * TOP_K
    padded = (counts + MOE_BLOCK - 1) // MOE_BLOCK * MOE_BLOCK
    pend = jnp.cumsum(padded)
    pstart = pend - padded
    dest_tk = pstart[top_idx] + rank
    n_blocks = -(-n_assign // MOE_BLOCK) + N_EXPERTS
    blk_start = jnp.arange(n_blocks, dtype=jnp.int32) * MOE_BLOCK
    blk_exp = jnp.minimum(jnp.sum((pend[None, :] <= blk_start[:, None]).astype(jnp.int32), axis=1),
                          N_EXPERTS - 1)
    n_used = (pend[-1] // MOE_BLOCK).astype(jnp.int32).reshape(1)
    return blk_exp, dest_tk.reshape(n_assign), n_used, n_blocks * MOE_BLOCK


def kernel(x_prompt, x_sample, c, cache_k, cache_v, c_ctx, w_ada, b_ada, w_in, lambda_q1, lambda_k1, lambda_q2, lambda_k2, subln_w, gmlp_ln_g, gmlp_ln_b, w_spatial, b_spatial, b_gate, w_pa, w_pb, w_o, ln1_g, ln1_b, w_router, b_router, w_up, b_up, w_down, b_down, ln2_g, ln2_b):
    l = 0
    xp = x_prompt.reshape(N_CTX_B * CTX_LEN, D_MODEL)
    xs = x_sample.reshape(N_LAT_B * LAT_LEN, D_MODEL)

    cond = jnp.concatenate([c_ctx[None, :], c, jnp.zeros((16 - 1 - N_LAT_B, D_MODEL), F32)], axis=0)
    mod3 = _modulation(cond, w_ada[l], b_ada[l][None, :]).reshape(16, 1, 6 * D_MODEL)

    cos_t, sin_t = _rope_tables()
    (q_all, k_all, v_all, new_k, new_v, ug, vg, sa, sb) = _input_projection(
        xp, xs, mod3, w_in[l].astype(BF16), cos_t, sin_t, b_gate[l],
        gmlp_ln_g[l][None, :], gmlp_ln_b[l][None, :])

    lam = (jnp.exp(jnp.sum(lambda_q1[l] * lambda_k1[l])) - jnp.exp(jnp.sum(lambda_q2[l] * lambda_k2[l]))
           + LAM_INIT).reshape(1).astype(F32)
    subw = (subln_w[l] * (1.0 - LAM_INIT))[None, :]
    ctx_k = cache_k[:, l].reshape(N_LAT_B, N_HEADS, PAST_LEN, HEAD_W)
    ctx_v = cache_v[:, l]
    a_ctx = _attention(lam, q_all, k_all, v_all, None, None, subw, latent=False)
    a_lat = _attention(lam, q_all, k_all, v_all, ctx_k, ctx_v, subw, latent=True)

    bs_full = jnp.repeat(b_spatial[l].T, D_MODEL // GMLP_GROUPS, axis=1)
    wr = jnp.pad(w_router[l], ((0, 0), (0, LANES - N_EXPERTS)))
    wr_hi = wr.astype(BF16)
    wr_lo = (wr - wr_hi.astype(F32)).astype(BF16)
    br = jnp.concatenate([b_router[l], jnp.full((LANES - N_EXPERTS,), NEG_BIG, F32)])[None, :]
    x1, h2, top_i, top_g, cnt = _mix(
        a_ctx, a_lat, ug, vg, sa, sb, xp, xs, mod3, w_spatial[l].astype(BF16), bs_full,
        w_pa[l].astype(BF16), w_pb[l].astype(BF16), w_o[l].astype(BF16),
        ln1_g[l][None, :], ln1_b[l][None, :], jnp.concatenate([wr_hi, wr_lo], axis=1), br)

    blk_exp, dest_flat, n_used, n_rows = _routing(
        top_i[:, :TOP_K], top_i[:, TOP_K:2 * TOP_K], cnt[0, :N_EXPERTS].astype(jnp.int32))
    xb = _dispatch(dest_flat, h2, n_rows)
    yb = _experts(blk_exp, n_used, xb, w_up[l], b_up[l][:, None, :], w_down[l], b_down[l][:, None, :])
    y_prompt, y_sample = _final(dest_flat, x1, top_g, mod3, ln2_g[l][None, :], ln2_b[l][None, :], yb)
    return (y_prompt.reshape(N_CTX_B, CTX_LEN, D_MODEL),
            y_sample.reshape(N_LAT_B, LAT_LEN, D_MODEL),
            new_k.reshape(N_CTX_B, DEPTH, N_HEADS, CTX_LEN, 2, HEAD_DIM),
            new_v.reshape(N_CTX_B, DEPTH, N_HEADS, CTX_LEN, HEAD_W))
```

```python
import functools
import math

import jax
import jax.numpy as jnp
from jax import lax
from jax.experimental import pallas as pl
from jax.experimental.pallas import tpu as pltpu

F32 = jnp.float32
BF16 = jnp.bfloat16

D_MODEL = 1024
N_CTX_B = 16
CTX_LEN = 256
N_LAT_B = 8
LAT_LEN = 4096
PAST_LEN = 256
GRID_W = 64
N_HEADS = 8
HEAD_DIM = 64
HEAD_W = 2 * HEAD_DIM
ROPE_HALF = HEAD_DIM // 4
ROPE_BASE = 10000.0
GMLP_GROUPS = 4
CHUNK = 128
N_EXPERTS = 32
TOP_K = 4
D_EXPERT = 1024
SWIGLU_LIMIT = 7.0
SWIGLU_ALPHA = 1.702
MOE_BLOCK = 512
LN_EPS = 1e-5
DEPTH = 1
N_SEG = 7

TM = 256
NT_CTX = N_CTX_B * CTX_LEN // TM
NT_LAT = N_LAT_B * LAT_LEN // TM
NT = NT_CTX + NT_LAT
T_ALL = NT * TM
LAT_TILES = LAT_LEN // TM
LANES = 128
KEY_CHUNK = 512
SCORE_LEAD = 1
Q_TILES = 2
HEADS_PER_STEP = 2
ONES_ROWS = 16
PACK_W = D_MODEL // 2
ROWS_PER_TILE = TM * TOP_K
SUBLANES = 8
NEG_BIG = -1e30
VMEM_LIMIT = 56 * 1024 * 1024

ALPHA = (2.0 * DEPTH) ** 0.25
LAM_INIT = 0.8 - 0.6 * math.exp(-0.3 * 0)
Q_SCALE = HEAD_DIM ** -0.5 * math.log2(math.e)


def _ctx_idx(i):
    return jnp.minimum(i, NT_CTX - 1)


def _lat_idx(i):
    return jnp.maximum(i - NT_CTX, 0)


def _mod_row(i):
    return jnp.where(i < NT_CTX, 0, 1 + (i - NT_CTX) // LAT_TILES)


def _layer_norm(x, g, b):
    mu = jnp.mean(x, axis=-1, keepdims=True)
    xc = x - mu
    var = jnp.mean(xc * xc, axis=-1, keepdims=True)
    return xc * lax.rsqrt(var + LN_EPS) * g + b


def _gelu(x):
    return 0.5 * x * (1.0 + lax.erf(x * (1.0 / math.sqrt(2.0))))


def _split(x):
    hi = x.astype(BF16)
    return hi, (x - hi.astype(F32)).astype(BF16)


def _pack_pairs(x):
    w = x.shape[1] // 2
    a = pltpu.bitcast(x[:, :w].astype(BF16).astype(F32), jnp.uint32)
    b = pltpu.bitcast(x[:, w:].astype(BF16).astype(F32), jnp.uint32)
    return lax.shift_right_logical(a, jnp.uint32(16)) | b


def _unpack_pairs(p):
    a = pltpu.bitcast(lax.shift_left(p, jnp.uint32(16)), F32)
    b = pltpu.bitcast(p & jnp.uint32(0xFFFF0000), F32)
    return jnp.concatenate([a, b], axis=1)


def _params(sem):
    return pltpu.CompilerParams(dimension_semantics=sem, vmem_limit_bytes=VMEM_LIMIT)


def _mod_kernel(c_ref, w_ref, b_ref, o_ref):
    c = c_ref[...]
    s = c * jax.nn.sigmoid(c)
    s_hi, s_lo = _split(s)
    w_hi, w_lo = _split(w_ref[...])
    o_ref[...] = (jnp.dot(s_hi, w_hi, preferred_element_type=F32)
                  + jnp.dot(s_lo, w_hi, preferred_element_type=F32)
                  + jnp.dot(s_hi, w_lo, preferred_element_type=F32)) + b_ref[...]


def _modulation(cond, w_ada, b_ada):
    n = cond.shape[0]
    return pl.pallas_call(
        _mod_kernel,
        grid=(6,),
        in_specs=[pl.BlockSpec((n, D_MODEL), lambda j: (0, 0)),
                  pl.BlockSpec((D_MODEL, D_MODEL), lambda j: (0, j)),
                  pl.BlockSpec((1, D_MODEL), lambda j: (0, j))],
        out_specs=pl.BlockSpec((n, D_MODEL), lambda j: (0, j)),
        out_shape=jax.ShapeDtypeStruct((n, 6 * D_MODEL), F32),
        compiler_params=_params(("arbitrary",)),
        name="adaln_mod",
    )(cond, w_ada, b_ada)


def _proj_kernel(xp_ref, xs_ref, mod_ref, w_ref, cos_ref, sin_ref, bg_ref, lg_ref, lb_ref,
                 q_ref, k_ref, v_ref, nk_ref, nv_ref, ug_ref, vg_ref, sa_ref, sb_ref):
    i = pl.program_id(0)
    is_ctx = i < NT_CTX
    x = jnp.where(is_ctx, xp_ref[...], xs_ref[...])
    mod = mod_ref[0]
    sh1 = mod[:, 0:D_MODEL]
    sc1 = mod[:, D_MODEL:2 * D_MODEL]
    h = (x * (1.0 + sc1) + sh1).astype(BF16)

    def seg(s):
        return jnp.dot(h, w_ref[:, s * D_MODEL:(s + 1) * D_MODEL], preferred_element_type=F32)

    cos = cos_ref[...]
    sin = sin_ref[...]
    lane = lax.broadcasted_iota(jnp.int32, (TM, LANES), 1)
    first = (lane % (2 * ROPE_HALF)) < ROPE_HALF

    def rope(zh):
        up = pltpu.roll(zh, LANES - ROPE_HALF, 1)
        dn = pltpu.roll(zh, ROPE_HALF, 1)
        return zh * cos + jnp.where(first, up, dn) * sin

    zq = seg(0)
    for hd in range(N_HEADS):
        q_ref[0, hd] = (rope(zq[:, hd * HEAD_W:(hd + 1) * HEAD_W]) * Q_SCALE).astype(BF16)
    zk = seg(1)
    for hd in range(N_HEADS):
        k_ref[0, hd] = rope(zk[:, hd * HEAD_W:(hd + 1) * HEAD_W]).astype(BF16)
    zv = seg(2)
    for hd in range(N_HEADS):
        v_ref[0, hd] = zv[:, hd * HEAD_W:(hd + 1) * HEAD_W].astype(BF16)

    @pl.when(is_ctx)
    def _():
        for hd in range(N_HEADS):
            nk_ref[0, hd] = zk[:, hd * HEAD_W:(hd + 1) * HEAD_W]
            nv_ref[0, hd] = zv[:, hd * HEAD_W:(hd + 1) * HEAD_W]

    ug_ref[...] = _gelu(seg(3)).astype(BF16)
    vg_ref[...] = _layer_norm(_gelu(seg(4)), lg_ref[...], lb_ref[...]).astype(BF16)
    sa_ref[...] = jax.nn.sigmoid(seg(5) + bg_ref[0:1, :]).astype(BF16)
    sb_ref[...] = jax.nn.sigmoid(seg(6) + bg_ref[1:2, :]).astype(BF16)


def _input_projection(xp, xs, mod3, w_in, cos_t, sin_t, b_gate, ln_g, ln_b):
    tile = lambda i: (i, 0)
    head_blk = (1, N_HEADS, TM, HEAD_W)
    tok_spec = pl.BlockSpec((TM, D_MODEL), tile)
    return pl.pallas_call(
        _proj_kernel,
        grid=(NT,),
        in_specs=[
            pl.BlockSpec((TM, D_MODEL), lambda i: (_ctx_idx(i), 0)),
            pl.BlockSpec((TM, D_MODEL), lambda i: (_lat_idx(i), 0)),
            pl.BlockSpec((1, 1, 6 * D_MODEL), lambda i: (_mod_row(i), 0, 0)),
            pl.BlockSpec((D_MODEL, N_SEG * D_MODEL), lambda i: (0, 0)),
            pl.BlockSpec((TM, LANES), lambda i: (jnp.where(i < NT_CTX, 0, 1 + (i - NT_CTX) % LAT_TILES), 0)),
            pl.BlockSpec((TM, LANES), lambda i: (jnp.where(i < NT_CTX, 0, 1 + (i - NT_CTX) % LAT_TILES), 0)),
            pl.BlockSpec((2, D_MODEL), lambda i: (0, 0)),
            pl.BlockSpec((1, D_MODEL), lambda i: (0, 0)),
            pl.BlockSpec((1, D_MODEL), lambda i: (0, 0)),
        ],
        out_specs=[
            pl.BlockSpec(head_blk, lambda i: (i, 0, 0, 0)),
            pl.BlockSpec(head_blk, lambda i: (i, 0, 0, 0)),
            pl.BlockSpec(head_blk, lambda i: (i, 0, 0, 0)),
            pl.BlockSpec(head_blk, lambda i: (_ctx_idx(i), 0, 0, 0)),
            pl.BlockSpec(head_blk, lambda i: (_ctx_idx(i), 0, 0, 0)),
            tok_spec, tok_spec, tok_spec, tok_spec,
        ],
        out_shape=[
            jax.ShapeDtypeStruct((NT, N_HEADS, TM, HEAD_W), BF16),
            jax.ShapeDtypeStruct((NT, N_HEADS, TM, HEAD_W), BF16),
            jax.ShapeDtypeStruct((NT, N_HEADS, TM, HEAD_W), BF16),
            jax.ShapeDtypeStruct((NT_CTX, N_HEADS, TM, HEAD_W), F32),
            jax.ShapeDtypeStruct((NT_CTX, N_HEADS, TM, HEAD_W), F32),
            jax.ShapeDtypeStruct((T_ALL, D_MODEL), BF16),
            jax.ShapeDtypeStruct((T_ALL, D_MODEL), BF16),
            jax.ShapeDtypeStruct((T_ALL, D_MODEL), BF16),
            jax.ShapeDtypeStruct((T_ALL, D_MODEL), BF16),
        ],
        compiler_params=_params(("arbitrary",)),
        name="input_projection",
    )(xp, xs, mod3, w_in, cos_t, sin_t, b_gate, ln_g, ln_b)


def _attn_kernel(n_main, has_ctx, lam_ref, q_ref, k_ref, v_ref, *rest):
    if has_ctx:
        ck_ref, cv_ref, sub_ref, o_ref, kall, vt_all = rest
    else:
        sub_ref, o_ref, kall, vt_all = rest
    n_heads, n_keys = kall.shape[0], kall.shape[1]

    @pl.when(pl.program_id(2) == 0)
    def _():
        r = lax.broadcasted_iota(jnp.int32, (ONES_ROWS, n_keys), 0)
        for j in range(n_heads):
            for c in range(n_main):
                kall[j, c * TM:(c + 1) * TM, :] = k_ref[c, j]
                vt_all[j, 0:HEAD_W, c * TM:(c + 1) * TM] = v_ref[c, j].astype(F32).T.astype(BF16)
            if has_ctx:
                kall[j, n_main * TM:n_keys, :] = ck_ref[0, j].astype(BF16)
                vt_all[j, 0:HEAD_W, n_main * TM:n_keys] = cv_ref[0, j].T.astype(BF16)
            vt_all[j, HEAD_W:HEAD_W + ONES_ROWS, :] = jnp.where(r == 0, 1.0, 0.0).astype(BF16)

    tq = q_ref.shape[0] * TM
    lane = lax.broadcasted_iota(jnp.int32, (tq, HEAD_W), 1)
    qqs = []
    for j in range(n_heads):
        q = q_ref[:, j].reshape(tq, HEAD_W)
        zero = jnp.zeros_like(q)
        qqs.append(jnp.concatenate([jnp.where(lane < HEAD_DIM, q, zero),
                                    jnp.where(lane >= HEAD_DIM, q, zero)], axis=0))

    def scores(j, st, sz):
        return lax.dot_general(kall[j, st:st + sz, :], qqs[j], (((1,), (1,)), ((), ())),
                               preferred_element_type=F32)

    chunks = [(st, min(KEY_CHUNK, n_keys - st)) for st in range(0, n_keys, KEY_CHUNK)]
    pending = [[scores(j, *ch) for ch in chunks[:SCORE_LEAD]] for j in range(n_heads)]
    m = [None] * n_heads
    acc = [None] * n_heads
    for c, (st, sz) in enumerate(chunks):
        for j in range(n_heads):
            if c + SCORE_LEAD < len(chunks):
                pending[j].append(scores(j, *chunks[c + SCORE_LEAD]))
            s = pending[j].pop(0)
            m_c = jnp.max(s, axis=0, keepdims=True)
            m_new = m_c if c == 0 else jnp.maximum(m[j], m_c)
            e = jnp.exp2(s - m_new).astype(BF16)
            pv = jnp.dot(vt_all[j, :, st:st + sz], e, preferred_element_type=F32)
            acc[j] = pv if c == 0 else jnp.exp2(m[j] - m_new) * acc[j] + pv
            m[j] = m_new
    for j in range(n_heads):
        l = acc[j][HEAD_W:HEAD_W + 1, :]
        c1 = 1.0 / l[:, :tq]
        c2 = lam_ref[0] / l[:, tq:]
        a = (acc[j][:HEAD_W, :tq] * c1 - acc[j][:HEAD_W, tq:] * c2).T
        ms = jnp.mean(a * a, axis=-1, keepdims=True)
        o_ref[:, j * HEAD_W:(j + 1) * HEAD_W] = (a * lax.rsqrt(ms + LN_EPS) * sub_ref[...]).astype(BF16)


def _attention(lam, q_all, k_all, v_all, ctx_k, ctx_v, subw, latent):
    if latent:
        n_b, n_q, n_main, base, qt = N_LAT_B, LAT_TILES // Q_TILES, LAT_TILES, NT_CTX, Q_TILES
        n_keys, hp = LAT_LEN + PAST_LEN, HEADS_PER_STEP
    else:
        n_b, n_q, n_main, base, qt = N_CTX_B, 1, 1, 0, 1
        n_keys, hp = CTX_LEN, N_HEADS
    q_spec = pl.BlockSpec((qt, hp, TM, HEAD_W), lambda b, h, t: (base // qt + b * n_q + t, h, 0, 0))
    kv_spec = pl.BlockSpec((n_main, hp, TM, HEAD_W), lambda b, h, t: (base // n_main + b, h, 0, 0))
    in_specs = [pl.BlockSpec(memory_space=pltpu.SMEM), q_spec, kv_spec, kv_spec]
    args = [lam, q_all, k_all, v_all]
    if latent:
        c_spec = pl.BlockSpec((1, hp, PAST_LEN, HEAD_W), lambda b, h, t: (b, h, 0, 0))
        in_specs += [c_spec, c_spec]
        args += [ctx_k, ctx_v]
    in_specs.append(pl.BlockSpec((1, HEAD_W), lambda b, h, t: (0, 0)))
    args.append(subw)
    return pl.pallas_call(
        functools.partial(_attn_kernel, n_main, latent),
        grid=(n_b, N_HEADS // hp, n_q),
        in_specs=in_specs,
        out_specs=pl.BlockSpec((qt * TM, hp * HEAD_W), lambda b, h, t: (b * n_q + t, h)),
        out_shape=jax.ShapeDtypeStruct((n_b * n_q * qt * TM, D_MODEL), BF16),
        scratch_shapes=[pltpu.VMEM((hp, n_keys, HEAD_W), BF16),
                        pltpu.VMEM((hp, HEAD_W + ONES_ROWS, n_keys), BF16)],
        compiler_params=_params(("arbitrary", "arbitrary", "arbitrary")),
        name="diff_attention_lat" if latent else "diff_attention_ctx",
    )(*args)


def _mix_kernel(ac_ref, al_ref, ug_ref, vg_ref, sa_ref, sb_ref, xp_ref, xs_ref, mod_ref,
                ws_ref, bs_ref, wpa_ref, wpb_ref, wo_ref, g_ref, b_ref, wrc_ref, br_ref,
                x1_ref, h2_ref, ti_ref, tg_ref, cnt_ref, run_ref):
    i = pl.program_id(0)
    is_ctx = i < NT_CTX
    a = jnp.where(is_ctx, ac_ref[...], al_ref[...])
    x = jnp.where(is_ctx, xp_ref[...], xs_ref[...])
    mod = mod_ref[0]
    g1 = mod[:, 2 * D_MODEL:3 * D_MODEL]
    sh2 = mod[:, 3 * D_MODEL:4 * D_MODEL]
    sc2 = mod[:, 4 * D_MODEL:5 * D_MODEL]

    gw = D_MODEL // GMLP_GROUPS
    chunks = []
    for c in range(TM // CHUNK):
        groups = []
        for g in range(GMLP_GROUPS):
            vc = vg_ref[c * CHUNK:(c + 1) * CHUNK, g * gw:(g + 1) * gw]
            groups.append(jnp.dot(ws_ref[g], vc, preferred_element_type=F32))
        chunks.append(jnp.concatenate(groups, axis=1) + bs_ref[...])
    sp = jnp.concatenate(chunks, axis=0)
    gm = (ug_ref[...].astype(F32) * sp).astype(BF16)

    pa = jnp.dot(a, wpa_ref[...], preferred_element_type=F32)
    pb = jnp.dot(gm, wpb_ref[...], preferred_element_type=F32)
    merged = (sa_ref[...].astype(F32) * pa + sb_ref[...].astype(F32) * pb).astype(BF16)
    mix = jnp.dot(merged, wo_ref[...], preferred_element_type=F32)
    x1 = _layer_norm(ALPHA * x + g1 * mix, g_ref[...], b_ref[...])
    x1_ref[...] = x1
    h2 = x1 * (1.0 + sc2) + sh2
    hi, lo = _split(h2)
    h2_ref[...] = _pack_pairs(h2)
    both = jnp.dot(hi, wrc_ref[...], preferred_element_type=F32)
    logits = (both[:, :LANES] + jnp.dot(lo, wrc_ref[:, :LANES], preferred_element_type=F32)
              + both[:, LANES:]) + br_ref[...]
    lane = lax.broadcasted_iota(jnp.int32, (TM, LANES), 1).astype(F32)
    vals, idxs = [], []
    for _ in range(TOP_K):
        mx = jnp.max(logits, axis=-1, keepdims=True)
        ix = jnp.min(jnp.where(logits == mx, lane, float(LANES)), axis=-1, keepdims=True)
        vals.append(mx)
        idxs.append(ix)
        logits = jnp.where(lane == ix, NEG_BIG * 2.0, logits)
    es = [jnp.exp(v - vals[0]) for v in vals]
    inv = 1.0 / (es[0] + es[1] + es[2] + es[3])

    @pl.when(i == 0)
    def _():
        run_ref[...] = jnp.zeros_like(run_ref)

    hot = [lane == ix for ix in idxs]
    memb = jnp.zeros((TM, LANES), F32)
    for hk in hot:
        memb = jnp.where(hk, 1.0, memb)
    row = lax.broadcasted_iota(jnp.int32, (TM, TM), 0)
    col = lax.broadcasted_iota(jnp.int32, (TM, TM), 1)
    before = jnp.where(row > col, 1.0, 0.0).astype(BF16)
    base = run_ref[...]
    rank_all = jnp.dot(before, memb.astype(BF16), preferred_element_type=F32) + base
    total = base + jnp.sum(memb, axis=0, keepdims=True)
    run_ref[...] = total
    cnt_ref[...] = total

    ti = jnp.zeros((TM, LANES), F32)
    tg = jnp.zeros((TM, LANES), F32)
    for k in range(TOP_K):
        rank_k = jnp.sum(jnp.where(hot[k], rank_all, 0.0), axis=-1, keepdims=True)
        ti = jnp.where(lane == float(k), idxs[k], ti)
        ti = jnp.where(lane == float(TOP_K + k), rank_k, ti)
        tg = jnp.where(lane == float(k), es[k] * inv, tg)
    ti_ref[...] = ti.astype(jnp.int32)
    tg_ref[...] = tg


def _mix(a_ctx, a_lat, ug, vg, sa, sb, xp, xs, mod3, ws, bs, wpa, wpb, wo, g, b, wrc, br):
    tile = lambda i: (i, 0)
    full2 = lambda i: (0, 0)
    tok = pl.BlockSpec((TM, D_MODEL), tile)
    ctx = pl.BlockSpec((TM, D_MODEL), lambda i: (_ctx_idx(i), 0))
    lat = pl.BlockSpec((TM, D_MODEL), lambda i: (_lat_idx(i), 0))
    wsq = pl.BlockSpec((D_MODEL, D_MODEL), full2)
    vec = pl.BlockSpec((1, D_MODEL), full2)
    return pl.pallas_call(
        _mix_kernel,
        grid=(NT,),
        in_specs=[ctx, lat, tok, tok, tok, tok, ctx, lat,
                  pl.BlockSpec((1, 1, 6 * D_MODEL), lambda i: (_mod_row(i), 0, 0)),
                  pl.BlockSpec((GMLP_GROUPS, CHUNK, CHUNK), lambda i: (0, 0, 0)),
                  pl.BlockSpec((CHUNK, D_MODEL), full2),
                  wsq, wsq, wsq, vec, vec,
                  pl.BlockSpec((D_MODEL, 2 * LANES), full2),
                  pl.BlockSpec((1, LANES), full2)],
        out_specs=[tok, pl.BlockSpec((TM, PACK_W), tile),
                   pl.BlockSpec((TM, LANES), tile), pl.BlockSpec((TM, LANES), tile),
                   pl.BlockSpec((1, LANES), full2)],
        out_shape=[jax.ShapeDtypeStruct((T_ALL, D_MODEL), F32),
                   jax.ShapeDtypeStruct((T_ALL, PACK_W), jnp.uint32),
                   jax.ShapeDtypeStruct((T_ALL, LANES), jnp.int32),
                   jax.ShapeDtypeStruct((T_ALL, LANES), F32),
                   jax.ShapeDtypeStruct((1, LANES), F32)],
        scratch_shapes=[pltpu.VMEM((1, LANES), F32)],
        compiler_params=_params(("arbitrary",)),
        name="mix_ln1_router",
    )(a_ctx, a_lat, ug, vg, sa, sb, xp, xs, mod3, ws, bs, wpa, wpb, wo, g, b, wrc, br)


def _dispatch_kernel(dest_ref, pend_ref, h_ref, out_hbm, zeros, sem):
    @pl.when(pl.program_id(0) == 0)
    def _():
        zeros[...] = jnp.zeros_like(zeros)

        def fill(start):
            return pltpu.make_async_copy(
                zeros, out_hbm.at[pl.ds(pl.multiple_of(start, MOE_BLOCK), MOE_BLOCK)], sem)

        fills = []
        prev = 0
        for e in range(N_EXPERTS):
            end = pend_ref[e]
            fills.append((end > prev, end - MOE_BLOCK))
            prev = end
        for j in range(N_EXPERTS):
            start = prev + j * MOE_BLOCK
            fills.append((start < out_hbm.shape[0], start))
        for needed, start in fills:
            pl.when(needed)(lambda start=start: fill(start).start())
        for needed, start in fills:
            pl.when(needed)(lambda start=start: fill(start).wait())

    def issue(g, carry):
        base = g * (SUBLANES * TOP_K)
        for u in range(SUBLANES):
            for k in range(TOP_K):
                dest = dest_ref[base + (u * TOP_K + k)]
                pltpu.make_async_copy(h_ref.at[g, pl.ds(u, 1)], out_hbm.at[pl.ds(dest, 1)],
                                      sem).start(priority=k % 2)
        return carry

    lax.fori_loop(0, TM // SUBLANES, issue, 0)
    for _ in range(TOP_K):
        pltpu.make_async_copy(out_hbm.at[pl.ds(0, TM)], out_hbm.at[pl.ds(0, TM)], sem).wait()


def _dispatch(dest_flat, pend, h2p, n_rows):
    return pl.pallas_call(
        _dispatch_kernel,
        grid=(NT,),
        in_specs=[pl.BlockSpec((ROWS_PER_TILE,), lambda i: (i,), memory_space=pltpu.SMEM),
                  pl.BlockSpec(memory_space=pltpu.SMEM),
                  pl.BlockSpec((TM // SUBLANES, SUBLANES, PACK_W), lambda i: (i, 0, 0))],
        out_specs=pl.BlockSpec(memory_space=pl.ANY),
        out_shape=jax.ShapeDtypeStruct((n_rows, PACK_W), jnp.uint32),
        scratch_shapes=[pltpu.VMEM((MOE_BLOCK, PACK_W), jnp.uint32),
                        pltpu.SemaphoreType.DMA(())],
        compiler_params=_params(("arbitrary",)),
        name="moe_dispatch",
    )(dest_flat, pend, h2p.reshape(T_ALL // SUBLANES, SUBLANES, PACK_W))


def _expert_kernel(be_ref, nu_ref, x_ref, wu_ref, bu_ref, wd_ref, bd_ref, y_ref, wu_bf, wd_bf):
    i = pl.program_id(0)
    used = i < nu_ref[0]
    new_expert = jnp.logical_or(i == 0, be_ref[i] != be_ref[jnp.maximum(i - 1, 0)])

    @pl.when(jnp.logical_and(used, new_expert))
    def _():
        wu_bf[...] = wu_ref[0].astype(BF16)
        wd_bf[...] = wd_ref[0].astype(BF16)

    @pl.when(used)
    def _():
        x = _unpack_pairs(x_ref[...]).astype(BF16)
        hu = jnp.dot(x, wu_bf[...], preferred_element_type=F32) + bu_ref[0]
        glu = jnp.minimum(hu[:, :D_EXPERT], SWIGLU_LIMIT)
        lin = jnp.clip(hu[:, D_EXPERT:], -SWIGLU_LIMIT, SWIGLU_LIMIT)
        act = glu * jax.nn.sigmoid(SWIGLU_ALPHA * glu) * (lin + 1.0)
        y = jnp.dot(act.astype(BF16), wd_bf[...], preferred_element_type=F32) + bd_ref[0]
        y_ref[...] = _pack_pairs(y)

    @pl.when(jnp.logical_not(used))
    def _():
        y_ref[...] = jnp.zeros_like(y_ref)


def _experts(blk_exp, n_used, xb, w_up, b_up, w_down, b_down):
    n_blocks = xb.shape[0] // MOE_BLOCK
    grid_spec = pltpu.PrefetchScalarGridSpec(
        num_scalar_prefetch=2,
        grid=(n_blocks,),
        in_specs=[pl.BlockSpec((MOE_BLOCK, PACK_W), lambda i, be, nu: (jnp.where(i < nu[0], i, 0), 0)),
                  pl.BlockSpec((1, D_MODEL, 2 * D_EXPERT), lambda i, be, nu: (be[i], 0, 0)),
                  pl.BlockSpec((1, 1, 2 * D_EXPERT), lambda i, be, nu: (be[i], 0, 0)),
                  pl.BlockSpec((1, D_EXPERT, D_MODEL), lambda i, be, nu: (be[i], 0, 0)),
                  pl.BlockSpec((1, 1, D_MODEL), lambda i, be, nu: (be[i], 0, 0))],
        out_specs=pl.BlockSpec((MOE_BLOCK, PACK_W), lambda i, be, nu: (i, 0)),
        scratch_shapes=[pltpu.VMEM((D_MODEL, 2 * D_EXPERT), BF16),
                        pltpu.VMEM((D_EXPERT, D_MODEL), BF16)],
    )
    return pl.pallas_call(
        _expert_kernel,
        grid_spec=grid_spec,
        out_shape=jax.ShapeDtypeStruct((n_blocks * MOE_BLOCK, PACK_W), jnp.uint32),
        compiler_params=_params(("arbitrary",)),
        name="expert_ffn",
    )(blk_exp, n_used, xb, w_up, b_up, w_down, b_down)


def _final_kernel(dcur_ref, dnxt_ref, x1_ref, tg_ref, mod_ref, g_ref, b_ref, yb_hbm,
                  yp_ref, ys_ref, ybuf, sem):
    i = pl.program_id(0)

    def gather(dest_ref, slot):
        def issue(g, carry):
            base = g * (SUBLANES * TOP_K)
            for u in range(SUBLANES):
                for k in range(TOP_K):
                    dest = dest_ref[base + (u * TOP_K + k)]
                    pltpu.make_async_copy(yb_hbm.at[pl.ds(dest, 1)],
                                          ybuf.at[slot, k, g, pl.ds(u, 1)],
                                          sem.at[slot]).start(priority=k % 2)
            return carry
        lax.fori_loop(0, TM // SUBLANES, issue, 0)

    def wait_rows(slot):
        for k in range(TOP_K):
            pltpu.make_async_copy(yb_hbm.at[pl.ds(0, TM)], yb_hbm.at[pl.ds(0, TM)],
                                  sem.at[slot]).wait()

    def combine(slot):
        g2 = mod_ref[0][:, 5 * D_MODEL:6 * D_MODEL]
        gates = tg_ref[...]
        ffn = jnp.zeros((TM, D_MODEL), F32)
        for k in range(TOP_K):
            yk = ybuf[slot, k].reshape(TM, PACK_W)
            ffn = ffn + gates[:, k:k + 1] * _unpack_pairs(yk)
        y = _layer_norm(ALPHA * x1_ref[...] + g2 * ffn, g_ref[...], b_ref[...])

        @pl.when(i < NT_CTX)
        def _():
            yp_ref[...] = y

        @pl.when(i >= NT_CTX)
        def _():
            ys_ref[...] = y

    @pl.when(i == 0)
    def _():
        gather(dcur_ref, 0)

    for slot in range(2):
        @pl.when(i % 2 == slot)
        def _():
            @pl.when(i + 1 < NT)
            def _():
                gather(dnxt_ref, 1 - slot)
            wait_rows(slot)
            combine(slot)


def _final(dest_flat, x1, top_g, mod3, g, b, ybp):
    tok = pl.BlockSpec((TM, D_MODEL), lambda i: (i, 0))
    vec = pl.BlockSpec((1, D_MODEL), lambda i: (0, 0))
    return pl.pallas_call(
        _final_kernel,
        grid=(NT,),
        in_specs=[pl.BlockSpec((ROWS_PER_TILE,), lambda i: (i,), memory_space=pltpu.SMEM),
                  pl.BlockSpec((ROWS_PER_TILE,), lambda i: (jnp.minimum(i + 1, NT - 1),),
                               memory_space=pltpu.SMEM),
                  tok, pl.BlockSpec((TM, LANES), lambda i: (i, 0)),
                  pl.BlockSpec((1, 1, 6 * D_MODEL), lambda i: (_mod_row(i), 0, 0)), vec, vec,
                  pl.BlockSpec(memory_space=pl.ANY)],
        scratch_shapes=[pltpu.VMEM((2, TOP_K, TM // SUBLANES, SUBLANES, PACK_W), jnp.uint32),
                        pltpu.SemaphoreType.DMA((2,))],
        out_specs=[pl.BlockSpec((TM, D_MODEL), lambda i: (_ctx_idx(i), 0)),
                   pl.BlockSpec((TM, D_MODEL), lambda i: (_lat_idx(i), 0))],
        out_shape=[jax.ShapeDtypeStruct((NT_CTX * TM, D_MODEL), F32),
                   jax.ShapeDtypeStruct((NT_LAT * TM, D_MODEL), F32)],
        compiler_params=_params(("arbitrary",)),
        name="residual_ln2",
    )(dest_flat, dest_flat, x1, top_g, mod3, g, b, ybp)


def _rope_tables():
    t = jnp.arange(LAT_LEN)
    r = (t // GRID_W).astype(F32)
    col = (t % GRID_W).astype(F32)
    inv = jnp.power(ROPE_BASE, -jnp.arange(ROPE_HALF, dtype=F32) / ROPE_HALF)
    ang_r = r[:, None] * inv
    ang_c = col[:, None] * inv
    ang = jnp.concatenate([ang_r, ang_r, ang_c, ang_c], axis=-1)
    cos = jnp.tile(jnp.cos(ang), (1, 2))
    sin = jnp.tile(jnp.sin(ang), (1, 2))
    sign = jnp.where((jnp.arange(LANES) % (2 * ROPE_HALF)) < ROPE_HALF, -1.0, 1.0).astype(F32)
    cos = jnp.concatenate([jnp.ones((TM, LANES), F32), cos], axis=0)
    sin = jnp.concatenate([jnp.zeros((TM, LANES), F32), sin * sign], axis=0)
    return cos, sin


def _routing(top_idx, rank, counts):
    n_tok = top_idx.shape[0]
    n_assign = n_tok * TOP_K
    padded = (counts + MOE_BLOCK - 1) // MOE_BLOCK * MOE_BLOCK
    pend = jnp.cumsum(padded)
    pstart = pend - padded
    experts = jnp.arange(N_EXPERTS, dtype=jnp.int32)
    dest_tk = rank + jnp.sum(jnp.where(top_idx[:, :, None] == experts, pstart, 0), axis=-1)
    n_blocks = -(-n_assign // MOE_BLOCK) + N_EXPERTS
    blk_start = jnp.arange(n_blocks, dtype=jnp.int32) * MOE_BLOCK
    blk_exp = jnp.minimum(jnp.sum((pend[None, :] <= blk_start[:, None]).astype(jnp.int32), axis=1),
                          N_EXPERTS - 1)
    n_used = (pend[-1] // MOE_BLOCK).astype(jnp.int32).reshape(1)
    return blk_exp, dest_tk.reshape(n_assign), n_used, pend.astype(jnp.int32), n_blocks * MOE_BLOCK


def kernel(x_prompt, x_sample, c, cache_k, cache_v, c_ctx, w_ada, b_ada, w_in, lambda_q1, lambda_k1, lambda_q2, lambda_k2, subln_w, gmlp_ln_g, gmlp_ln_b, w_spatial, b_spatial, b_gate, w_pa, w_pb, w_o, ln1_g, ln1_b, w_router, b_router, w_up, b_up, w_down, b_down, ln2_g, ln2_b):
    l = 0
    xp = x_prompt.reshape(N_CTX_B * CTX_LEN, D_MODEL)
    xs = x_sample.reshape(N_LAT_B * LAT_LEN, D_MODEL)

    cond = jnp.concatenate([c_ctx[None, :], c, jnp.zeros((16 - 1 - N_LAT_B, D_MODEL), F32)], axis=0)
    mod3 = _modulation(cond, w_ada[l], b_ada[l][None, :]).reshape(16, 1, 6 * D_MODEL)

    cos_t, sin_t = _rope_tables()
    (q_all, k_all, v_all, new_k, new_v, ug, vg, sa, sb) = _input_projection(
        xp, xs, mod3, w_in[l].astype(BF16), cos_t, sin_t, b_gate[l],
        gmlp_ln_g[l][None, :], gmlp_ln_b[l][None, :])

    lam = (jnp.exp(jnp.sum(lambda_q1[l] * lambda_k1[l])) - jnp.exp(jnp.sum(lambda_q2[l] * lambda_k2[l]))
           + LAM_INIT).reshape(1).astype(F32)
    subw = (subln_w[l] * (1.0 - LAM_INIT))[None, :]
    ctx_k = cache_k[:, l].reshape(N_LAT_B, N_HEADS, PAST_LEN, HEAD_W)
    ctx_v = cache_v[:, l]
    a_ctx = _attention(lam, q_all, k_all, v_all, None, None, subw, latent=False)
    a_lat = _attention(lam, q_all, k_all, v_all, ctx_k, ctx_v, subw, latent=True)

    bs_full = jnp.repeat(b_spatial[l].T, D_MODEL // GMLP_GROUPS, axis=1)
    wr = jnp.pad(w_router[l], ((0, 0), (0, LANES - N_EXPERTS)))
    wr_hi = wr.astype(BF16)
    wr_lo = (wr - wr_hi.astype(F32)).astype(BF16)
    br = jnp.concatenate([b_router[l], jnp.full((LANES - N_EXPERTS,), NEG_BIG, F32)])[None, :]
    x1, h2, top_i, top_g, cnt = _mix(
        a_ctx, a_lat, ug, vg, sa, sb, xp, xs, mod3, w_spatial[l].astype(BF16), bs_full,
        w_pa[l].astype(BF16), w_pb[l].astype(BF16), w_o[l].astype(BF16),
        ln1_g[l][None, :], ln1_b[l][None, :], jnp.concatenate([wr_hi, wr_lo], axis=1), br)

    blk_exp, dest_flat, n_used, pend, n_rows = _routing(
        top_i[:, :TOP_K], top_i[:, TOP_K:2 * TOP_K], cnt[0, :N_EXPERTS].astype(jnp.int32))
    xb = _dispatch(dest_flat, pend, h2, n_rows)
    yb = _experts(blk_exp, n_used, xb, w_up[l], b_up[l][:, None, :], w_down[l], b_down[l][:, None, :])
    y_prompt, y_sample = _final(dest_flat, x1, top_g, mod3, ln2_g[l][None, :], ln2_b[l][None, :], yb)
    return (y_prompt.reshape(N_CTX_B, CTX_LEN, D_MODEL),
            y_sample.reshape(N_LAT_B, LAT_LEN, D_MODEL),
            new_k.reshape(N_CTX_B, DEPTH, N_HEADS, CTX_LEN, 2, HEAD_DIM),
            new_v.reshape(N_CTX_B, DEPTH, N_HEADS, CTX_LEN, HEAD_W))
```

```python
import functools
import math

import jax
import jax.numpy as jnp
from jax import lax
from jax.experimental import pallas as pl
from jax.experimental.pallas import tpu as pltpu

F32 = jnp.float32
BF16 = jnp.bfloat16

D_MODEL = 1024
N_CTX_B = 16
CTX_LEN = 256
N_LAT_B = 8
LAT_LEN = 4096
PAST_LEN = 256
GRID_W = 64
N_HEADS = 8
HEAD_DIM = 64
HEAD_W = 2 * HEAD_DIM
ROPE_HALF = HEAD_DIM // 4
ROPE_BASE = 10000.0
GMLP_GROUPS = 4
CHUNK = 128
N_EXPERTS = 32
TOP_K = 4
D_EXPERT = 1024
SWIGLU_LIMIT = 7.0
SWIGLU_ALPHA = 1.702
MOE_BLOCK = 512
LN_EPS = 1e-5
DEPTH = 1
N_SEG = 7

TM = 256
NT_CTX = N_CTX_B * CTX_LEN // TM
NT_LAT = N_LAT_B * LAT_LEN // TM
NT = NT_CTX + NT_LAT
T_ALL = NT * TM
LAT_TILES = LAT_LEN // TM
LANES = 128
KEY_CHUNK = 512
SCORE_LEAD = 1
Q_TILES = 2
HEADS_PER_STEP = 2
ONES_ROWS = 16
PACK_W = D_MODEL // 2
ROWS_PER_TILE = TM * TOP_K
SUBLANES = 8
DISPATCH_TILES = 4
NEG_BIG = -1e30
VMEM_LIMIT = 56 * 1024 * 1024

ALPHA = (2.0 * DEPTH) ** 0.25
LAM_INIT = 0.8 - 0.6 * math.exp(-0.3 * 0)
Q_SCALE = HEAD_DIM ** -0.5 * math.log2(math.e)


def _ctx_idx(i):
    return jnp.minimum(i, NT_CTX - 1)


def _lat_idx(i):
    return jnp.maximum(i - NT_CTX, 0)


def _mod_row(i):
    return jnp.where(i < NT_CTX, 0, 1 + (i - NT_CTX) // LAT_TILES)


def _layer_norm(x, g, b):
    mu = jnp.mean(x, axis=-1, keepdims=True)
    xc = x - mu
    var = jnp.mean(xc * xc, axis=-1, keepdims=True)
    return xc * lax.rsqrt(var + LN_EPS) * g + b


def _gelu(x):
    return 0.5 * x * (1.0 + lax.erf(x * (1.0 / math.sqrt(2.0))))


def _split(x):
    hi = x.astype(BF16)
    return hi, (x - hi.astype(F32)).astype(BF16)


def _pack_pairs(x):
    w = x.shape[1] // 2
    a = pltpu.bitcast(x[:, :w].astype(BF16).astype(F32), jnp.uint32)
    b = pltpu.bitcast(x[:, w:].astype(BF16).astype(F32), jnp.uint32)
    return lax.shift_right_logical(a, jnp.uint32(16)) | b


def _unpack_pairs(p):
    a = pltpu.bitcast(lax.shift_left(p, jnp.uint32(16)), F32)
    b = pltpu.bitcast(p & jnp.uint32(0xFFFF0000), F32)
    return jnp.concatenate([a, b], axis=1)


def _params(sem):
    return pltpu.CompilerParams(dimension_semantics=sem, vmem_limit_bytes=VMEM_LIMIT)


def _mod_kernel(c_ref, w_ref, b_ref, o_ref):
    c = c_ref[...]
    s = c * jax.nn.sigmoid(c)
    s_hi, s_lo = _split(s)
    w_hi, w_lo = _split(w_ref[...])
    o_ref[...] = (jnp.dot(s_hi, w_hi, preferred_element_type=F32)
                  + jnp.dot(s_lo, w_hi, preferred_element_type=F32)
                  + jnp.dot(s_hi, w_lo, preferred_element_type=F32)) + b_ref[...]


def _modulation(cond, w_ada, b_ada):
    n = cond.shape[0]
    return pl.pallas_call(
        _mod_kernel,
        grid=(6,),
        in_specs=[pl.BlockSpec((n, D_MODEL), lambda j: (0, 0)),
                  pl.BlockSpec((D_MODEL, D_MODEL), lambda j: (0, j)),
                  pl.BlockSpec((1, D_MODEL), lambda j: (0, j))],
        out_specs=pl.BlockSpec((n, D_MODEL), lambda j: (0, j)),
        out_shape=jax.ShapeDtypeStruct((n, 6 * D_MODEL), F32),
        compiler_params=_params(("arbitrary",)),
        name="adaln_mod",
    )(cond, w_ada, b_ada)


def _proj_kernel(xp_ref, xs_ref, mod_ref, w_ref, cos_ref, sin_ref, bg_ref, lg_ref, lb_ref,
                 q_ref, k_ref, v_ref, nk_ref, nv_ref, ug_ref, vg_ref, sa_ref, sb_ref):
    i = pl.program_id(0)
    is_ctx = i < NT_CTX
    x = jnp.where(is_ctx, xp_ref[...], xs_ref[...])
    mod = mod_ref[0]
    sh1 = mod[:, 0:D_MODEL]
    sc1 = mod[:, D_MODEL:2 * D_MODEL]
    h = (x * (1.0 + sc1) + sh1).astype(BF16)

    def seg(s):
        return jnp.dot(h, w_ref[:, s * D_MODEL:(s + 1) * D_MODEL], preferred_element_type=F32)

    cos = cos_ref[...]
    sin = sin_ref[...]
    lane = lax.broadcasted_iota(jnp.int32, (TM, LANES), 1)
    first = (lane % (2 * ROPE_HALF)) < ROPE_HALF

    def rope(zh):
        up = pltpu.roll(zh, LANES - ROPE_HALF, 1)
        dn = pltpu.roll(zh, ROPE_HALF, 1)
        return zh * cos + jnp.where(first, up, dn) * sin

    zq = seg(0)
    for hd in range(N_HEADS):
        q_ref[0, hd] = (rope(zq[:, hd * HEAD_W:(hd + 1) * HEAD_W]) * Q_SCALE).astype(BF16)
    zk = seg(1)
    for hd in range(N_HEADS):
        k_ref[0, hd] = rope(zk[:, hd * HEAD_W:(hd + 1) * HEAD_W]).astype(BF16)
    zv = seg(2)
    for hd in range(N_HEADS):
        v_ref[0, hd] = zv[:, hd * HEAD_W:(hd + 1) * HEAD_W].astype(BF16)

    @pl.when(is_ctx)
    def _():
        for hd in range(N_HEADS):
            nk_ref[0, hd] = zk[:, hd * HEAD_W:(hd + 1) * HEAD_W]
            nv_ref[0, hd] = zv[:, hd * HEAD_W:(hd + 1) * HEAD_W]

    ug_ref[...] = _gelu(seg(3)).astype(BF16)
    vg_ref[...] = _layer_norm(_gelu(seg(4)), lg_ref[...], lb_ref[...]).astype(BF16)
    sa_ref[...] = jax.nn.sigmoid(seg(5) + bg_ref[0:1, :]).astype(BF16)
    sb_ref[...] = jax.nn.sigmoid(seg(6) + bg_ref[1:2, :]).astype(BF16)


def _input_projection(xp, xs, mod3, w_in, cos_t, sin_t, b_gate, ln_g, ln_b):
    tile = lambda i: (i, 0)
    head_blk = (1, N_HEADS, TM, HEAD_W)
    tok_spec = pl.BlockSpec((TM, D_MODEL), tile)
    return pl.pallas_call(
        _proj_kernel,
        grid=(NT,),
        in_specs=[
            pl.BlockSpec((TM, D_MODEL), lambda i: (_ctx_idx(i), 0)),
            pl.BlockSpec((TM, D_MODEL), lambda i: (_lat_idx(i), 0)),
            pl.BlockSpec((1, 1, 6 * D_MODEL), lambda i: (_mod_row(i), 0, 0)),
            pl.BlockSpec((D_MODEL, N_SEG * D_MODEL), lambda i: (0, 0)),
            pl.BlockSpec((TM, LANES), lambda i: (jnp.where(i < NT_CTX, 0, 1 + (i - NT_CTX) % LAT_TILES), 0)),
            pl.BlockSpec((TM, LANES), lambda i: (jnp.where(i < NT_CTX, 0, 1 + (i - NT_CTX) % LAT_TILES), 0)),
            pl.BlockSpec((2, D_MODEL), lambda i: (0, 0)),
            pl.BlockSpec((1, D_MODEL), lambda i: (0, 0)),
            pl.BlockSpec((1, D_MODEL), lambda i: (0, 0)),
        ],
        out_specs=[
            pl.BlockSpec(head_blk, lambda i: (i, 0, 0, 0)),
            pl.BlockSpec(head_blk, lambda i: (i, 0, 0, 0)),
            pl.BlockSpec(head_blk, lambda i: (i, 0, 0, 0)),
            pl.BlockSpec(head_blk, lambda i: (_ctx_idx(i), 0, 0, 0)),
            pl.BlockSpec(head_blk, lambda i: (_ctx_idx(i), 0, 0, 0)),
            tok_spec, tok_spec, tok_spec, tok_spec,
        ],
        out_shape=[
            jax.ShapeDtypeStruct((NT, N_HEADS, TM, HEAD_W), BF16),
            jax.ShapeDtypeStruct((NT, N_HEADS, TM, HEAD_W), BF16),
            jax.ShapeDtypeStruct((NT, N_HEADS, TM, HEAD_W), BF16),
            jax.ShapeDtypeStruct((NT_CTX, N_HEADS, TM, HEAD_W), F32),
            jax.ShapeDtypeStruct((NT_CTX, N_HEADS, TM, HEAD_W), F32),
            jax.ShapeDtypeStruct((T_ALL, D_MODEL), BF16),
            jax.ShapeDtypeStruct((T_ALL, D_MODEL), BF16),
            jax.ShapeDtypeStruct((T_ALL, D_MODEL), BF16),
            jax.ShapeDtypeStruct((T_ALL, D_MODEL), BF16),
        ],
        compiler_params=_params(("arbitrary",)),
        name="input_projection",
    )(xp, xs, mod3, w_in, cos_t, sin_t, b_gate, ln_g, ln_b)


def _attn_kernel(n_main, has_ctx, lam_ref, q_ref, k_ref, v_ref, *rest):
    if has_ctx:
        ck_ref, cv_ref, sub_ref, o_ref, kall, vt_all = rest
    else:
        sub_ref, o_ref, kall, vt_all = rest
    n_heads, n_keys = kall.shape[0], kall.shape[1]

    @pl.when(pl.program_id(2) == 0)
    def _():
        r = lax.broadcasted_iota(jnp.int32, (ONES_ROWS, n_keys), 0)
        for j in range(n_heads):
            for c in range(n_main):
                kall[j, c * TM:(c + 1) * TM, :] = k_ref[c, j]
                vt_all[j, 0:HEAD_W, c * TM:(c + 1) * TM] = v_ref[c, j].astype(F32).T.astype(BF16)
            if has_ctx:
                kall[j, n_main * TM:n_keys, :] = ck_ref[0, j].astype(BF16)
                vt_all[j, 0:HEAD_W, n_main * TM:n_keys] = cv_ref[0, j].T.astype(BF16)
            vt_all[j, HEAD_W:HEAD_W + ONES_ROWS, :] = jnp.where(r == 0, 1.0, 0.0).astype(BF16)

    tq = q_ref.shape[0] * TM
    lane = lax.broadcasted_iota(jnp.int32, (tq, HEAD_W), 1)
    qqs = []
    for j in range(n_heads):
        q = q_ref[:, j].reshape(tq, HEAD_W)
        zero = jnp.zeros_like(q)
        qqs.append(jnp.concatenate([jnp.where(lane < HEAD_DIM, q, zero),
                                    jnp.where(lane >= HEAD_DIM, q, zero)], axis=0))

    def scores(j, st, sz):
        return lax.dot_general(kall[j, st:st + sz, :], qqs[j], (((1,), (1,)), ((), ())),
                               preferred_element_type=F32)

    chunks = [(st, min(KEY_CHUNK, n_keys - st)) for st in range(0, n_keys, KEY_CHUNK)]
    pending = [[scores(j, *ch) for ch in chunks[:SCORE_LEAD]] for j in range(n_heads)]
    m = [None] * n_heads
    acc = [None] * n_heads
    for c, (st, sz) in enumerate(chunks):
        for j in range(n_heads):
            if c + SCORE_LEAD < len(chunks):
                pending[j].append(scores(j, *chunks[c + SCORE_LEAD]))
            s = pending[j].pop(0)
            m_c = jnp.max(s, axis=0, keepdims=True)
            m_new = m_c if c == 0 else jnp.maximum(m[j], m_c)
            e = jnp.exp2(s - m_new).astype(BF16)
            pv = jnp.dot(vt_all[j, :, st:st + sz], e, preferred_element_type=F32)
            acc[j] = pv if c == 0 else jnp.exp2(m[j] - m_new) * acc[j] + pv
            m[j] = m_new
    for j in range(n_heads):
        l = acc[j][HEAD_W:HEAD_W + 1, :]
        c1 = 1.0 / l[:, :tq]
        c2 = lam_ref[0] / l[:, tq:]
        a = (acc[j][:HEAD_W, :tq] * c1 - acc[j][:HEAD_W, tq:] * c2).T
        ms = jnp.mean(a * a, axis=-1, keepdims=True)
        o_ref[:, j * HEAD_W:(j + 1) * HEAD_W] = (a * lax.rsqrt(ms + LN_EPS) * sub_ref[...]).astype(BF16)


def _attention(lam, q_all, k_all, v_all, ctx_k, ctx_v, subw, latent):
    if latent:
        n_b, n_q, n_main, base, qt = N_LAT_B, LAT_TILES // Q_TILES, LAT_TILES, NT_CTX, Q_TILES
        n_keys, hp = LAT_LEN + PAST_LEN, HEADS_PER_STEP
    else:
        n_b, n_q, n_main, base, qt = N_CTX_B, 1, 1, 0, 1
        n_keys, hp = CTX_LEN, N_HEADS
    q_spec = pl.BlockSpec((qt, hp, TM, HEAD_W), lambda b, h, t: (base // qt + b * n_q + t, h, 0, 0))
    kv_spec = pl.BlockSpec((n_main, hp, TM, HEAD_W), lambda b, h, t: (base // n_main + b, h, 0, 0))
    in_specs = [pl.BlockSpec(memory_space=pltpu.SMEM), q_spec, kv_spec, kv_spec]
    args = [lam, q_all, k_all, v_all]
    if latent:
        c_spec = pl.BlockSpec((1, hp, PAST_LEN, HEAD_W), lambda b, h, t: (b, h, 0, 0))
        in_specs += [c_spec, c_spec]
        args += [ctx_k, ctx_v]
    in_specs.append(pl.BlockSpec((1, HEAD_W), lambda b, h, t: (0, 0)))
    args.append(subw)
    return pl.pallas_call(
        functools.partial(_attn_kernel, n_main, latent),
        grid=(n_b, N_HEADS // hp, n_q),
        in_specs=in_specs,
        out_specs=pl.BlockSpec((qt * TM, hp * HEAD_W), lambda b, h, t: (b * n_q + t, h)),
        out_shape=jax.ShapeDtypeStruct((n_b * n_q * qt * TM, D_MODEL), BF16),
        scratch_shapes=[pltpu.VMEM((hp, n_keys, HEAD_W), BF16),
                        pltpu.VMEM((hp, HEAD_W + ONES_ROWS, n_keys), BF16)],
        compiler_params=_params(("arbitrary", "arbitrary", "arbitrary")),
        name="diff_attention_lat" if latent else "diff_attention_ctx",
    )(*args)


def _mix_kernel(ac_ref, al_ref, ug_ref, vg_ref, sa_ref, sb_ref, xp_ref, xs_ref, mod_ref,
                ws_ref, bs_ref, wpa_ref, wpb_ref, wo_ref, g_ref, b_ref, wrc_ref, br_ref,
                x1_ref, h2_ref, ti_ref, tg_ref, cnt_ref, run_ref):
    i = pl.program_id(0)
    is_ctx = i < NT_CTX
    a = jnp.where(is_ctx, ac_ref[...], al_ref[...])
    x = jnp.where(is_ctx, xp_ref[...], xs_ref[...])
    mod = mod_ref[0]
    g1 = mod[:, 2 * D_MODEL:3 * D_MODEL]
    sh2 = mod[:, 3 * D_MODEL:4 * D_MODEL]
    sc2 = mod[:, 4 * D_MODEL:5 * D_MODEL]

    gw = D_MODEL // GMLP_GROUPS
    chunks = []
    for c in range(TM // CHUNK):
        groups = []
        for g in range(GMLP_GROUPS):
            vc = vg_ref[c * CHUNK:(c + 1) * CHUNK, g * gw:(g + 1) * gw]
            groups.append(jnp.dot(ws_ref[g], vc, preferred_element_type=F32))
        chunks.append(jnp.concatenate(groups, axis=1) + bs_ref[...])
    sp = jnp.concatenate(chunks, axis=0)
    gm = (ug_ref[...].astype(F32) * sp).astype(BF16)

    pa = jnp.dot(a, wpa_ref[...], preferred_element_type=F32)
    pb = jnp.dot(gm, wpb_ref[...], preferred_element_type=F32)
    merged = (sa_ref[...].astype(F32) * pa + sb_ref[...].astype(F32) * pb).astype(BF16)
    mix = jnp.dot(merged, wo_ref[...], preferred_element_type=F32)
    x1 = _layer_norm(ALPHA * x + g1 * mix, g_ref[...], b_ref[...])
    x1_ref[...] = x1
    h2 = x1 * (1.0 + sc2) + sh2
    hi, lo = _split(h2)
    h2_ref[...] = _pack_pairs(h2)
    both = jnp.dot(hi, wrc_ref[...], preferred_element_type=F32)
    logits = (both[:, :LANES] + jnp.dot(lo, wrc_ref[:, :LANES], preferred_element_type=F32)
              + both[:, LANES:]) + br_ref[...]
    lane = lax.broadcasted_iota(jnp.int32, (TM, LANES), 1).astype(F32)
    vals, idxs = [], []
    for _ in range(TOP_K):
        mx = jnp.max(logits, axis=-1, keepdims=True)
        ix = jnp.min(jnp.where(logits == mx, lane, float(LANES)), axis=-1, keepdims=True)
        vals.append(mx)
        idxs.append(ix)
        logits = jnp.where(lane == ix, NEG_BIG * 2.0, logits)
    es = [jnp.exp(v - vals[0]) for v in vals]
    inv = 1.0 / (es[0] + es[1] + es[2] + es[3])

    @pl.when(i == 0)
    def _():
        run_ref[...] = jnp.zeros_like(run_ref)

    hot = [lane == ix for ix in idxs]
    memb = jnp.zeros((TM, LANES), F32)
    for hk in hot:
        memb = jnp.where(hk, 1.0, memb)
    row = lax.broadcasted_iota(jnp.int32, (TM, TM), 0)
    col = lax.broadcasted_iota(jnp.int32, (TM, TM), 1)
    before = jnp.where(row > col, 1.0, 0.0).astype(BF16)
    base = run_ref[...]
    rank_all = jnp.dot(before, memb.astype(BF16), preferred_element_type=F32) + base
    total = base + jnp.sum(memb, axis=0, keepdims=True)
    run_ref[...] = total
    cnt_ref[...] = total

    ti = jnp.zeros((TM, LANES), F32)
    tg = jnp.zeros((TM, LANES), F32)
    for k in range(TOP_K):
        rank_k = jnp.sum(jnp.where(hot[k], rank_all, 0.0), axis=-1, keepdims=True)
        ti = jnp.where(lane == float(k), idxs[k], ti)
        ti = jnp.where(lane == float(TOP_K + k), rank_k, ti)
        tg = jnp.where(lane == float(k), es[k] * inv, tg)
    ti_ref[...] = ti.astype(jnp.int32)
    tg_ref[...] = tg


def _mix(a_ctx, a_lat, ug, vg, sa, sb, xp, xs, mod3, ws, bs, wpa, wpb, wo, g, b, wrc, br):
    tile = lambda i: (i, 0)
    full2 = lambda i: (0, 0)
    tok = pl.BlockSpec((TM, D_MODEL), tile)
    ctx = pl.BlockSpec((TM, D_MODEL), lambda i: (_ctx_idx(i), 0))
    lat = pl.BlockSpec((TM, D_MODEL), lambda i: (_lat_idx(i), 0))
    wsq = pl.BlockSpec((D_MODEL, D_MODEL), full2)
    vec = pl.BlockSpec((1, D_MODEL), full2)
    return pl.pallas_call(
        _mix_kernel,
        grid=(NT,),
        in_specs=[ctx, lat, tok, tok, tok, tok, ctx, lat,
                  pl.BlockSpec((1, 1, 6 * D_MODEL), lambda i: (_mod_row(i), 0, 0)),
                  pl.BlockSpec((GMLP_GROUPS, CHUNK, CHUNK), lambda i: (0, 0, 0)),
                  pl.BlockSpec((CHUNK, D_MODEL), full2),
                  wsq, wsq, wsq, vec, vec,
                  pl.BlockSpec((D_MODEL, 2 * LANES), full2),
                  pl.BlockSpec((1, LANES), full2)],
        out_specs=[tok, pl.BlockSpec((TM, PACK_W), tile),
                   pl.BlockSpec((TM, LANES), tile), pl.BlockSpec((TM, LANES), tile),
                   pl.BlockSpec((1, LANES), full2)],
        out_shape=[jax.ShapeDtypeStruct((T_ALL, D_MODEL), F32),
                   jax.ShapeDtypeStruct((T_ALL, PACK_W), jnp.uint32),
                   jax.ShapeDtypeStruct((T_ALL, LANES), jnp.int32),
                   jax.ShapeDtypeStruct((T_ALL, LANES), F32),
                   jax.ShapeDtypeStruct((1, LANES), F32)],
        scratch_shapes=[pltpu.VMEM((1, LANES), F32)],
        compiler_params=_params(("arbitrary",)),
        name="mix_ln1_router",
    )(a_ctx, a_lat, ug, vg, sa, sb, xp, xs, mod3, ws, bs, wpa, wpb, wo, g, b, wrc, br)


def _dispatch_kernel(dest_ref, pend_ref, h_ref, out_hbm, zeros, sem):
    @pl.when(pl.program_id(0) == 0)
    def _():
        zeros[...] = jnp.zeros_like(zeros)

        def fill(start):
            return pltpu.make_async_copy(
                zeros, out_hbm.at[pl.ds(pl.multiple_of(start, MOE_BLOCK), MOE_BLOCK)], sem)

        fills = []
        prev = 0
        for e in range(N_EXPERTS):
            end = pend_ref[e]
            fills.append((end > prev, end - MOE_BLOCK))
            prev = end
        for j in range(N_EXPERTS):
            start = prev + j * MOE_BLOCK
            fills.append((start < out_hbm.shape[0], start))
        for needed, start in fills:
            pl.when(needed)(lambda start=start: fill(start).start())
        for needed, start in fills:
            pl.when(needed)(lambda start=start: fill(start).wait())

    def issue(g, carry):
        base = g * (SUBLANES * TOP_K)
        for u in range(SUBLANES):
            for k in range(TOP_K):
                dest = dest_ref[base + (u * TOP_K + k)]
                pltpu.make_async_copy(h_ref.at[g, pl.ds(u, 1)], out_hbm.at[pl.ds(dest, 1)],
                                      sem).start(priority=k % 2)
        return carry

    lax.fori_loop(0, h_ref.shape[0], issue, 0)
    for _ in range(h_ref.shape[0] * SUBLANES * TOP_K // TM):
        pltpu.make_async_copy(out_hbm.at[pl.ds(0, TM)], out_hbm.at[pl.ds(0, TM)], sem).wait()


def _dispatch(dest_flat, pend, h2p, n_rows):
    return pl.pallas_call(
        _dispatch_kernel,
        grid=(NT // DISPATCH_TILES,),
        in_specs=[pl.BlockSpec((DISPATCH_TILES * ROWS_PER_TILE,), lambda i: (i,),
                               memory_space=pltpu.SMEM),
                  pl.BlockSpec(memory_space=pltpu.SMEM),
                  pl.BlockSpec((DISPATCH_TILES * TM // SUBLANES, SUBLANES, PACK_W),
                               lambda i: (i, 0, 0))],
        out_specs=pl.BlockSpec(memory_space=pl.ANY),
        out_shape=jax.ShapeDtypeStruct((n_rows, PACK_W), jnp.uint32),
        scratch_shapes=[pltpu.VMEM((MOE_BLOCK, PACK_W), jnp.uint32),
                        pltpu.SemaphoreType.DMA(())],
        compiler_params=_params(("arbitrary",)),
        name="moe_dispatch",
    )(dest_flat, pend, h2p.reshape(T_ALL // SUBLANES, SUBLANES, PACK_W))


def _expert_kernel(be_ref, nu_ref, x_ref, wu_ref, bu_ref, wd_ref, bd_ref, y_ref, wu_bf, wd_bf):
    i = pl.program_id(0)
    used = i < nu_ref[0]
    new_expert = jnp.logical_or(i == 0, be_ref[i] != be_ref[jnp.maximum(i - 1, 0)])

    @pl.when(jnp.logical_and(used, new_expert))
    def _():
        wu_bf[...] = wu_ref[0].astype(BF16)
        wd_bf[...] = wd_ref[0].astype(BF16)

    @pl.when(used)
    def _():
        x = _unpack_pairs(x_ref[...]).astype(BF16)
        hu = jnp.dot(x, wu_bf[...], preferred_element_type=F32) + bu_ref[0]
        glu = jnp.minimum(hu[:, :D_EXPERT], SWIGLU_LIMIT)
        lin = jnp.clip(hu[:, D_EXPERT:], -SWIGLU_LIMIT, SWIGLU_LIMIT)
        act = glu * jax.nn.sigmoid(SWIGLU_ALPHA * glu) * (lin + 1.0)
        y = jnp.dot(act.astype(BF16), wd_bf[...], preferred_element_type=F32) + bd_ref[0]
        y_ref[...] = _pack_pairs(y)

    @pl.when(jnp.logical_not(used))
    def _():
        y_ref[...] = jnp.zeros_like(y_ref)


def _experts(blk_exp, n_used, xb, w_up, b_up, w_down, b_down):
    n_blocks = xb.shape[0] // MOE_BLOCK
    grid_spec = pltpu.PrefetchScalarGridSpec(
        num_scalar_prefetch=2,
        grid=(n_blocks,),
        in_specs=[pl.BlockSpec((MOE_BLOCK, PACK_W), lambda i, be, nu: (jnp.where(i < nu[0], i, 0), 0)),
                  pl.BlockSpec((1, D_MODEL, 2 * D_EXPERT), lambda i, be, nu: (be[i], 0, 0)),
                  pl.BlockSpec((1, 1, 2 * D_EXPERT), lambda i, be, nu: (be[i], 0, 0)),
                  pl.BlockSpec((1, D_EXPERT, D_MODEL), lambda i, be, nu: (be[i], 0, 0)),
                  pl.BlockSpec((1, 1, D_MODEL), lambda i, be, nu: (be[i], 0, 0))],
        out_specs=pl.BlockSpec((MOE_BLOCK, PACK_W), lambda i, be, nu: (i, 0)),
        scratch_shapes=[pltpu.VMEM((D_MODEL, 2 * D_EXPERT), BF16),
                        pltpu.VMEM((D_EXPERT, D_MODEL), BF16)],
    )
    return pl.pallas_call(
        _expert_kernel,
        grid_spec=grid_spec,
        out_shape=jax.ShapeDtypeStruct((n_blocks * MOE_BLOCK, PACK_W), jnp.uint32),
        compiler_params=_params(("arbitrary",)),
        name="expert_ffn",
    )(blk_exp, n_used, xb, w_up, b_up, w_down, b_down)


def _final_kernel(dcur_ref, dnxt_ref, x1_ref, tg_ref, mod_ref, g_ref, b_ref, yb_hbm,
                  yp_ref, ys_ref, ybuf, sem):
    i = pl.program_id(0)

    def gather(dest_ref, slot):
        def issue(g, carry):
            base = g * (SUBLANES * TOP_K)
            for u in range(SUBLANES):
                for k in range(TOP_K):
                    dest = dest_ref[base + (u * TOP_K + k)]
                    pltpu.make_async_copy(yb_hbm.at[pl.ds(dest, 1)],
                                          ybuf.at[slot, k, g, pl.ds(u, 1)],
                                          sem.at[slot]).start(priority=k % 2)
            return carry
        lax.fori_loop(0, TM // SUBLANES, issue, 0)

    def wait_rows(slot):
        for k in range(TOP_K):
            pltpu.make_async_copy(yb_hbm.at[pl.ds(0, TM)], yb_hbm.at[pl.ds(0, TM)],
                                  sem.at[slot]).wait()

    def combine(slot):
        g2 = mod_ref[0][:, 5 * D_MODEL:6 * D_MODEL]
        gates = tg_ref[...]
        ffn = jnp.zeros((TM, D_MODEL), F32)
        for k in range(TOP_K):
            yk = ybuf[slot, k].reshape(TM, PACK_W)
            ffn = ffn + gates[:, k:k + 1] * _unpack_pairs(yk)
        y = _layer_norm(ALPHA * x1_ref[...] + g2 * ffn, g_ref[...], b_ref[...])

        @pl.when(i < NT_CTX)
        def _():
            yp_ref[...] = y

        @pl.when(i >= NT_CTX)
        def _():
            ys_ref[...] = y

    @pl.when(i == 0)
    def _():
        gather(dcur_ref, 0)

    for slot in range(2):
        @pl.when(i % 2 == slot)
        def _():
            @pl.when(i + 1 < NT)
            def _():
                gather(dnxt_ref, 1 - slot)
            wait_rows(slot)
            combine(slot)


def _final(dest_flat, x1, top_g, mod3, g, b, ybp):
    tok = pl.BlockSpec((TM, D_MODEL), lambda i: (i, 0))
    vec = pl.BlockSpec((1, D_MODEL), lambda i: (0, 0))
    return pl.pallas_call(
        _final_kernel,
        grid=(NT,),
        in_specs=[pl.BlockSpec((ROWS_PER_TILE,), lambda i: (i,), memory_space=pltpu.SMEM),
                  pl.BlockSpec((ROWS_PER_TILE,), lambda i: (jnp.minimum(i + 1, NT - 1),),
                               memory_space=pltpu.SMEM),
                  tok, pl.BlockSpec((TM, LANES), lambda i: (i, 0)),
                  pl.BlockSpec((1, 1, 6 * D_MODEL), lambda i: (_mod_row(i), 0, 0)), vec, vec,
                  pl.BlockSpec(memory_space=pl.ANY)],
        scratch_shapes=[pltpu.VMEM((2, TOP_K, TM // SUBLANES, SUBLANES, PACK_W), jnp.uint32),
                        pltpu.SemaphoreType.DMA((2,))],
        out_specs=[pl.BlockSpec((TM, D_MODEL), lambda i: (_ctx_idx(i), 0)),
                   pl.BlockSpec((TM, D_MODEL), lambda i: (_lat_idx(i), 0))],
        out_shape=[jax.ShapeDtypeStruct((NT_CTX * TM, D_MODEL), F32),
                   jax.ShapeDtypeStruct((NT_LAT * TM, D_MODEL), F32)],
        compiler_params=_params(("arbitrary",)),
        name="residual_ln2",
    )(dest_flat, dest_flat, x1, top_g, mod3, g, b, ybp)


def _rope_tables():
    t = jnp.arange(LAT_LEN)
    r = (t // GRID_W).astype(F32)
    col = (t % GRID_W).astype(F32)
    inv = jnp.power(ROPE_BASE, -jnp.arange(ROPE_HALF, dtype=F32) / ROPE_HALF)
    ang_r = r[:, None] * inv
    ang_c = col[:, None] * inv
    ang = jnp.concatenate([ang_r, ang_r, ang_c, ang_c], axis=-1)
    cos = jnp.tile(jnp.cos(ang), (1, 2))
    sin = jnp.tile(jnp.sin(ang), (1, 2))
    sign = jnp.where((jnp.arange(LANES) % (2 * ROPE_HALF)) < ROPE_HALF, -1.0, 1.0).astype(F32)
    cos = jnp.concatenate([jnp.ones((TM, LANES), F32), cos], axis=0)
    sin = jnp.concatenate([jnp.zeros((TM, LANES), F32), sin * sign], axis=0)
    return cos, sin


def _routing(top_idx, rank, counts):
    n_tok = top_idx.shape[0]
    n_assign = n_tok * TOP_K
    padded = (counts + MOE_BLOCK - 1) // MOE_BLOCK * MOE_BLOCK
    pend = jnp.cumsum(padded)
    pstart = pend - padded
    experts = jnp.arange(N_EXPERTS, dtype=jnp.int32)
    dest_tk = rank + jnp.sum(jnp.where(top_idx[:, :, None] == experts, pstart, 0), axis=-1)
    n_blocks = -(-n_assign // MOE_BLOCK) + N_EXPERTS
    blk_start = jnp.arange(n_blocks, dtype=jnp.int32) * MOE_BLOCK
    blk_exp = jnp.minimum(jnp.sum((pend[None, :] <= blk_start[:, None]).astype(jnp.int32), axis=1),
                          N_EXPERTS - 1)
    n_used = (pend[-1] // MOE_BLOCK).astype(jnp.int32).reshape(1)
    return blk_exp, dest_tk.reshape(n_assign), n_used, pend.astype(jnp.int32), n_blocks * MOE_BLOCK


def kernel(x_prompt, x_sample, c, cache_k, cache_v, c_ctx, w_ada, b_ada, w_in, lambda_q1, lambda_k1, lambda_q2, lambda_k2, subln_w, gmlp_ln_g, gmlp_ln_b, w_spatial, b_spatial, b_gate, w_pa, w_pb, w_o, ln1_g, ln1_b, w_router, b_router, w_up, b_up, w_down, b_down, ln2_g, ln2_b):
    l = 0
    xp = x_prompt.reshape(N_CTX_B * CTX_LEN, D_MODEL)
    xs = x_sample.reshape(N_LAT_B * LAT_LEN, D_MODEL)

    cond = jnp.concatenate([c_ctx[None, :], c, jnp.zeros((16 - 1 - N_LAT_B, D_MODEL), F32)], axis=0)
    mod3 = _modulation(cond, w_ada[l], b_ada[l][None, :]).reshape(16, 1, 6 * D_MODEL)

    cos_t, sin_t = _rope_tables()
    (q_all, k_all, v_all, new_k, new_v, ug, vg, sa, sb) = _input_projection(
        xp, xs, mod3, w_in[l].astype(BF16), cos_t, sin_t, b_gate[l],
        gmlp_ln_g[l][None, :], gmlp_ln_b[l][None, :])

    lam = (jnp.exp(jnp.sum(lambda_q1[l] * lambda_k1[l])) - jnp.exp(jnp.sum(lambda_q2[l] * lambda_k2[l]))
           + LAM_INIT).reshape(1).astype(F32)
    subw = (subln_w[l] * (1.0 - LAM_INIT))[None, :]
    ctx_k = cache_k[:, l].reshape(N_LAT_B, N_HEADS, PAST_LEN, HEAD_W)
    ctx_v = cache_v[:, l]
    a_ctx = _attention(lam, q_all, k_all, v_all, None, None, subw, latent=False)
    a_lat = _attention(lam, q_all, k_all, v_all, ctx_k, ctx_v, subw, latent=True)

    bs_full = jnp.repeat(b_spatial[l].T, D_MODEL // GMLP_GROUPS, axis=1)
    wr = jnp.pad(w_router[l], ((0, 0), (0, LANES - N_EXPERTS)))
    wr_hi = wr.astype(BF16)
    wr_lo = (wr - wr_hi.astype(F32)).astype(BF16)
    br = jnp.concatenate([b_router[l], jnp.full((LANES - N_EXPERTS,), NEG_BIG, F32)])[None, :]
    x1, h2, top_i, top_g, cnt = _mix(
        a_ctx, a_lat, ug, vg, sa, sb, xp, xs, mod3, w_spatial[l].astype(BF16), bs_full,
        w_pa[l].astype(BF16), w_pb[l].astype(BF16), w_o[l].astype(BF16),
        ln1_g[l][None, :], ln1_b[l][None, :], jnp.concatenate([wr_hi, wr_lo], axis=1), br)

    blk_exp, dest_flat, n_used, pend, n_rows = _routing(
        top_i[:, :TOP_K], top_i[:, TOP_K:2 * TOP_K], cnt[0, :N_EXPERTS].astype(jnp.int32))
    xb = _dispatch(dest_flat, pend, h2, n_rows)
    yb = _experts(blk_exp, n_used, xb, w_up[l], b_up[l][:, None, :], w_down[l], b_down[l][:, None, :])
    y_prompt, y_sample = _final(dest_flat, x1, top_g, mod3, ln2_g[l][None, :], ln2_b[l][None, :], yb)
    return (y_prompt.reshape(N_CTX_B, CTX_LEN, D_MODEL),
            y_sample.reshape(N_LAT_B, LAT_LEN, D_MODEL),
            new_k.reshape(N_CTX_B, DEPTH, N_HEADS, CTX_LEN, 2, HEAD_DIM),
            new_v.reshape(N_CTX_B, DEPTH, N_HEADS, CTX_LEN, HEAD_W))
```

```python
import functools
import math

import jax
import jax.numpy as jnp
from jax import lax
from jax.experimental import pallas as pl
from jax.experimental.pallas import tpu as pltpu

F32 = jnp.float32
BF16 = jnp.bfloat16

D_MODEL = 1024
N_CTX_B = 16
CTX_LEN = 256
N_LAT_B = 8
LAT_LEN = 4096
PAST_LEN = 256
GRID_W = 64
N_HEADS = 8
HEAD_DIM = 64
HEAD_W = 2 * HEAD_DIM
ROPE_HALF = HEAD_DIM // 4
ROPE_BASE = 10000.0
GMLP_GROUPS = 4
CHUNK = 128
N_EXPERTS = 32
TOP_K = 4
D_EXPERT = 1024
SWIGLU_LIMIT = 7.0
SWIGLU_ALPHA = 1.702
MOE_BLOCK = 1024
LN_EPS = 1e-5
DEPTH = 1
N_SEG = 7

TM = 256
NT_CTX = N_CTX_B * CTX_LEN // TM
NT_LAT = N_LAT_B * LAT_LEN // TM
NT = NT_CTX + NT_LAT
T_ALL = NT * TM
LAT_TILES = LAT_LEN // TM
LANES = 128
KEY_CHUNK = 512
SCORE_LEAD = 1
Q_TILES = 2
HEADS_PER_STEP = 2
ONES_ROWS = 16
PACK_W = D_MODEL // 2
ROWS_PER_TILE = TM * TOP_K
SUBLANES = 8
DISPATCH_TILES = 4
NEG_BIG = -1e30
VMEM_LIMIT = 56 * 1024 * 1024

ALPHA = (2.0 * DEPTH) ** 0.25
LAM_INIT = 0.8 - 0.6 * math.exp(-0.3 * 0)
Q_SCALE = HEAD_DIM ** -0.5 * math.log2(math.e)


def _ctx_idx(i):
    return jnp.minimum(i, NT_CTX - 1)


def _lat_idx(i):
    return jnp.maximum(i - NT_CTX, 0)


def _mod_row(i):
    return jnp.where(i < NT_CTX, 0, 1 + (i - NT_CTX) // LAT_TILES)


def _layer_norm(x, g, b):
    mu = jnp.mean(x, axis=-1, keepdims=True)
    xc = x - mu
    var = jnp.mean(xc * xc, axis=-1, keepdims=True)
    return xc * lax.rsqrt(var + LN_EPS) * g + b


def _gelu(x):
    return 0.5 * x * (1.0 + lax.erf(x * (1.0 / math.sqrt(2.0))))


def _split(x):
    hi = x.astype(BF16)
    return hi, (x - hi.astype(F32)).astype(BF16)


def _pack_pairs(x):
    w = x.shape[1] // 2
    a = pltpu.bitcast(x[:, :w].astype(BF16).astype(F32), jnp.uint32)
    b = pltpu.bitcast(x[:, w:].astype(BF16).astype(F32), jnp.uint32)
    return lax.shift_right_logical(a, jnp.uint32(16)) | b


def _unpack_pairs(p):
    a = pltpu.bitcast(lax.shift_left(p, jnp.uint32(16)), F32)
    b = pltpu.bitcast(p & jnp.uint32(0xFFFF0000), F32)
    return jnp.concatenate([a, b], axis=1)


def _params(sem):
    return pltpu.CompilerParams(dimension_semantics=sem, vmem_limit_bytes=VMEM_LIMIT)


def _mod_kernel(c_ref, w_ref, b_ref, o_ref):
    c = c_ref[...]
    s = c * jax.nn.sigmoid(c)
    s_hi, s_lo = _split(s)
    w_hi, w_lo = _split(w_ref[...])
    o_ref[...] = (jnp.dot(s_hi, w_hi, preferred_element_type=F32)
                  + jnp.dot(s_lo, w_hi, preferred_element_type=F32)
                  + jnp.dot(s_hi, w_lo, preferred_element_type=F32)) + b_ref[...]


def _modulation(cond, w_ada, b_ada):
    n = cond.shape[0]
    return pl.pallas_call(
        _mod_kernel,
        grid=(6,),
        in_specs=[pl.BlockSpec((n, D_MODEL), lambda j: (0, 0)),
                  pl.BlockSpec((D_MODEL, D_MODEL), lambda j: (0, j)),
                  pl.BlockSpec((1, D_MODEL), lambda j: (0, j))],
        out_specs=pl.BlockSpec((n, D_MODEL), lambda j: (0, j)),
        out_shape=jax.ShapeDtypeStruct((n, 6 * D_MODEL), F32),
        compiler_params=_params(("arbitrary",)),
        name="adaln_mod",
    )(cond, w_ada, b_ada)


def _proj_kernel(xp_ref, xs_ref, mod_ref, w_ref, cos_ref, sin_ref, bg_ref, lg_ref, lb_ref,
                 q_ref, k_ref, v_ref, nk_ref, nv_ref, ug_ref, vg_ref, sa_ref, sb_ref):
    i = pl.program_id(0)
    is_ctx = i < NT_CTX
    x = jnp.where(is_ctx, xp_ref[...], xs_ref[...])
    mod = mod_ref[0]
    sh1 = mod[:, 0:D_MODEL]
    sc1 = mod[:, D_MODEL:2 * D_MODEL]
    h = (x * (1.0 + sc1) + sh1).astype(BF16)

    def seg(s):
        return jnp.dot(h, w_ref[:, s * D_MODEL:(s + 1) * D_MODEL], preferred_element_type=F32)

    cos = cos_ref[...]
    sin = sin_ref[...]
    lane = lax.broadcasted_iota(jnp.int32, (TM, LANES), 1)
    first = (lane % (2 * ROPE_HALF)) < ROPE_HALF

    def rope(zh):
        up = pltpu.roll(zh, LANES - ROPE_HALF, 1)
        dn = pltpu.roll(zh, ROPE_HALF, 1)
        return zh * cos + jnp.where(first, up, dn) * sin

    zq = seg(0)
    for hd in range(N_HEADS):
        q_ref[0, hd] = (rope(zq[:, hd * HEAD_W:(hd + 1) * HEAD_W]) * Q_SCALE).astype(BF16)
    zk = seg(1)
    for hd in range(N_HEADS):
        k_ref[0, hd] = rope(zk[:, hd * HEAD_W:(hd + 1) * HEAD_W]).astype(BF16)
    zv = seg(2)
    for hd in range(N_HEADS):
        v_ref[0, hd] = zv[:, hd * HEAD_W:(hd + 1) * HEAD_W].astype(BF16)

    @pl.when(is_ctx)
    def _():
        for hd in range(N_HEADS):
            nk_ref[0, hd] = zk[:, hd * HEAD_W:(hd + 1) * HEAD_W]
            nv_ref[0, hd] = zv[:, hd * HEAD_W:(hd + 1) * HEAD_W]

    ug_ref[...] = _gelu(seg(3)).astype(BF16)
    vg_ref[...] = _layer_norm(_gelu(seg(4)), lg_ref[...], lb_ref[...]).astype(BF16)
    sa_ref[...] = jax.nn.sigmoid(seg(5) + bg_ref[0:1, :]).astype(BF16)
    sb_ref[...] = jax.nn.sigmoid(seg(6) + bg_ref[1:2, :]).astype(BF16)


def _input_projection(xp, xs, mod3, w_in, cos_t, sin_t, b_gate, ln_g, ln_b):
    tile = lambda i: (i, 0)
    head_blk = (1, N_HEADS, TM, HEAD_W)
    tok_spec = pl.BlockSpec((TM, D_MODEL), tile)
    return pl.pallas_call(
        _proj_kernel,
        grid=(NT,),
        in_specs=[
            pl.BlockSpec((TM, D_MODEL), lambda i: (_ctx_idx(i), 0)),
            pl.BlockSpec((TM, D_MODEL), lambda i: (_lat_idx(i), 0)),
            pl.BlockSpec((1, 1, 6 * D_MODEL), lambda i: (_mod_row(i), 0, 0)),
            pl.BlockSpec((D_MODEL, N_SEG * D_MODEL), lambda i: (0, 0)),
            pl.BlockSpec((TM, LANES), lambda i: (jnp.where(i < NT_CTX, 0, 1 + (i - NT_CTX) % LAT_TILES), 0)),
            pl.BlockSpec((TM, LANES), lambda i: (jnp.where(i < NT_CTX, 0, 1 + (i - NT_CTX) % LAT_TILES), 0)),
            pl.BlockSpec((2, D_MODEL), lambda i: (0, 0)),
            pl.BlockSpec((1, D_MODEL), lambda i: (0, 0)),
            pl.BlockSpec((1, D_MODEL), lambda i: (0, 0)),
        ],
        out_specs=[
            pl.BlockSpec(head_blk, lambda i: (i, 0, 0, 0)),
            pl.BlockSpec(head_blk, lambda i: (i, 0, 0, 0)),
            pl.BlockSpec(head_blk, lambda i: (i, 0, 0, 0)),
            pl.BlockSpec(head_blk, lambda i: (_ctx_idx(i), 0, 0, 0)),
            pl.BlockSpec(head_blk, lambda i: (_ctx_idx(i), 0, 0, 0)),
            tok_spec, tok_spec, tok_spec, tok_spec,
        ],
        out_shape=[
            jax.ShapeDtypeStruct((NT, N_HEADS, TM, HEAD_W), BF16),
            jax.ShapeDtypeStruct((NT, N_HEADS, TM, HEAD_W), BF16),
            jax.ShapeDtypeStruct((NT, N_HEADS, TM, HEAD_W), BF16),
            jax.ShapeDtypeStruct((NT_CTX, N_HEADS, TM, HEAD_W), F32),
            jax.ShapeDtypeStruct((NT_CTX, N_HEADS, TM, HEAD_W), F32),
            jax.ShapeDtypeStruct((T_ALL, D_MODEL), BF16),
            jax.ShapeDtypeStruct((T_ALL, D_MODEL), BF16),
            jax.ShapeDtypeStruct((T_ALL, D_MODEL), BF16),
            jax.ShapeDtypeStruct((T_ALL, D_MODEL), BF16),
        ],
        compiler_params=_params(("arbitrary",)),
        name="input_projection",
    )(xp, xs, mod3, w_in, cos_t, sin_t, b_gate, ln_g, ln_b)


def _attn_kernel(n_main, has_ctx, lam_ref, q_ref, k_ref, v_ref, *rest):
    if has_ctx:
        ck_ref, cv_ref, sub_ref, o_ref, kall, vt_all = rest
    else:
        sub_ref, o_ref, kall, vt_all = rest
    n_heads, n_keys = kall.shape[0], kall.shape[1]

    @pl.when(pl.program_id(2) == 0)
    def _():
        r = lax.broadcasted_iota(jnp.int32, (ONES_ROWS, n_keys), 0)
        for j in range(n_heads):
            for c in range(n_main):
                kall[j, c * TM:(c + 1) * TM, :] = k_ref[c, j]
                vt_all[j, 0:HEAD_W, c * TM:(c + 1) * TM] = v_ref[c, j].astype(F32).T.astype(BF16)
            if has_ctx:
                kall[j, n_main * TM:n_keys, :] = ck_ref[0, j].astype(BF16)
                vt_all[j, 0:HEAD_W, n_main * TM:n_keys] = cv_ref[0, j].T.astype(BF16)
            vt_all[j, HEAD_W:HEAD_W + ONES_ROWS, :] = jnp.where(r == 0, 1.0, 0.0).astype(BF16)

    tq = q_ref.shape[0] * TM
    lane = lax.broadcasted_iota(jnp.int32, (tq, HEAD_W), 1)
    qqs = []
    for j in range(n_heads):
        q = q_ref[:, j].reshape(tq, HEAD_W)
        zero = jnp.zeros_like(q)
        qqs.append(jnp.concatenate([jnp.where(lane < HEAD_DIM, q, zero),
                                    jnp.where(lane >= HEAD_DIM, q, zero)], axis=0))

    def scores(j, st, sz):
        return lax.dot_general(kall[j, st:st + sz, :], qqs[j], (((1,), (1,)), ((), ())),
                               preferred_element_type=F32)

    chunks = [(st, min(KEY_CHUNK, n_keys - st)) for st in range(0, n_keys, KEY_CHUNK)]
    pending = [[scores(j, *ch) for ch in chunks[:SCORE_LEAD]] for j in range(n_heads)]
    m = [None] * n_heads
    acc = [None] * n_heads
    for c, (st, sz) in enumerate(chunks):
        for j in range(n_heads):
            if c + SCORE_LEAD < len(chunks):
                pending[j].append(scores(j, *chunks[c + SCORE_LEAD]))
            s = pending[j].pop(0)
            m_c = jnp.max(s, axis=0, keepdims=True)
            m_new = m_c if c == 0 else jnp.maximum(m[j], m_c)
            e = jnp.exp2(s - m_new).astype(BF16)
            pv = jnp.dot(vt_all[j, :, st:st + sz], e, preferred_element_type=F32)
            acc[j] = pv if c == 0 else jnp.exp2(m[j] - m_new) * acc[j] + pv
            m[j] = m_new
    for j in range(n_heads):
        l = acc[j][HEAD_W:HEAD_W + 1, :]
        c1 = 1.0 / l[:, :tq]
        c2 = lam_ref[0] / l[:, tq:]
        a = (acc[j][:HEAD_W, :tq] * c1 - acc[j][:HEAD_W, tq:] * c2).T
        ms = jnp.mean(a * a, axis=-1, keepdims=True)
        o_ref[:, j * HEAD_W:(j + 1) * HEAD_W] = (a * lax.rsqrt(ms + LN_EPS) * sub_ref[...]).astype(BF16)


def _attention(lam, q_all, k_all, v_all, ctx_k, ctx_v, subw, latent):
    if latent:
        n_b, n_q, n_main, base, qt = N_LAT_B, LAT_TILES // Q_TILES, LAT_TILES, NT_CTX, Q_TILES
        n_keys, hp = LAT_LEN + PAST_LEN, HEADS_PER_STEP
    else:
        n_b, n_q, n_main, base, qt = N_CTX_B, 1, 1, 0, 1
        n_keys, hp = CTX_LEN, N_HEADS
    q_spec = pl.BlockSpec((qt, hp, TM, HEAD_W), lambda b, h, t: (base // qt + b * n_q + t, h, 0, 0))
    kv_spec = pl.BlockSpec((n_main, hp, TM, HEAD_W), lambda b, h, t: (base // n_main + b, h, 0, 0))
    in_specs = [pl.BlockSpec(memory_space=pltpu.SMEM), q_spec, kv_spec, kv_spec]
    args = [lam, q_all, k_all, v_all]
    if latent:
        c_spec = pl.BlockSpec((1, hp, PAST_LEN, HEAD_W), lambda b, h, t: (b, h, 0, 0))
        in_specs += [c_spec, c_spec]
        args += [ctx_k, ctx_v]
    in_specs.append(pl.BlockSpec((1, HEAD_W), lambda b, h, t: (0, 0)))
    args.append(subw)
    return pl.pallas_call(
        functools.partial(_attn_kernel, n_main, latent),
        grid=(n_b, N_HEADS // hp, n_q),
        in_specs=in_specs,
        out_specs=pl.BlockSpec((qt * TM, hp * HEAD_W), lambda b, h, t: (b * n_q + t, h)),
        out_shape=jax.ShapeDtypeStruct((n_b * n_q * qt * TM, D_MODEL), BF16),
        scratch_shapes=[pltpu.VMEM((hp, n_keys, HEAD_W), BF16),
                        pltpu.VMEM((hp, HEAD_W + ONES_ROWS, n_keys), BF16)],
        compiler_params=_params(("arbitrary", "arbitrary", "arbitrary")),
        name="diff_attention_lat" if latent else "diff_attention_ctx",
    )(*args)


def _mix_kernel(ac_ref, al_ref, ug_ref, vg_ref, sa_ref, sb_ref, xp_ref, xs_ref, mod_ref,
                ws_ref, bs_ref, wpa_ref, wpb_ref, wo_ref, g_ref, b_ref, wrc_ref, br_ref,
                x1_ref, h2_ref, ti_ref, tg_ref, cnt_ref, run_ref):
    i = pl.program_id(0)
    is_ctx = i < NT_CTX
    a = jnp.where(is_ctx, ac_ref[...], al_ref[...])
    x = jnp.where(is_ctx, xp_ref[...], xs_ref[...])
    mod = mod_ref[0]
    g1 = mod[:, 2 * D_MODEL:3 * D_MODEL]
    sh2 = mod[:, 3 * D_MODEL:4 * D_MODEL]
    sc2 = mod[:, 4 * D_MODEL:5 * D_MODEL]

    gw = D_MODEL // GMLP_GROUPS
    chunks = []
    for c in range(TM // CHUNK):
        groups = []
        for g in range(GMLP_GROUPS):
            vc = vg_ref[c * CHUNK:(c + 1) * CHUNK, g * gw:(g + 1) * gw]
            groups.append(jnp.dot(ws_ref[g], vc, preferred_element_type=F32))
        chunks.append(jnp.concatenate(groups, axis=1) + bs_ref[...])
    sp = jnp.concatenate(chunks, axis=0)
    gm = (ug_ref[...].astype(F32) * sp).astype(BF16)

    pa = jnp.dot(a, wpa_ref[...], preferred_element_type=F32)
    pb = jnp.dot(gm, wpb_ref[...], preferred_element_type=F32)
    merged = (sa_ref[...].astype(F32) * pa + sb_ref[...].astype(F32) * pb).astype(BF16)
    mix = jnp.dot(merged, wo_ref[...], preferred_element_type=F32)
    x1 = _layer_norm(ALPHA * x + g1 * mix, g_ref[...], b_ref[...])
    x1_ref[...] = x1
    h2 = x1 * (1.0 + sc2) + sh2
    hi, lo = _split(h2)
    h2_ref[...] = _pack_pairs(h2)
    both = jnp.dot(hi, wrc_ref[...], preferred_element_type=F32)
    logits = (both[:, :LANES] + jnp.dot(lo, wrc_ref[:, :LANES], preferred_element_type=F32)
              + both[:, LANES:]) + br_ref[...]
    lane = lax.broadcasted_iota(jnp.int32, (TM, LANES), 1).astype(F32)
    vals, idxs = [], []
    for _ in range(TOP_K):
        mx = jnp.max(logits, axis=-1, keepdims=True)
        ix = jnp.min(jnp.where(logits == mx, lane, float(LANES)), axis=-1, keepdims=True)
        vals.append(mx)
        idxs.append(ix)
        logits = jnp.where(lane == ix, NEG_BIG * 2.0, logits)
    es = [jnp.exp(v - vals[0]) for v in vals]
    inv = 1.0 / (es[0] + es[1] + es[2] + es[3])

    @pl.when(i == 0)
    def _():
        run_ref[...] = jnp.zeros_like(run_ref)

    hot = [lane == ix for ix in idxs]
    memb = jnp.zeros((TM, LANES), F32)
    for hk in hot:
        memb = jnp.where(hk, 1.0, memb)
    row = lax.broadcasted_iota(jnp.int32, (TM, TM), 0)
    col = lax.broadcasted_iota(jnp.int32, (TM, TM), 1)
    before = jnp.where(row > col, 1.0, 0.0).astype(BF16)
    base = run_ref[...]
    rank_all = jnp.dot(before, memb.astype(BF16), preferred_element_type=F32) + base
    total = base + jnp.sum(memb, axis=0, keepdims=True)
    run_ref[...] = total
    cnt_ref[...] = total

    ti = jnp.zeros((TM, LANES), F32)
    tg = jnp.zeros((TM, LANES), F32)
    for k in range(TOP_K):
        rank_k = jnp.sum(jnp.where(hot[k], rank_all, 0.0), axis=-1, keepdims=True)
        ti = jnp.where(lane == float(k), idxs[k], ti)
        ti = jnp.where(lane == float(TOP_K + k), rank_k, ti)
        tg = jnp.where(lane == float(k), es[k] * inv, tg)
    ti_ref[...] = ti.astype(jnp.int32)
    tg_ref[...] = tg


def _mix(a_ctx, a_lat, ug, vg, sa, sb, xp, xs, mod3, ws, bs, wpa, wpb, wo, g, b, wrc, br):
    tile = lambda i: (i, 0)
    full2 = lambda i: (0, 0)
    tok = pl.BlockSpec((TM, D_MODEL), tile)
    ctx = pl.BlockSpec((TM, D_MODEL), lambda i: (_ctx_idx(i), 0))
    lat = pl.BlockSpec((TM, D_MODEL), lambda i: (_lat_idx(i), 0))
    wsq = pl.BlockSpec((D_MODEL, D_MODEL), full2)
    vec = pl.BlockSpec((1, D_MODEL), full2)
    return pl.pallas_call(
        _mix_kernel,
        grid=(NT,),
        in_specs=[ctx, lat, tok, tok, tok, tok, ctx, lat,
                  pl.BlockSpec((1, 1, 6 * D_MODEL), lambda i: (_mod_row(i), 0, 0)),
                  pl.BlockSpec((GMLP_GROUPS, CHUNK, CHUNK), lambda i: (0, 0, 0)),
                  pl.BlockSpec((CHUNK, D_MODEL), full2),
                  wsq, wsq, wsq, vec, vec,
                  pl.BlockSpec((D_MODEL, 2 * LANES), full2),
                  pl.BlockSpec((1, LANES), full2)],
        out_specs=[tok, pl.BlockSpec((TM, PACK_W), tile),
                   pl.BlockSpec((TM, LANES), tile), pl.BlockSpec((TM, LANES), tile),
                   pl.BlockSpec((1, LANES), full2)],
        out_shape=[jax.ShapeDtypeStruct((T_ALL, D_MODEL), F32),
                   jax.ShapeDtypeStruct((T_ALL, PACK_W), jnp.uint32),
                   jax.ShapeDtypeStruct((T_ALL, LANES), jnp.int32),
                   jax.ShapeDtypeStruct((T_ALL, LANES), F32),
                   jax.ShapeDtypeStruct((1, LANES), F32)],
        scratch_shapes=[pltpu.VMEM((1, LANES), F32)],
        compiler_params=_params(("arbitrary",)),
        name="mix_ln1_router",
    )(a_ctx, a_lat, ug, vg, sa, sb, xp, xs, mod3, ws, bs, wpa, wpb, wo, g, b, wrc, br)


def _dispatch_kernel(dest_ref, pend_ref, h_ref, out_hbm, zeros, sem):
    @pl.when(pl.program_id(0) == 0)
    def _():
        zeros[...] = jnp.zeros_like(zeros)

        def fill(start):
            return pltpu.make_async_copy(
                zeros, out_hbm.at[pl.ds(pl.multiple_of(start, MOE_BLOCK), MOE_BLOCK)], sem)

        fills = []
        prev = 0
        for e in range(N_EXPERTS):
            end = pend_ref[e]
            fills.append((end > prev, end - MOE_BLOCK))
            prev = end
        for j in range(N_EXPERTS):
            start = prev + j * MOE_BLOCK
            fills.append((start < out_hbm.shape[0], start))
        for needed, start in fills:
            pl.when(needed)(lambda start=start: fill(start).start())
        for needed, start in fills:
            pl.when(needed)(lambda start=start: fill(start).wait())

    def issue(g, carry):
        base = g * (SUBLANES * TOP_K)
        for u in range(SUBLANES):
            for k in range(TOP_K):
                dest = dest_ref[base + (u * TOP_K + k)]
                pltpu.make_async_copy(h_ref.at[g, pl.ds(u, 1)], out_hbm.at[pl.ds(dest, 1)],
                                      sem).start(priority=k % 2)
        return carry

    lax.fori_loop(0, h_ref.shape[0], issue, 0)
    for _ in range(h_ref.shape[0] * SUBLANES * TOP_K // TM):
        pltpu.make_async_copy(out_hbm.at[pl.ds(0, TM)], out_hbm.at[pl.ds(0, TM)], sem).wait()


def _dispatch(dest_flat, pend, h2p, n_rows):
    return pl.pallas_call(
        _dispatch_kernel,
        grid=(NT // DISPATCH_TILES,),
        in_specs=[pl.BlockSpec((DISPATCH_TILES * ROWS_PER_TILE,), lambda i: (i,),
                               memory_space=pltpu.SMEM),
                  pl.BlockSpec(memory_space=pltpu.SMEM),
                  pl.BlockSpec((DISPATCH_TILES * TM // SUBLANES, SUBLANES, PACK_W),
                               lambda i: (i, 0, 0))],
        out_specs=pl.BlockSpec(memory_space=pl.ANY),
        out_shape=jax.ShapeDtypeStruct((n_rows, PACK_W), jnp.uint32),
        scratch_shapes=[pltpu.VMEM((MOE_BLOCK, PACK_W), jnp.uint32),
                        pltpu.SemaphoreType.DMA(())],
        compiler_params=_params(("arbitrary",)),
        name="moe_dispatch",
    )(dest_flat, pend, h2p.reshape(T_ALL // SUBLANES, SUBLANES, PACK_W))


def _expert_kernel(be_ref, nu_ref, x_ref, wu_ref, bu_ref, wd_ref, bd_ref, y_ref, wu_bf, wd_bf):
    i = pl.program_id(0)
    used = i < nu_ref[0]
    new_expert = jnp.logical_or(i == 0, be_ref[i] != be_ref[jnp.maximum(i - 1, 0)])

    @pl.when(jnp.logical_and(used, new_expert))
    def _():
        wu_bf[...] = wu_ref[0].astype(BF16)
        wd_bf[...] = wd_ref[0].astype(BF16)

    @pl.when(used)
    def _():
        x = _unpack_pairs(x_ref[...]).astype(BF16)
        hu = jnp.dot(x, wu_bf[...], preferred_element_type=F32) + bu_ref[0]
        glu = jnp.minimum(hu[:, :D_EXPERT], SWIGLU_LIMIT)
        lin = jnp.clip(hu[:, D_EXPERT:], -SWIGLU_LIMIT, SWIGLU_LIMIT)
        act = glu * jax.nn.sigmoid(SWIGLU_ALPHA * glu) * (lin + 1.0)
        y = jnp.dot(act.astype(BF16), wd_bf[...], preferred_element_type=F32) + bd_ref[0]
        y_ref[...] = _pack_pairs(y)

    @pl.when(jnp.logical_not(used))
    def _():
        y_ref[...] = jnp.zeros_like(y_ref)


def _experts(blk_exp, n_used, xb, w_up, b_up, w_down, b_down):
    n_blocks = xb.shape[0] // MOE_BLOCK
    grid_spec = pltpu.PrefetchScalarGridSpec(
        num_scalar_prefetch=2,
        grid=(n_blocks,),
        in_specs=[pl.BlockSpec((MOE_BLOCK, PACK_W), lambda i, be, nu: (jnp.where(i < nu[0], i, 0), 0)),
                  pl.BlockSpec((1, D_MODEL, 2 * D_EXPERT), lambda i, be, nu: (be[i], 0, 0)),
                  pl.BlockSpec((1, 1, 2 * D_EXPERT), lambda i, be, nu: (be[i], 0, 0)),
                  pl.BlockSpec((1, D_EXPERT, D_MODEL), lambda i, be, nu: (be[i], 0, 0)),
                  pl.BlockSpec((1, 1, D_MODEL), lambda i, be, nu: (be[i], 0, 0))],
        out_specs=pl.BlockSpec((MOE_BLOCK, PACK_W), lambda i, be, nu: (i, 0)),
        scratch_shapes=[pltpu.VMEM((D_MODEL, 2 * D_EXPERT), BF16),
                        pltpu.VMEM((D_EXPERT, D_MODEL), BF16)],
    )
    return pl.pallas_call(
        _expert_kernel,
        grid_spec=grid_spec,
        out_shape=jax.ShapeDtypeStruct((n_blocks * MOE_BLOCK, PACK_W), jnp.uint32),
        compiler_params=_params(("arbitrary",)),
        name="expert_ffn",
    )(blk_exp, n_used, xb, w_up, b_up, w_down, b_down)


def _final_kernel(dcur_ref, dnxt_ref, x1_ref, tg_ref, mod_ref, g_ref, b_ref, yb_hbm,
                  yp_ref, ys_ref, ybuf, sem):
    i = pl.program_id(0)

    def gather(dest_ref, slot):
        def issue(g, carry):
            base = g * (SUBLANES * TOP_K)
            for u in range(SUBLANES):
                for k in range(TOP_K):
                    dest = dest_ref[base + (u * TOP_K + k)]
                    pltpu.make_async_copy(yb_hbm.at[pl.ds(dest, 1)],
                                          ybuf.at[slot, k, g, pl.ds(u, 1)],
                                          sem.at[slot]).start(priority=k % 2)
            return carry
        lax.fori_loop(0, TM // SUBLANES, issue, 0)

    def wait_rows(slot):
        for k in range(TOP_K):
            pltpu.make_async_copy(yb_hbm.at[pl.ds(0, TM)], yb_hbm.at[pl.ds(0, TM)],
                                  sem.at[slot]).wait()

    def combine(slot):
        g2 = mod_ref[0][:, 5 * D_MODEL:6 * D_MODEL]
        gates = tg_ref[...]
        ffn = jnp.zeros((TM, D_MODEL), F32)
        for k in range(TOP_K):
            yk = ybuf[slot, k].reshape(TM, PACK_W)
            ffn = ffn + gates[:, k:k + 1] * _unpack_pairs(yk)
        y = _layer_norm(ALPHA * x1_ref[...] + g2 * ffn, g_ref[...], b_ref[...])

        @pl.when(i < NT_CTX)
        def _():
            yp_ref[...] = y

        @pl.when(i >= NT_CTX)
        def _():
            ys_ref[...] = y

    @pl.when(i == 0)
    def _():
        gather(dcur_ref, 0)

    for slot in range(2):
        @pl.when(i % 2 == slot)
        def _():
            @pl.when(i + 1 < NT)
            def _():
                gather(dnxt_ref, 1 - slot)
            wait_rows(slot)
            combine(slot)


def _final(dest_flat, x1, top_g, mod3, g, b, ybp):
    tok = pl.BlockSpec((TM, D_MODEL), lambda i: (i, 0))
    vec = pl.BlockSpec((1, D_MODEL), lambda i: (0, 0))
    return pl.pallas_call(
        _final_kernel,
        grid=(NT,),
        in_specs=[pl.BlockSpec((ROWS_PER_TILE,), lambda i: (i,), memory_space=pltpu.SMEM),
                  pl.BlockSpec((ROWS_PER_TILE,), lambda i: (jnp.minimum(i + 1, NT - 1),),
                               memory_space=pltpu.SMEM),
                  tok, pl.BlockSpec((TM, LANES), lambda i: (i, 0)),
                  pl.BlockSpec((1, 1, 6 * D_MODEL), lambda i: (_mod_row(i), 0, 0)), vec, vec,
                  pl.BlockSpec(memory_space=pl.ANY)],
        scratch_shapes=[pltpu.VMEM((2, TOP_K, TM // SUBLANES, SUBLANES, PACK_W), jnp.uint32),
                        pltpu.SemaphoreType.DMA((2,))],
        out_specs=[pl.BlockSpec((TM, D_MODEL), lambda i: (_ctx_idx(i), 0)),
                   pl.BlockSpec((TM, D_MODEL), lambda i: (_lat_idx(i), 0))],
        out_shape=[jax.ShapeDtypeStruct((NT_CTX * TM, D_MODEL), F32),
                   jax.ShapeDtypeStruct((NT_LAT * TM, D_MODEL), F32)],
        compiler_params=_params(("arbitrary",)),
        name="residual_ln2",
    )(dest_flat, dest_flat, x1, top_g, mod3, g, b, ybp)


def _rope_tables():
    t = jnp.arange(LAT_LEN)
    r = (t // GRID_W).astype(F32)
    col = (t % GRID_W).astype(F32)
    inv = jnp.power(ROPE_BASE, -jnp.arange(ROPE_HALF, dtype=F32) / ROPE_HALF)
    ang_r = r[:, None] * inv
    ang_c = col[:, None] * inv
    ang = jnp.concatenate([ang_r, ang_r, ang_c, ang_c], axis=-1)
    cos = jnp.tile(jnp.cos(ang), (1, 2))
    sin = jnp.tile(jnp.sin(ang), (1, 2))
    sign = jnp.where((jnp.arange(LANES) % (2 * ROPE_HALF)) < ROPE_HALF, -1.0, 1.0).astype(F32)
    cos = jnp.concatenate([jnp.ones((TM, LANES), F32), cos], axis=0)
    sin = jnp.concatenate([jnp.zeros((TM, LANES), F32), sin * sign], axis=0)
    return cos, sin


def _routing(top_idx, rank, counts):
    n_tok = top_idx.shape[0]
    n_assign = n_tok * TOP_K
    padded = (counts + MOE_BLOCK - 1) // MOE_BLOCK * MOE_BLOCK
    pend = jnp.cumsum(padded)
    pstart = pend - padded
    experts = jnp.arange(N_EXPERTS, dtype=jnp.int32)
    dest_tk = rank + jnp.sum(jnp.where(top_idx[:, :, None] == experts, pstart, 0), axis=-1)
    n_blocks = -(-n_assign // MOE_BLOCK) + N_EXPERTS
    blk_start = jnp.arange(n_blocks, dtype=jnp.int32) * MOE_BLOCK
    blk_exp = jnp.minimum(jnp.sum((pend[None, :] <= blk_start[:, None]).astype(jnp.int32), axis=1),
                          N_EXPERTS - 1)
    n_used = (pend[-1] // MOE_BLOCK).astype(jnp.int32).reshape(1)
    return blk_exp, dest_tk.reshape(n_assign), n_used, pend.astype(jnp.int32), n_blocks * MOE_BLOCK


def kernel(x_prompt, x_sample, c, cache_k, cache_v, c_ctx, w_ada, b_ada, w_in, lambda_q1, lambda_k1, lambda_q2, lambda_k2, subln_w, gmlp_ln_g, gmlp_ln_b, w_spatial, b_spatial, b_gate, w_pa, w_pb, w_o, ln1_g, ln1_b, w_router, b_router, w_up, b_up, w_down, b_down, ln2_g, ln2_b):
    l = 0
    xp = x_prompt.reshape(N_CTX_B * CTX_LEN, D_MODEL)
    xs = x_sample.reshape(N_LAT_B * LAT_LEN, D_MODEL)

    cond = jnp.concatenate([c_ctx[None, :], c, jnp.zeros((16 - 1 - N_LAT_B, D_MODEL), F32)], axis=0)
    mod3 = _modulation(cond, w_ada[l], b_ada[l][None, :]).reshape(16, 1, 6 * D_MODEL)

    cos_t, sin_t = _rope_tables()
    (q_all, k_all, v_all, new_k, new_v, ug, vg, sa, sb) = _input_projection(
        xp, xs, mod3, w_in[l].astype(BF16), cos_t, sin_t, b_gate[l],
        gmlp_ln_g[l][None, :], gmlp_ln_b[l][None, :])

    lam = (jnp.exp(jnp.sum(lambda_q1[l] * lambda_k1[l])) - jnp.exp(jnp.sum(lambda_q2[l] * lambda_k2[l]))
           + LAM_INIT).reshape(1).astype(F32)
    subw = (subln_w[l] * (1.0 - LAM_INIT))[None, :]
    ctx_k = cache_k[:, l].reshape(N_LAT_B, N_HEADS, PAST_LEN, HEAD_W)
    ctx_v = cache_v[:, l]
    a_ctx = _attention(lam, q_all, k_all, v_all, None, None, subw, latent=False)
    a_lat = _attention(lam, q_all, k_all, v_all, ctx_k, ctx_v, subw, latent=True)

    bs_full = jnp.repeat(b_spatial[l].T, D_MODEL // GMLP_GROUPS, axis=1)
    wr = jnp.pad(w_router[l], ((0, 0), (0, LANES - N_EXPERTS)))
    wr_hi = wr.astype(BF16)
    wr_lo = (wr - wr_hi.astype(F32)).astype(BF16)
    br = jnp.concatenate([b_router[l], jnp.full((LANES - N_EXPERTS,), NEG_BIG, F32)])[None, :]
    x1, h2, top_i, top_g, cnt = _mix(
        a_ctx, a_lat, ug, vg, sa, sb, xp, xs, mod3, w_spatial[l].astype(BF16), bs_full,
        w_pa[l].astype(BF16), w_pb[l].astype(BF16), w_o[l].astype(BF16),
        ln1_g[l][None, :], ln1_b[l][None, :], jnp.concatenate([wr_hi, wr_lo], axis=1), br)

    blk_exp, dest_flat, n_used, pend, n_rows = _routing(
        top_i[:, :TOP_K], top_i[:, TOP_K:2 * TOP_K], cnt[0, :N_EXPERTS].astype(jnp.int32))
    xb = _dispatch(dest_flat, pend, h2, n_rows)
    yb = _experts(blk_exp, n_used, xb, w_up[l], b_up[l][:, None, :], w_down[l], b_down[l][:, None, :])
    y_prompt, y_sample = _final(dest_flat, x1, top_g, mod3, ln2_g[l][None, :], ln2_b[l][None, :], yb)
    return (y_prompt.reshape(N_CTX_B, CTX_LEN, D_MODEL),
            y_sample.reshape(N_LAT_B, LAT_LEN, D_MODEL),
            new_k.reshape(N_CTX_B, DEPTH, N_HEADS, CTX_LEN, 2, HEAD_DIM),
            new_v.reshape(N_CTX_B, DEPTH, N_HEADS, CTX_LEN, HEAD_W))
```

```python
import functools
import math

import jax
import jax.numpy as jnp
from jax import lax
from jax.experimental import pallas as pl
from jax.experimental.pallas import tpu as pltpu

F32 = jnp.float32
BF16 = jnp.bfloat16

D_MODEL = 1024
N_CTX_B = 16
CTX_LEN = 256
N_LAT_B = 8
LAT_LEN = 4096
PAST_LEN = 256
GRID_W = 64
N_HEADS = 8
HEAD_DIM = 64
HEAD_W = 2 * HEAD_DIM
ROPE_HALF = HEAD_DIM // 4
ROPE_BASE = 10000.0
GMLP_GROUPS = 4
CHUNK = 128
N_EXPERTS = 32
TOP_K = 4
D_EXPERT = 1024
SWIGLU_LIMIT = 7.0
SWIGLU_ALPHA = 1.702
MOE_BLOCK = 1024
LN_EPS = 1e-5
DEPTH = 1
N_SEG = 7

TM = 256
NT_CTX = N_CTX_B * CTX_LEN // TM
NT_LAT = N_LAT_B * LAT_LEN // TM
NT = NT_CTX + NT_LAT
T_ALL = NT * TM
LAT_TILES = LAT_LEN // TM
LANES = 128
KEY_CHUNK = 512
SCORE_LEAD = 1
Q_TILES = 2
HEADS_PER_STEP = 2
ONES_ROWS = 16
PACK_W = D_MODEL // 2
ROWS_PER_TILE = TM * TOP_K
SUBLANES = 8
DISPATCH_TILES = 4
NEG_BIG = -1e30
VMEM_LIMIT = 56 * 1024 * 1024

ALPHA = (2.0 * DEPTH) ** 0.25
LAM_INIT = 0.8 - 0.6 * math.exp(-0.3 * 0)
Q_SCALE = HEAD_DIM ** -0.5 * math.log2(math.e)


def _ctx_idx(i):
    return jnp.minimum(i, NT_CTX - 1)


def _lat_idx(i):
    return jnp.maximum(i - NT_CTX, 0)


def _mod_row(i):
    return jnp.where(i < NT_CTX, 0, 1 + (i - NT_CTX) // LAT_TILES)


def _layer_norm(x, g, b):
    mu = jnp.mean(x, axis=-1, keepdims=True)
    xc = x - mu
    var = jnp.mean(xc * xc, axis=-1, keepdims=True)
    return xc * lax.rsqrt(var + LN_EPS) * g + b


def _gelu(x):
    return 0.5 * x * (1.0 + lax.erf(x * (1.0 / math.sqrt(2.0))))


def _split(x):
    hi = x.astype(BF16)
    return hi, (x - hi.astype(F32)).astype(BF16)


def _pack_pairs(x):
    w = x.shape[1] // 2
    a = pltpu.bitcast(x[:, :w].astype(BF16).astype(F32), jnp.uint32)
    b = pltpu.bitcast(x[:, w:].astype(BF16).astype(F32), jnp.uint32)
    return lax.shift_right_logical(a, jnp.uint32(16)) | b


def _unpack_pairs(p):
    a = pltpu.bitcast(lax.shift_left(p, jnp.uint32(16)), F32)
    b = pltpu.bitcast(p & jnp.uint32(0xFFFF0000), F32)
    return jnp.concatenate([a, b], axis=1)


def _params(sem):
    return pltpu.CompilerParams(dimension_semantics=sem, vmem_limit_bytes=VMEM_LIMIT)


def _mod_kernel(c_ref, w_ref, b_ref, o_ref):
    c = c_ref[...]
    s = c * jax.nn.sigmoid(c)
    s_hi, s_lo = _split(s)
    w_hi, w_lo = _split(w_ref[...])
    o_ref[...] = (jnp.dot(s_hi, w_hi, preferred_element_type=F32)
                  + jnp.dot(s_lo, w_hi, preferred_element_type=F32)
                  + jnp.dot(s_hi, w_lo, preferred_element_type=F32)) + b_ref[...]


def _modulation(cond, w_ada, b_ada):
    n = cond.shape[0]
    return pl.pallas_call(
        _mod_kernel,
        grid=(6,),
        in_specs=[pl.BlockSpec((n, D_MODEL), lambda j: (0, 0)),
                  pl.BlockSpec((D_MODEL, D_MODEL), lambda j: (0, j)),
                  pl.BlockSpec((1, D_MODEL), lambda j: (0, j))],
        out_specs=pl.BlockSpec((n, D_MODEL), lambda j: (0, j)),
        out_shape=jax.ShapeDtypeStruct((n, 6 * D_MODEL), F32),
        compiler_params=_params(("arbitrary",)),
        name="adaln_mod",
    )(cond, w_ada, b_ada)


def _proj_kernel(xp_ref, xs_ref, mod_ref, w_ref, cos_ref, sin_ref, bg_ref, lg_ref, lb_ref,
                 q_ref, k_ref, v_ref, nk_ref, nv_ref, ug_ref, vg_ref, sa_ref, sb_ref):
    i = pl.program_id(0)
    is_ctx = i < NT_CTX
    x = jnp.where(is_ctx, xp_ref[...], xs_ref[...])
    mod = mod_ref[0]
    sh1 = mod[:, 0:D_MODEL]
    sc1 = mod[:, D_MODEL:2 * D_MODEL]
    h = (x * (1.0 + sc1) + sh1).astype(BF16)

    def seg(s):
        return jnp.dot(h, w_ref[:, s * D_MODEL:(s + 1) * D_MODEL], preferred_element_type=F32)

    cos = cos_ref[...]
    sin = sin_ref[...]
    lane = lax.broadcasted_iota(jnp.int32, (TM, LANES), 1)
    first = (lane % (2 * ROPE_HALF)) < ROPE_HALF

    def rope(zh):
        up = pltpu.roll(zh, LANES - ROPE_HALF, 1)
        dn = pltpu.roll(zh, ROPE_HALF, 1)
        return zh * cos + jnp.where(first, up, dn) * sin

    zq = seg(0)
    for hd in range(N_HEADS):
        q_ref[0, hd] = (rope(zq[:, hd * HEAD_W:(hd + 1) * HEAD_W]) * Q_SCALE).astype(BF16)
    zk = seg(1)
    for hd in range(N_HEADS):
        k_ref[0, hd] = rope(zk[:, hd * HEAD_W:(hd + 1) * HEAD_W]).astype(BF16)
    zv = seg(2)
    for hd in range(N_HEADS):
        v_ref[0, hd] = zv[:, hd * HEAD_W:(hd + 1) * HEAD_W].astype(BF16)

    @pl.when(is_ctx)
    def _():
        for hd in range(N_HEADS):
            nk_ref[0, hd] = zk[:, hd * HEAD_W:(hd + 1) * HEAD_W]
            nv_ref[0, hd] = zv[:, hd * HEAD_W:(hd + 1) * HEAD_W]

    ug_ref[...] = _gelu(seg(3)).astype(BF16)
    vg_ref[...] = _layer_norm(_gelu(seg(4)), lg_ref[...], lb_ref[...]).astype(BF16)
    sa_ref[...] = jax.nn.sigmoid(seg(5) + bg_ref[0:1, :]).astype(BF16)
    sb_ref[...] = jax.nn.sigmoid(seg(6) + bg_ref[1:2, :]).astype(BF16)


def _input_projection(xp, xs, mod3, w_in, cos_t, sin_t, b_gate, ln_g, ln_b):
    tile = lambda i: (i, 0)
    head_blk = (1, N_HEADS, TM, HEAD_W)
    tok_spec = pl.BlockSpec((TM, D_MODEL), tile)
    return pl.pallas_call(
        _proj_kernel,
        grid=(NT,),
        in_specs=[
            pl.BlockSpec((TM, D_MODEL), lambda i: (_ctx_idx(i), 0)),
            pl.BlockSpec((TM, D_MODEL), lambda i: (_lat_idx(i), 0)),
            pl.BlockSpec((1, 1, 6 * D_MODEL), lambda i: (_mod_row(i), 0, 0)),
            pl.BlockSpec((D_MODEL, N_SEG * D_MODEL), lambda i: (0, 0)),
            pl.BlockSpec((TM, LANES), lambda i: (jnp.where(i < NT_CTX, 0, 1 + (i - NT_CTX) % LAT_TILES), 0)),
            pl.BlockSpec((TM, LANES), lambda i: (jnp.where(i < NT_CTX, 0, 1 + (i - NT_CTX) % LAT_TILES), 0)),
            pl.BlockSpec((2, D_MODEL), lambda i: (0, 0)),
            pl.BlockSpec((1, D_MODEL), lambda i: (0, 0)),
            pl.BlockSpec((1, D_MODEL), lambda i: (0, 0)),
        ],
        out_specs=[
            pl.BlockSpec(head_blk, lambda i: (i, 0, 0, 0)),
            pl.BlockSpec(head_blk, lambda i: (i, 0, 0, 0)),
            pl.BlockSpec(head_blk, lambda i: (i, 0, 0, 0)),
            pl.BlockSpec(head_blk, lambda i: (_ctx_idx(i), 0, 0, 0)),
            pl.BlockSpec(head_blk, lambda i: (_ctx_idx(i), 0, 0, 0)),
            tok_spec, tok_spec, tok_spec, tok_spec,
        ],
        out_shape=[
            jax.ShapeDtypeStruct((NT, N_HEADS, TM, HEAD_W), BF16),
            jax.ShapeDtypeStruct((NT, N_HEADS, TM, HEAD_W), BF16),
            jax.ShapeDtypeStruct((NT, N_HEADS, TM, HEAD_W), BF16),
            jax.ShapeDtypeStruct((NT_CTX, N_HEADS, TM, HEAD_W), F32),
            jax.ShapeDtypeStruct((NT_CTX, N_HEADS, TM, HEAD_W), F32),
            jax.ShapeDtypeStruct((T_ALL, D_MODEL), BF16),
            jax.ShapeDtypeStruct((T_ALL, D_MODEL), BF16),
            jax.ShapeDtypeStruct((T_ALL, D_MODEL), BF16),
            jax.ShapeDtypeStruct((T_ALL, D_MODEL), BF16),
        ],
        compiler_params=_params(("arbitrary",)),
        name="input_projection",
    )(xp, xs, mod3, w_in, cos_t, sin_t, b_gate, ln_g, ln_b)


def _attn_kernel(n_main, has_ctx, lam_ref, q_ref, k_ref, v_ref, *rest):
    if has_ctx:
        ck_ref, cv_ref, sub_ref, o_ref, kall, vt_all = rest
    else:
        sub_ref, o_ref, kall, vt_all = rest
    n_heads, n_keys = kall.shape[0], kall.shape[1]

    @pl.when(pl.program_id(2) == 0)
    def _():
        r = lax.broadcasted_iota(jnp.int32, (ONES_ROWS, n_keys), 0)
        for j in range(n_heads):
            for c in range(n_main):
                kall[j, c * TM:(c + 1) * TM, :] = k_ref[c, j]
                vt_all[j, 0:HEAD_W, c * TM:(c + 1) * TM] = v_ref[c, j].T
            if has_ctx:
                kall[j, n_main * TM:n_keys, :] = ck_ref[0, j].astype(BF16)
                vt_all[j, 0:HEAD_W, n_main * TM:n_keys] = cv_ref[0, j].T.astype(BF16)
            vt_all[j, HEAD_W:HEAD_W + ONES_ROWS, :] = jnp.where(r == 0, 1.0, 0.0).astype(BF16)

    tq = q_ref.shape[0] * TM
    lane = lax.broadcasted_iota(jnp.int32, (tq, HEAD_W), 1)
    qqs = []
    for j in range(n_heads):
        q = q_ref[:, j].reshape(tq, HEAD_W)
        zero = jnp.zeros_like(q)
        qqs.append(jnp.concatenate([jnp.where(lane < HEAD_DIM, q, zero),
                                    jnp.where(lane >= HEAD_DIM, q, zero)], axis=0))

    def scores(j, st, sz):
        return lax.dot_general(kall[j, st:st + sz, :], qqs[j], (((1,), (1,)), ((), ())),
                               preferred_element_type=F32)

    chunks = [(st, min(KEY_CHUNK, n_keys - st)) for st in range(0, n_keys, KEY_CHUNK)]
    pending = [[scores(j, *ch) for ch in chunks[:SCORE_LEAD]] for j in range(n_heads)]
    m = [None] * n_heads
    acc = [None] * n_heads
    for c, (st, sz) in enumerate(chunks):
        for j in range(n_heads):
            if c + SCORE_LEAD < len(chunks):
                pending[j].append(scores(j, *chunks[c + SCORE_LEAD]))
            s = pending[j].pop(0)
            m_c = jnp.max(s, axis=0, keepdims=True)
            m_new = m_c if c == 0 else jnp.maximum(m[j], m_c)
            e = jnp.exp2(s - m_new).astype(BF16)
            pv = jnp.dot(vt_all[j, :, st:st + sz], e, preferred_element_type=F32)
            acc[j] = pv if c == 0 else jnp.exp2(m[j] - m_new) * acc[j] + pv
            m[j] = m_new
    for j in range(n_heads):
        l = acc[j][HEAD_W:HEAD_W + 1, :]
        c1 = 1.0 / l[:, :tq]
        c2 = lam_ref[0] / l[:, tq:]
        a = (acc[j][:HEAD_W, :tq] * c1 - acc[j][:HEAD_W, tq:] * c2).T
        ms = jnp.mean(a * a, axis=-1, keepdims=True)
        o_ref[:, j * HEAD_W:(j + 1) * HEAD_W] = (a * lax.rsqrt(ms + LN_EPS) * sub_ref[...]).astype(BF16)


def _attention(lam, q_all, k_all, v_all, ctx_k, ctx_v, subw, latent):
    if latent:
        n_b, n_q, n_main, base, qt = N_LAT_B, LAT_TILES // Q_TILES, LAT_TILES, NT_CTX, Q_TILES
        n_keys, hp = LAT_LEN + PAST_LEN, HEADS_PER_STEP
    else:
        n_b, n_q, n_main, base, qt = N_CTX_B, 1, 1, 0, 1
        n_keys, hp = CTX_LEN, N_HEADS
    q_spec = pl.BlockSpec((qt, hp, TM, HEAD_W), lambda b, h, t: (base // qt + b * n_q + t, h, 0, 0))
    kv_spec = pl.BlockSpec((n_main, hp, TM, HEAD_W), lambda b, h, t: (base // n_main + b, h, 0, 0))
    in_specs = [pl.BlockSpec(memory_space=pltpu.SMEM), q_spec, kv_spec, kv_spec]
    args = [lam, q_all, k_all, v_all]
    if latent:
        c_spec = pl.BlockSpec((1, hp, PAST_LEN, HEAD_W), lambda b, h, t: (b, h, 0, 0))
        in_specs += [c_spec, c_spec]
        args += [ctx_k, ctx_v]
    in_specs.append(pl.BlockSpec((1, HEAD_W), lambda b, h, t: (0, 0)))
    args.append(subw)
    return pl.pallas_call(
        functools.partial(_attn_kernel, n_main, latent),
        grid=(n_b, N_HEADS // hp, n_q),
        in_specs=in_specs,
        out_specs=pl.BlockSpec((qt * TM, hp * HEAD_W), lambda b, h, t: (b * n_q + t, h)),
        out_shape=jax.ShapeDtypeStruct((n_b * n_q * qt * TM, D_MODEL), BF16),
        scratch_shapes=[pltpu.VMEM((hp, n_keys, HEAD_W), BF16),
                        pltpu.VMEM((hp, HEAD_W + ONES_ROWS, n_keys), BF16)],
        compiler_params=_params(("arbitrary", "arbitrary", "arbitrary")),
        name="diff_attention_lat" if latent else "diff_attention_ctx",
    )(*args)


def _mix_kernel(ac_ref, al_ref, ug_ref, vg_ref, sa_ref, sb_ref, xp_ref, xs_ref, mod_ref,
                ws_ref, bs_ref, wpa_ref, wpb_ref, wo_ref, g_ref, b_ref, wrc_ref, br_ref,
                x1_ref, h2_ref, ti_ref, tg_ref, cnt_ref, run_ref):
    i = pl.program_id(0)
    is_ctx = i < NT_CTX
    a = jnp.where(is_ctx, ac_ref[...], al_ref[...])
    x = jnp.where(is_ctx, xp_ref[...], xs_ref[...])
    mod = mod_ref[0]
    g1 = mod[:, 2 * D_MODEL:3 * D_MODEL]
    sh2 = mod[:, 3 * D_MODEL:4 * D_MODEL]
    sc2 = mod[:, 4 * D_MODEL:5 * D_MODEL]

    gw = D_MODEL // GMLP_GROUPS
    chunks = []
    for c in range(TM // CHUNK):
        groups = []
        for g in range(GMLP_GROUPS):
            vc = vg_ref[c * CHUNK:(c + 1) * CHUNK, g * gw:(g + 1) * gw]
            groups.append(jnp.dot(ws_ref[g], vc, preferred_element_type=F32))
        chunks.append(jnp.concatenate(groups, axis=1) + bs_ref[...])
    sp = jnp.concatenate(chunks, axis=0)
    gm = (ug_ref[...].astype(F32) * sp).astype(BF16)

    pa = jnp.dot(a, wpa_ref[...], preferred_element_type=F32)
    pb = jnp.dot(gm, wpb_ref[...], preferred_element_type=F32)
    merged = (sa_ref[...].astype(F32) * pa + sb_ref[...].astype(F32) * pb).astype(BF16)
    mix = jnp.dot(merged, wo_ref[...], preferred_element_type=F32)
    x1 = _layer_norm(ALPHA * x + g1 * mix, g_ref[...], b_ref[...])
    x1_ref[...] = x1
    h2 = x1 * (1.0 + sc2) + sh2
    hi, lo = _split(h2)
    h2_ref[...] = _pack_pairs(h2)
    both = jnp.dot(hi, wrc_ref[...], preferred_element_type=F32)
    logits = (both[:, :LANES] + jnp.dot(lo, wrc_ref[:, :LANES], preferred_element_type=F32)
              + both[:, LANES:]) + br_ref[...]
    lane = lax.broadcasted_iota(jnp.int32, (TM, LANES), 1).astype(F32)
    vals, idxs = [], []
    for _ in range(TOP_K):
        mx = jnp.max(logits, axis=-1, keepdims=True)
        ix = jnp.min(jnp.where(logits == mx, lane, float(LANES)), axis=-1, keepdims=True)
        vals.append(mx)
        idxs.append(ix)
        logits = jnp.where(lane == ix, NEG_BIG * 2.0, logits)
    es = [jnp.exp(v - vals[0]) for v in vals]
    inv = 1.0 / (es[0] + es[1] + es[2] + es[3])

    @pl.when(i == 0)
    def _():
        run_ref[...] = jnp.zeros_like(run_ref)

    hot = [lane == ix for ix in idxs]
    memb = jnp.zeros((TM, LANES), F32)
    for hk in hot:
        memb = jnp.where(hk, 1.0, memb)
    row = lax.broadcasted_iota(jnp.int32, (TM, TM), 0)
    col = lax.broadcasted_iota(jnp.int32, (TM, TM), 1)
    before = jnp.where(row > col, 1.0, 0.0).astype(BF16)
    base = run_ref[...]
    rank_all = jnp.dot(before, memb.astype(BF16), preferred_element_type=F32) + base
    total = base + jnp.sum(memb, axis=0, keepdims=True)
    run_ref[...] = total
    cnt_ref[...] = total

    ti = jnp.zeros((TM, LANES), F32)
    tg = jnp.zeros((TM, LANES), F32)
    for k in range(TOP_K):
        rank_k = jnp.sum(jnp.where(hot[k], rank_all, 0.0), axis=-1, keepdims=True)
        ti = jnp.where(lane == float(k), idxs[k], ti)
        ti = jnp.where(lane == float(TOP_K + k), rank_k, ti)
        tg = jnp.where(lane == float(k), es[k] * inv, tg)
    ti_ref[...] = ti.astype(jnp.int32)
    tg_ref[...] = tg


def _mix(a_ctx, a_lat, ug, vg, sa, sb, xp, xs, mod3, ws, bs, wpa, wpb, wo, g, b, wrc, br):
    tile = lambda i: (i, 0)
    full2 = lambda i: (0, 0)
    tok = pl.BlockSpec((TM, D_MODEL), tile)
    ctx = pl.BlockSpec((TM, D_MODEL), lambda i: (_ctx_idx(i), 0))
    lat = pl.BlockSpec((TM, D_MODEL), lambda i: (_lat_idx(i), 0))
    wsq = pl.BlockSpec((D_MODEL, D_MODEL), full2)
    vec = pl.BlockSpec((1, D_MODEL), full2)
    return pl.pallas_call(
        _mix_kernel,
        grid=(NT,),
        in_specs=[ctx, lat, tok, tok, tok, tok, ctx, lat,
                  pl.BlockSpec((1, 1, 6 * D_MODEL), lambda i: (_mod_row(i), 0, 0)),
                  pl.BlockSpec((GMLP_GROUPS, CHUNK, CHUNK), lambda i: (0, 0, 0)),
                  pl.BlockSpec((CHUNK, D_MODEL), full2),
                  wsq, wsq, wsq, vec, vec,
                  pl.BlockSpec((D_MODEL, 2 * LANES), full2),
                  pl.BlockSpec((1, LANES), full2)],
        out_specs=[tok, pl.BlockSpec((TM, PACK_W), tile),
                   pl.BlockSpec((TM, LANES), tile), pl.BlockSpec((TM, LANES), tile),
                   pl.BlockSpec((1, LANES), full2)],
        out_shape=[jax.ShapeDtypeStruct((T_ALL, D_MODEL), F32),
                   jax.ShapeDtypeStruct((T_ALL, PACK_W), jnp.uint32),
                   jax.ShapeDtypeStruct((T_ALL, LANES), jnp.int32),
                   jax.ShapeDtypeStruct((T_ALL, LANES), F32),
                   jax.ShapeDtypeStruct((1, LANES), F32)],
        scratch_shapes=[pltpu.VMEM((1, LANES), F32)],
        compiler_params=_params(("arbitrary",)),
        name="mix_ln1_router",
    )(a_ctx, a_lat, ug, vg, sa, sb, xp, xs, mod3, ws, bs, wpa, wpb, wo, g, b, wrc, br)


def _dispatch_kernel(dest_ref, pend_ref, h_ref, out_hbm, zeros, sem):
    @pl.when(pl.program_id(0) == 0)
    def _():
        zeros[...] = jnp.zeros_like(zeros)

        def fill(start):
            return pltpu.make_async_copy(
                zeros, out_hbm.at[pl.ds(pl.multiple_of(start, MOE_BLOCK), MOE_BLOCK)], sem)

        fills = []
        prev = 0
        for e in range(N_EXPERTS):
            end = pend_ref[e]
            fills.append((end > prev, end - MOE_BLOCK))
            prev = end
        for j in range(N_EXPERTS):
            start = prev + j * MOE_BLOCK
            fills.append((start < out_hbm.shape[0], start))
        for needed, start in fills:
            pl.when(needed)(lambda start=start: fill(start).start())
        for needed, start in fills:
            pl.when(needed)(lambda start=start: fill(start).wait())

    def issue(g, carry):
        base = g * (SUBLANES * TOP_K)
        for u in range(SUBLANES):
            for k in range(TOP_K):
                dest = dest_ref[base + (u * TOP_K + k)]
                pltpu.make_async_copy(h_ref.at[g, pl.ds(u, 1)], out_hbm.at[pl.ds(dest, 1)],
                                      sem).start(priority=k % 2)
        return carry

    lax.fori_loop(0, h_ref.shape[0], issue, 0)
    for _ in range(h_ref.shape[0] * SUBLANES * TOP_K // TM):
        pltpu.make_async_copy(out_hbm.at[pl.ds(0, TM)], out_hbm.at[pl.ds(0, TM)], sem).wait()


def _dispatch(dest_flat, pend, h2p, n_rows):
    return pl.pallas_call(
        _dispatch_kernel,
        grid=(NT // DISPATCH_TILES,),
        in_specs=[pl.BlockSpec((DISPATCH_TILES * ROWS_PER_TILE,), lambda i: (i,),
                               memory_space=pltpu.SMEM),
                  pl.BlockSpec(memory_space=pltpu.SMEM),
                  pl.BlockSpec((DISPATCH_TILES * TM // SUBLANES, SUBLANES, PACK_W),
                               lambda i: (i, 0, 0))],
        out_specs=pl.BlockSpec(memory_space=pl.ANY),
        out_shape=jax.ShapeDtypeStruct((n_rows, PACK_W), jnp.uint32),
        scratch_shapes=[pltpu.VMEM((MOE_BLOCK, PACK_W), jnp.uint32),
                        pltpu.SemaphoreType.DMA(())],
        compiler_params=_params(("arbitrary",)),
        name="moe_dispatch",
    )(dest_flat, pend, h2p.reshape(T_ALL // SUBLANES, SUBLANES, PACK_W))


def _expert_kernel(be_ref, nu_ref, x_ref, wu_ref, bu_ref, wd_ref, bd_ref, y_ref, wu_bf, wd_bf):
    i = pl.program_id(0)
    used = i < nu_ref[0]
    new_expert = jnp.logical_or(i == 0, be_ref[i] != be_ref[jnp.maximum(i - 1, 0)])

    @pl.when(jnp.logical_and(used, new_expert))
    def _():
        wu_bf[...] = wu_ref[0].astype(BF16)
        wd_bf[...] = wd_ref[0].astype(BF16)

    @pl.when(used)
    def _():
        x = _unpack_pairs(x_ref[...]).astype(BF16)
        hu = jnp.dot(x, wu_bf[...], preferred_element_type=F32) + bu_ref[0]
        glu = jnp.minimum(hu[:, :D_EXPERT], SWIGLU_LIMIT)
        lin = jnp.clip(hu[:, D_EXPERT:], -SWIGLU_LIMIT, SWIGLU_LIMIT)
        act = glu * jax.nn.sigmoid(SWIGLU_ALPHA * glu) * (lin + 1.0)
        y = jnp.dot(act.astype(BF16), wd_bf[...], preferred_element_type=F32) + bd_ref[0]
        y_ref[...] = _pack_pairs(y)

    @pl.when(jnp.logical_not(used))
    def _():
        y_ref[...] = jnp.zeros_like(y_ref)


def _experts(blk_exp, n_used, xb, w_up, b_up, w_down, b_down):
    n_blocks = xb.shape[0] // MOE_BLOCK
    grid_spec = pltpu.PrefetchScalarGridSpec(
        num_scalar_prefetch=2,
        grid=(n_blocks,),
        in_specs=[pl.BlockSpec((MOE_BLOCK, PACK_W), lambda i, be, nu: (jnp.where(i < nu[0], i, 0), 0)),
                  pl.BlockSpec((1, D_MODEL, 2 * D_EXPERT), lambda i, be, nu: (be[i], 0, 0)),
                  pl.BlockSpec((1, 1, 2 * D_EXPERT), lambda i, be, nu: (be[i], 0, 0)),
                  pl.BlockSpec((1, D_EXPERT, D_MODEL), lambda i, be, nu: (be[i], 0, 0)),
                  pl.BlockSpec((1, 1, D_MODEL), lambda i, be, nu: (be[i], 0, 0))],
        out_specs=pl.BlockSpec((MOE_BLOCK, PACK_W), lambda i, be, nu: (i, 0)),
        scratch_shapes=[pltpu.VMEM((D_MODEL, 2 * D_EXPERT), BF16),
                        pltpu.VMEM((D_EXPERT, D_MODEL), BF16)],
    )
    return pl.pallas_call(
        _expert_kernel,
        grid_spec=grid_spec,
        out_shape=jax.ShapeDtypeStruct((n_blocks * MOE_BLOCK, PACK_W), jnp.uint32),
        compiler_params=_params(("arbitrary",)),
        name="expert_ffn",
    )(blk_exp, n_used, xb, w_up, b_up, w_down, b_down)


def _final_kernel(dcur_ref, dnxt_ref, x1_ref, tg_ref, mod_ref, g_ref, b_ref, yb_hbm,
                  yp_ref, ys_ref, ybuf, sem):
    i = pl.program_id(0)

    def gather(dest_ref, slot):
        def issue(g, carry):
            base = g * (SUBLANES * TOP_K)
            for u in range(SUBLANES):
                for k in range(TOP_K):
                    dest = dest_ref[base + (u * TOP_K + k)]
                    pltpu.make_async_copy(yb_hbm.at[pl.ds(dest, 1)],
                                          ybuf.at[slot, k, g, pl.ds(u, 1)],
                                          sem.at[slot]).start(priority=k % 2)
            return carry
        lax.fori_loop(0, TM // SUBLANES, issue, 0)

    def wait_rows(slot):
        for k in range(TOP_K):
            pltpu.make_async_copy(yb_hbm.at[pl.ds(0, TM)], yb_hbm.at[pl.ds(0, TM)],
                                  sem.at[slot]).wait()

    def combine(slot):
        g2 = mod_ref[0][:, 5 * D_MODEL:6 * D_MODEL]
        gates = tg_ref[...]
        ffn = jnp.zeros((TM, D_MODEL), F32)
        for k in range(TOP_K):
            yk = ybuf[slot, k].reshape(TM, PACK_W)
            ffn = ffn + gates[:, k:k + 1] * _unpack_pairs(yk)
        y = _layer_norm(ALPHA * x1_ref[...] + g2 * ffn, g_ref[...], b_ref[...])

        @pl.when(i < NT_CTX)
        def _():
            yp_ref[...] = y

        @pl.when(i >= NT_CTX)
        def _():
            ys_ref[...] = y

    @pl.when(i == 0)
    def _():
        gather(dcur_ref, 0)

    for slot in range(2):
        @pl.when(i % 2 == slot)
        def _():
            @pl.when(i + 1 < NT)
            def _():
                gather(dnxt_ref, 1 - slot)
            wait_rows(slot)
            combine(slot)


def _final(dest_flat, x1, top_g, mod3, g, b, ybp):
    tok = pl.BlockSpec((TM, D_MODEL), lambda i: (i, 0))
    vec = pl.BlockSpec((1, D_MODEL), lambda i: (0, 0))
    return pl.pallas_call(
        _final_kernel,
        grid=(NT,),
        in_specs=[pl.BlockSpec((ROWS_PER_TILE,), lambda i: (i,), memory_space=pltpu.SMEM),
                  pl.BlockSpec((ROWS_PER_TILE,), lambda i: (jnp.minimum(i + 1, NT - 1),),
                               memory_space=pltpu.SMEM),
                  tok, pl.BlockSpec((TM, LANES), lambda i: (i, 0)),
                  pl.BlockSpec((1, 1, 6 * D_MODEL), lambda i: (_mod_row(i), 0, 0)), vec, vec,
                  pl.BlockSpec(memory_space=pl.ANY)],
        scratch_shapes=[pltpu.VMEM((2, TOP_K, TM // SUBLANES, SUBLANES, PACK_W), jnp.uint32),
                        pltpu.SemaphoreType.DMA((2,))],
        out_specs=[pl.BlockSpec((TM, D_MODEL), lambda i: (_ctx_idx(i), 0)),
                   pl.BlockSpec((TM, D_MODEL), lambda i: (_lat_idx(i), 0))],
        out_shape=[jax.ShapeDtypeStruct((NT_CTX * TM, D_MODEL), F32),
                   jax.ShapeDtypeStruct((NT_LAT * TM, D_MODEL), F32)],
        compiler_params=_params(("arbitrary",)),
        name="residual_ln2",
    )(dest_flat, dest_flat, x1, top_g, mod3, g, b, ybp)


def _rope_tables():
    t = jnp.arange(LAT_LEN)
    r = (t // GRID_W).astype(F32)
    col = (t % GRID_W).astype(F32)
    inv = jnp.power(ROPE_BASE, -jnp.arange(ROPE_HALF, dtype=F32) / ROPE_HALF)
    ang_r = r[:, None] * inv
    ang_c = col[:, None] * inv
    ang = jnp.concatenate([ang_r, ang_r, ang_c, ang_c], axis=-1)
    cos = jnp.tile(jnp.cos(ang), (1, 2))
    sin = jnp.tile(jnp.sin(ang), (1, 2))
    sign = jnp.where((jnp.arange(LANES) % (2 * ROPE_HALF)) < ROPE_HALF, -1.0, 1.0).astype(F32)
    cos = jnp.concatenate([jnp.ones((TM, LANES), F32), cos], axis=0)
    sin = jnp.concatenate([jnp.zeros((TM, LANES), F32), sin * sign], axis=0)
    return cos, sin


def _routing(top_idx, rank, counts):
    n_tok = top_idx.shape[0]
    n_assign = n_tok * TOP_K
    padded = (counts + MOE_BLOCK - 1) // MOE_BLOCK * MOE_BLOCK
    pend = jnp.cumsum(padded)
    pstart = pend - padded
    experts = jnp.arange(N_EXPERTS, dtype=jnp.int32)
    dest_tk = rank + jnp.sum(jnp.where(top_idx[:, :, None] == experts, pstart, 0), axis=-1)
    n_blocks = -(-n_assign // MOE_BLOCK) + N_EXPERTS
    blk_start = jnp.arange(n_blocks, dtype=jnp.int32) * MOE_BLOCK
    blk_exp = jnp.minimum(jnp.sum((pend[None, :] <= blk_start[:, None]).astype(jnp.int32), axis=1),
                          N_EXPERTS - 1)
    n_used = (pend[-1] // MOE_BLOCK).astype(jnp.int32).reshape(1)
    return blk_exp, dest_tk.reshape(n_assign), n_used, pend.astype(jnp.int32), n_blocks * MOE_BLOCK


def kernel(x_prompt, x_sample, c, cache_k, cache_v, c_ctx, w_ada, b_ada, w_in, lambda_q1, lambda_k1, lambda_q2, lambda_k2, subln_w, gmlp_ln_g, gmlp_ln_b, w_spatial, b_spatial, b_gate, w_pa, w_pb, w_o, ln1_g, ln1_b, w_router, b_router, w_up, b_up, w_down, b_down, ln2_g, ln2_b):
    l = 0
    xp = x_prompt.reshape(N_CTX_B * CTX_LEN, D_MODEL)
    xs = x_sample.reshape(N_LAT_B * LAT_LEN, D_MODEL)

    cond = jnp.concatenate([c_ctx[None, :], c, jnp.zeros((16 - 1 - N_LAT_B, D_MODEL), F32)], axis=0)
    mod3 = _modulation(cond, w_ada[l], b_ada[l][None, :]).reshape(16, 1, 6 * D_MODEL)

    cos_t, sin_t = _rope_tables()
    (q_all, k_all, v_all, new_k, new_v, ug, vg, sa, sb) = _input_projection(
        xp, xs, mod3, w_in[l].astype(BF16), cos_t, sin_t, b_gate[l],
        gmlp_ln_g[l][None, :], gmlp_ln_b[l][None, :])

    lam = (jnp.exp(jnp.sum(lambda_q1[l] * lambda_k1[l])) - jnp.exp(jnp.sum(lambda_q2[l] * lambda_k2[l]))
           + LAM_INIT).reshape(1).astype(F32)
    subw = (subln_w[l] * (1.0 - LAM_INIT))[None, :]
    ctx_k = cache_k[:, l].reshape(N_LAT_B, N_HEADS, PAST_LEN, HEAD_W)
    ctx_v = cache_v[:, l]
    a_ctx = _attention(lam, q_all, k_all, v_all, None, None, subw, latent=False)
    a_lat = _attention(lam, q_all, k_all, v_all, ctx_k, ctx_v, subw, latent=True)

    bs_full = jnp.repeat(b_spatial[l].T, D_MODEL // GMLP_GROUPS, axis=1)
    wr = jnp.pad(w_router[l], ((0, 0), (0, LANES - N_EXPERTS)))
    wr_hi = wr.astype(BF16)
    wr_lo = (wr - wr_hi.astype(F32)).astype(BF16)
    br = jnp.concatenate([b_router[l], jnp.full((LANES - N_EXPERTS,), NEG_BIG, F32)])[None, :]
    x1, h2, top_i, top_g, cnt = _mix(
        a_ctx, a_lat, ug, vg, sa, sb, xp, xs, mod3, w_spatial[l].astype(BF16), bs_full,
        w_pa[l].astype(BF16), w_pb[l].astype(BF16), w_o[l].astype(BF16),
        ln1_g[l][None, :], ln1_b[l][None, :], jnp.concatenate([wr_hi, wr_lo], axis=1), br)

    blk_exp, dest_flat, n_used, pend, n_rows = _routing(
        top_i[:, :TOP_K], top_i[:, TOP_K:2 * TOP_K], cnt[0, :N_EXPERTS].astype(jnp.int32))
    xb = _dispatch(dest_flat, pend, h2, n_rows)
    yb = _experts(blk_exp, n_used, xb, w_up[l], b_up[l][:, None, :], w_down[l], b_down[l][:, None, :])
    y_prompt, y_sample = _final(dest_flat, x1, top_g, mod3, ln2_g[l][None, :], ln2_b[l][None, :], yb)
    return (y_prompt.reshape(N_CTX_B, CTX_LEN, D_MODEL),
            y_sample.reshape(N_LAT_B, LAT_LEN, D_MODEL),
            new_k.reshape(N_CTX_B, DEPTH, N_HEADS, CTX_LEN, 2, HEAD_DIM),
            new_v.reshape(N_CTX_B, DEPTH, N_HEADS, CTX_LEN, HEAD_W))
```
